```python
import math
import jax, jax.numpy as jnp
from jax import lax
import numpy as np

D_MODEL = 1024
BATCH = 16
SEQ = 2048
DEPTH = 2
DEC_BATCH = 128
DEC_SEQ = 1
PAST_LEN = 8192
PAGE_SIZE = 128

D_MIX = D_MODEL
HEAD_A = 64
D_A = D_MIX // 2
H_A = D_A // HEAD_A
W_LORA = 64
A_LORA = 64
D_B = D_MIX // 4
DV_B = 64
H_B = D_B // DV_B
NOPE_B = 64
ROPE_B = 32
Q_RANK = D_MODEL // 4
KV_RANK = D_MODEL // 8
ROPE_THETA = 10000.0
D_C = D_MIX - D_A - D_B
DV_C = 64
H_C = D_C // DV_C
DC = DV_C // 2
NUM_BUCKETS = 32
MAX_DISTANCE = 128
PLE_DIM = 256
Q_BLOCK = 128
NEG_INF = -1e30
EPS = 1e-6
GN_EPS = 64e-5
MLA_SCALE = (NOPE_B + ROPE_B) ** -0.5
DIFF_SCALE = DC ** -0.5
A_IN = 3 * D_A + W_LORA + A_LORA
COL_SIZES = (A_IN, D_A, Q_RANK, KV_RANK, ROPE_B, D_B, D_C, D_C, D_C, D_C)
IN_COLS = sum(COL_SIZES)

kernel_name = 'hybrid_rwkv7_mla_diffattn_step'


def rms_norm(x, g, eps=EPS):
    xf = x.astype(jnp.float32)
    y = xf * lax.rsqrt(jnp.mean(xf * xf, axis=-1, keepdims=True) + eps)
    return (y * g.astype(jnp.float32)).astype(x.dtype)


def rope_cos_sin(pos):
    inv = ROPE_THETA ** (-jnp.arange(0, ROPE_B, 2, dtype=jnp.float32) / ROPE_B)
    ang = pos.astype(jnp.float32)[:, None] * inv[None, :]
    return jnp.cos(ang), jnp.sin(ang)


def apply_rope(x, cos, sin):
    half = ROPE_B // 2
    xf = x.astype(jnp.float32)
    x1, x2 = xf[..., :half], xf[..., half:]
    return jnp.concatenate([x1 * cos - x2 * sin, x1 * sin + x2 * cos], axis=-1).astype(x.dtype)


def t5_bucket(dist):
    n = jnp.maximum(dist, 0)
    max_exact = NUM_BUCKETS // 2
    n_safe = jnp.maximum(n, max_exact).astype(jnp.float32)
    large = max_exact + (jnp.log(n_safe / max_exact) / math.log(MAX_DISTANCE / max_exact)
                         * (NUM_BUCKETS - max_exact)).astype(jnp.int32)
    large = jnp.minimum(large, NUM_BUCKETS - 1)
    return jnp.where(n < max_exact, n, large)


def over_query_blocks(fn, q_pos, *q_args):
    T = q_pos.shape[0]
    if T <= Q_BLOCK or T % Q_BLOCK:
        return fn(q_pos, *q_args)
    nb = T // Q_BLOCK
    split = lambda a: jnp.moveaxis(a.reshape((a.shape[0], nb, Q_BLOCK) + a.shape[2:]), 1, 0)
    out = lax.map(lambda args: fn(*args), (q_pos.reshape(nb, Q_BLOCK),) + tuple(split(a) for a in q_args))
    out = jnp.moveaxis(out, 0, 1)
    return out.reshape((out.shape[0], T) + out.shape[3:])


def wkv7_scan(s0, r, decay, k, v, kk, a):
    def step(s, inp):
        r_t, d_t, k_t, v_t, kk_t, a_t = inp
        sa = jnp.einsum('bhij,bhj->bhi', s, -kk_t)
        s = (s * d_t[:, :, None, :] + sa[..., None] * (kk_t * a_t)[:, :, None, :]
             + v_t[..., None] * k_t[:, :, None, :])
        return s, jnp.einsum('bhij,bhj->bhi', s, r_t)
    xs = tuple(jnp.moveaxis(t, 1, 0) for t in (r, decay, k, v, kk, a))
    s, ys = lax.scan(step, s0, xs)
    return s, jnp.moveaxis(ys, 0, 1)


def rwkv7_branch(u, s0, w0, w2, a0, a2, k_k, k_a, r_k, gn_g, gn_b):
    f32 = jnp.float32
    B, T, _ = u.shape
    r, k, v, w_lo, a_lo = jnp.split(u.astype(f32), [D_A, 2 * D_A, 3 * D_A, 3 * D_A + W_LORA], axis=-1)
    heads = lambda t: t.reshape(B, T, H_A, HEAD_A)
    w = -jax.nn.softplus(-(w0.astype(f32) + jnp.tanh(w_lo) @ w2.astype(f32))) - 0.5
    decay = jnp.exp(-jnp.exp(w))
    a = jax.nn.sigmoid(a0.astype(f32) + a_lo @ a2.astype(f32))
    kk = heads(k * k_k.astype(f32))
    kk = kk / jnp.maximum(jnp.sqrt(jnp.sum(kk * kk, axis=-1, keepdims=True)), 1e-12)
    k = k * (1.0 + (a - 1.0) * k_a.astype(f32))
    s_new, y = wkv7_scan(s0.astype(f32), heads(r), heads(decay), heads(k), heads(v), kk, heads(a))
    mu = jnp.mean(y, axis=-1, keepdims=True)
    var = jnp.mean(jnp.square(y - mu), axis=-1, keepdims=True)
    y = ((y - mu) * lax.rsqrt(var + GN_EPS)).reshape(B, T, D_A) * gn_g.astype(f32) + gn_b.astype(f32)
    bonus = jnp.sum(heads(r * k) * r_k.astype(f32), axis=-1, keepdims=True) * heads(v)
    y = y + bonus.reshape(B, T, D_A)
    return y.astype(u.dtype), s_new.astype(s0.dtype)


def mla_branch(c_q, c_kv, k_r, past, q_pos, k_pos, cos, sin, q_norm_g, w_uq, kv_norm_g, w_uk, w_uv):
    B, T, _ = c_q.shape
    q = (rms_norm(c_q, q_norm_g) @ w_uq).reshape(B, T, H_B, NOPE_B + ROPE_B)
    q_nope = q[..., :NOPE_B]
    q_rope = apply_rope(q[..., NOPE_B:], cos[:, None], sin[:, None])
    rows = jnp.concatenate([rms_norm(c_kv, kv_norm_g), apply_rope(k_r, cos, sin)], axis=-1)
    keys = jnp.concatenate([past.astype(rows.dtype), rows], axis=1)
    c, kr = keys[..., :KV_RANK], keys[..., KV_RANK:]
    q_lat = jnp.einsum('bthn,rhn->bthr', q_nope, w_uk)

    def block(qp, ql, qr):
        s = (jnp.einsum('bqhr,bkr->bhqk', ql, c) + jnp.einsum('bqhe,bke->bhqk', qr, kr)).astype(jnp.float32) * MLA_SCALE
        s = jnp.where(k_pos[None, :] <= qp[:, None], s, NEG_INF)
        pr = jax.nn.softmax(s, axis=-1).astype(c.dtype)
        return jnp.einsum('bhqk,bkr->bqhr', pr, c)

    o_lat = over_query_blocks(block, q_pos, q_lat, q_rope)
    o = jnp.einsum('bthr,rhd->bthd', o_lat, w_uv).reshape(B, T, D_B)
    return o, rows


def diff_branch(q_c, k_c, v_c, past_k, past_v, q_pos, k_pos, rel_bias, lq1, lk1, lq2, lk2, subln_g, lam_init):
    B, T, _ = q_c.shape
    q = q_c.reshape(B, T, H_C, 2, DC)
    k_rows = k_c.reshape(B, T, H_C, 2 * DC)
    v_rows = v_c.reshape(B, T, H_C, DV_C)
    keys = jnp.concatenate([past_k.astype(k_rows.dtype), k_rows], axis=1)
    keys = keys.reshape(B, keys.shape[1], H_C, 2, DC)
    vals = jnp.concatenate([past_v.astype(v_rows.dtype), v_rows], axis=1)
    f32 = jnp.float32
    lam = (jnp.exp(jnp.sum(lq1.astype(f32) * lk1.astype(f32))) - jnp.exp(jnp.sum(lq2.astype(f32) * lk2.astype(f32)))
           + lam_init)

    def block(qp, qb):
        s = jnp.einsum('bqhcd,bkhcd->bhcqk', qb, keys).astype(f32) * DIFF_SCALE
        dist = qp[:, None] - k_pos[None, :]
        bias = jnp.moveaxis(rel_bias[t5_bucket(dist)], -1, 0).astype(f32)
        s = jnp.where(dist >= 0, s + bias[None, :, None], NEG_INF)
        pr = jax.nn.softmax(s, axis=-1)
        attn = (pr[:, :, 0] - lam * pr[:, :, 1]).astype(vals.dtype)
        return jnp.einsum('bhqk,bkhd->bqhd', attn, vals)

    o = over_query_blocks(block, q_pos, q)
    o = rms_norm(o, subln_g, eps=1e-5) * (1.0 - lam_init)
    return o.reshape(B, T, D_C), k_rows, v_rows


def run_group(x, p, shift0, wkv0, past_fn, past_len, W):
    B, T, _ = x.shape
    q_pos = past_len + jnp.arange(T, dtype=jnp.int32)
    k_pos = jnp.arange(past_len + T, dtype=jnp.int32)
    cos, sin = rope_cos_sin(q_pos)
    offs = np.cumsum(COL_SIZES)[:-1].tolist()
    h = x
    mla_rows, k_rows, v_rows, wkv_out, shift_out = [], [], [], [], []
    for l in range(DEPTH):
        past_mla, past_k, past_v = past_fn(l)
        xn = rms_norm(h, W['norm_g'][l])
        w_in = W['w_in'][l]
        u, g_a, c_q, c_kv, k_r, g_b, q_c, k_c, v_c, g_c = jnp.split(xn @ w_in, offs, axis=-1)
        u_first_prev = shift0[l].astype(xn.dtype) @ w_in[:, :A_IN]
        u_prev = jnp.concatenate([u_first_prev[:, None], u[:, :-1]], axis=1)
        u = u + W['mu_shift'][l] * (u_prev - u)
        y_a, s_new = rwkv7_branch(u, wkv0[l], W['rw_w0'][l], W['rw_w2'][l], W['rw_a0'][l], W['rw_a2'][l],
                                  W['rw_k_k'][l], W['rw_k_a'][l], W['rw_r_k'][l], W['rw_gn_g'][l], W['rw_gn_b'][l])
        y_b, rows_b = mla_branch(c_q, c_kv, k_r, past_mla, q_pos, k_pos, cos, sin, W['mla_q_norm_g'][l],
                                 W['mla_w_uq'][l], W['mla_kv_norm_g'][l], W['mla_w_uk'][l], W['mla_w_uv'][l])
        y_c, kr_c, vr_c = diff_branch(q_c, k_c, v_c, past_k, past_v, q_pos, k_pos, W['rel_bias'],
                                      W['diff_lam_q1'][l], W['diff_lam_k1'][l], W['diff_lam_q2'][l],
                                      W['diff_lam_k2'][l], W['diff_subln_g'][l], 0.8 - 0.6 * math.exp(-0.3 * l))
        mix = jnp.concatenate([y_a * jax.nn.silu(g_a), y_b * jax.nn.silu(g_b), y_c * jax.nn.silu(g_c)], axis=-1)
        h = h + mix @ W['w_out'][l]
        h = h + (p[l] @ W['w_ple'][l]) * jax.nn.sigmoid(h @ W['w_ple_gate'][l])
        mla_rows.append(rows_b)
        k_rows.append(kr_c)
        v_rows.append(vr_c)
        wkv_out.append(s_new)
        shift_out.append(xn[:, -1])
    y = rms_norm(h, W['final_norm_g'])
    return (y, jnp.stack(mla_rows), jnp.stack(k_rows), jnp.stack(v_rows), jnp.stack(wkv_out), jnp.stack(shift_out))


def setup_inputs(seed: int = 0) -> dict:
    f32 = jnp.float32
    key = jax.random.key(seed)
    ks = iter(jax.random.split(key, 64))
    nrm = lambda shape, scale=1.0: scale * jax.random.normal(next(ks), shape, f32)
    gain = lambda shape: 1.0 + nrm(shape, 0.02)
    n_pages = PAST_LEN // PAGE_SIZE
    n_pool = (DEC_BATCH * n_pages * 5) // 4
    perm = jax.random.permutation(next(ks), n_pool)
    page_table = perm[:DEC_BATCH * n_pages].reshape(DEC_BATCH, n_pages).astype(jnp.int32)
    return {
        'x_prompt': nrm((BATCH, SEQ, D_MODEL)),
        'x_sample': nrm((DEC_BATCH, DEC_SEQ, D_MODEL)),
        'cache_mla': nrm((DEPTH, n_pool, PAGE_SIZE, KV_RANK + ROPE_B)),
        'cache_diff_k': nrm((DEPTH, n_pool, PAGE_SIZE, H_C, 2 * DC)),
        'cache_diff_v': nrm((DEPTH, n_pool, PAGE_SIZE, H_C, DV_C)),
        'state_wkv': nrm((DEPTH, DEC_BATCH, H_A, HEAD_A, HEAD_A), 0.3),
        'state_shift': nrm((DEPTH, DEC_BATCH, D_MODEL)),
        'page_table': page_table,
        'p_prompt': nrm((DEPTH, BATCH, SEQ, PLE_DIM)),
        'p_sample': nrm((DEPTH, DEC_BATCH, DEC_SEQ, PLE_DIM)),
        'norm_g': gain((DEPTH, D_MODEL)),
        'w_in': nrm((DEPTH, D_MODEL, IN_COLS), D_MODEL ** -0.5),
        'mu_shift': jax.random.uniform(next(ks), (DEPTH, A_IN), f32),
        'rw_w0': jax.random.uniform(next(ks), (DEPTH, D_A), f32, minval=-4.0, maxval=0.0),
        'rw_w2': nrm((DEPTH, W_LORA, D_A), 0.5 * W_LORA ** -0.5),
        'rw_a0': nrm((DEPTH, D_A), 0.5),
        'rw_a2': nrm((DEPTH, A_LORA, D_A), A_LORA ** -0.5),
        'rw_k_k': 0.85 + nrm((DEPTH, D_A), 0.05),
        'rw_k_a': 1.0 + nrm((DEPTH, D_A), 0.05),
        'rw_r_k': nrm((DEPTH, H_A, HEAD_A), 0.1),
        'rw_gn_g': gain((DEPTH, D_A)),
        'rw_gn_b': nrm((DEPTH, D_A), 0.01),
        'mla_q_norm_g': gain((DEPTH, Q_RANK)),
        'mla_w_uq': nrm((DEPTH, Q_RANK, H_B * (NOPE_B + ROPE_B)), Q_RANK ** -0.5),
        'mla_kv_norm_g': gain((DEPTH, KV_RANK)),
        'mla_w_uk': nrm((DEPTH, KV_RANK, H_B, NOPE_B), KV_RANK ** -0.5),
        'mla_w_uv': nrm((DEPTH, KV_RANK, H_B, DV_B), KV_RANK ** -0.5),
        'diff_lam_q1': nrm((DEPTH, DC), 0.1),
        'diff_lam_k1': nrm((DEPTH, DC), 0.1),
        'diff_lam_q2': nrm((DEPTH, DC), 0.1),
        'diff_lam_k2': nrm((DEPTH, DC), 0.1),
        'diff_subln_g': gain((DEPTH, DV_C)),
        'rel_bias': nrm((NUM_BUCKETS, H_C), 0.5),
        'w_out': nrm((DEPTH, D_MIX, D_MODEL), 0.5 * D_MIX ** -0.5),
        'w_ple': nrm((DEPTH, PLE_DIM, D_MODEL), 0.5 * PLE_DIM ** -0.5),
        'w_ple_gate': nrm((DEPTH, D_MODEL, D_MODEL), D_MODEL ** -0.5),
        'final_norm_g': gain((D_MODEL,)),
    }


def reference(x_prompt, x_sample, cache_mla, cache_diff_k, cache_diff_v, state_wkv, state_shift, page_table,
              p_prompt, p_sample, norm_g, w_in, mu_shift, rw_w0, rw_w2, rw_a0, rw_a2, rw_k_k, rw_k_a, rw_r_k,
              rw_gn_g, rw_gn_b, mla_q_norm_g, mla_w_uq, mla_kv_norm_g, mla_w_uk, mla_w_uv, diff_lam_q1,
              diff_lam_k1, diff_lam_q2, diff_lam_k2, diff_subln_g, rel_bias, w_out, w_ple, w_ple_gate,
              final_norm_g):
    W = {'norm_g': norm_g, 'w_in': w_in, 'mu_shift': mu_shift, 'rw_w0': rw_w0, 'rw_w2': rw_w2, 'rw_a0': rw_a0,
         'rw_a2': rw_a2, 'rw_k_k': rw_k_k, 'rw_k_a': rw_k_a, 'rw_r_k': rw_r_k, 'rw_gn_g': rw_gn_g,
         'rw_gn_b': rw_gn_b, 'mla_q_norm_g': mla_q_norm_g, 'mla_w_uq': mla_w_uq, 'mla_kv_norm_g': mla_kv_norm_g,
         'mla_w_uk': mla_w_uk, 'mla_w_uv': mla_w_uv, 'diff_lam_q1': diff_lam_q1, 'diff_lam_k1': diff_lam_k1,
         'diff_lam_q2': diff_lam_q2, 'diff_lam_k2': diff_lam_k2, 'diff_subln_g': diff_subln_g,
         'rel_bias': rel_bias, 'w_out': w_out, 'w_ple': w_ple, 'w_ple_gate': w_ple_gate,
         'final_norm_g': final_norm_g}
    dt = x_prompt.dtype
    bp = x_prompt.shape[0]
    bs = x_sample.shape[0]

    def prompt_past(l):
        return (jnp.zeros((bp, 0, KV_RANK + ROPE_B), dt), jnp.zeros((bp, 0, H_C, 2 * DC), dt),
                jnp.zeros((bp, 0, H_C, DV_C), dt))

    n_pages = page_table.shape[1]
    past_len = n_pages * PAGE_SIZE

    def sample_past(l):
        gather = lambda c: c[l][page_table].reshape((bs, past_len) + c.shape[3:])
        return gather(cache_mla), gather(cache_diff_k), gather(cache_diff_v)

    shift0_p = jnp.zeros((DEPTH, bp, D_MODEL), dt)
    wkv0_p = jnp.zeros((DEPTH, bp, H_A, HEAD_A, HEAD_A), state_wkv.dtype)
    y_prompt, mla_p, dk_p, dv_p, wkv_p, sh_p = run_group(x_prompt, p_prompt, shift0_p, wkv0_p, prompt_past, 0, W)
    y_sample, mla_s, dk_s, dv_s, wkv_s, sh_s = run_group(x_sample, p_sample, state_shift, state_wkv, sample_past,
                                                         past_len, W)
    return (y_prompt, y_sample, mla_p, mla_s, dk_p, dk_s, dv_p, dv_s, wkv_p, wkv_s, sh_p, sh_s)
```

```python
import functools
import math

import jax
import jax.numpy as jnp
from jax import lax
from jax.experimental import pallas as pl
from jax.experimental.pallas import tpu as pltpu

F32 = jnp.float32
BF16 = jnp.bfloat16
HIGHEST = lax.Precision.HIGHEST

LANES = 128
SUBLANES = 8
VMEM_LIMIT_BYTES = 56 * 1024 * 1024

D_MODEL = 1024
HEAD_A = 64
D_A = 512
H_A = D_A // HEAD_A
W_LORA = 64
A_LORA = 64
A_IN = 3 * D_A + W_LORA + A_LORA
D_B = 256
DV_B = 64
H_B = D_B // DV_B
NOPE_B = 64
ROPE_B = 32
Q_RANK = 256
KV_RANK = 128
MLA_W = KV_RANK + ROPE_B
ROPE_THETA = 10000.0
D_C = 256
DV_C = 64
H_C = D_C // DV_C
DC = DV_C // 2
NUM_BUCKETS = 32
MAX_DISTANCE = 128
PLE_DIM = 256
PAGE_SIZE = 128
NEG_INF = -1e30
EPS = 1e-6
GN_EPS = 64e-5
SUBLN_EPS = 1e-5
MLA_SCALE = (NOPE_B + ROPE_B) ** -0.5
DIFF_SCALE = DC ** -0.5

SEG_U = (0, A_IN)
SEG_GA = (A_IN, A_IN + D_A)
SEG_CQ = (SEG_GA[1], SEG_GA[1] + Q_RANK)
SEG_CKV = (SEG_CQ[1], SEG_CQ[1] + 2 * LANES)
SEG_GB = (SEG_CKV[1], SEG_CKV[1] + D_B)
SEG_QC = (SEG_GB[1], SEG_GB[1] + D_C)
SEG_KC = (SEG_QC[1], SEG_QC[1] + D_C)
SEG_VC = (SEG_KC[1], SEG_KC[1] + D_C)
SEG_GC = (SEG_VC[1], SEG_VC[1] + D_C)
ALL_SEGS = (SEG_U, SEG_GA, SEG_CQ, SEG_CKV, SEG_GB, SEG_QC, SEG_KC, SEG_VC, SEG_GC)
IN_COLS_PERM = SEG_GC[1]

RWKV_CHUNK = 64
ATTN_TILE = 256
DECODE_PAGES_PER_STEP = 8


def _cparams(semantics):
    return pltpu.CompilerParams(dimension_semantics=semantics, vmem_limit_bytes=VMEM_LIMIT_BYTES)


def _full(shape):
    n = len(shape)
    return pl.BlockSpec(shape, lambda *_: (0,) * n)


def _sigmoid(x):
    return 1.0 / (1.0 + jnp.exp(-x))


def _silu(x):
    return x * _sigmoid(x)


def _rms(x, g, eps):
    return x * lax.rsqrt(jnp.mean(x * x, axis=-1, keepdims=True) + eps) * g


def _dot(a, b, **kw):
    return jnp.dot(a, b, preferred_element_type=F32, **kw)


def _dot_nt(a, b, **kw):
    return lax.dot_general(a, b, (((1,), (1,)), ((), ())), preferred_element_type=F32, **kw)


def _dot_tn(a, b, **kw):
    return lax.dot_general(a, b, (((0,), (0,)), ((), ())), preferred_element_type=F32, **kw)


def _inproj_kernel(h_ref, g_ref, w_ref, *out_refs, normalize, segs, emit_xn):
    x = h_ref[...]
    xn = _rms(x, g_ref[...], EPS) if normalize else x
    xb = xn.astype(BF16)
    for o_ref, (a, b) in zip(out_refs, segs):
        o_ref[...] = _dot(xb, w_ref[:, a:b])
    if emit_xn == "last_row":
        rows = x.shape[0]
        out_refs[len(segs)][...] = xn[rows - 1:rows, :]
    elif emit_xn == "all":
        out_refs[len(segs)][...] = xn


def _inproj(h2d, norm_g, w_bf16, segs, *, normalize, rows_per_seq, tm):
    m, d = h2d.shape
    assert m % tm == 0
    out_shapes = [jax.ShapeDtypeStruct((m, b - a), F32) for a, b in segs]
    out_specs = [pl.BlockSpec((tm, b - a), lambda i: (i, 0)) for a, b in segs]
    emit_xn = None
    if normalize:
        if rows_per_seq == 1:
            emit_xn = "all"
            out_shapes.append(jax.ShapeDtypeStruct((m, d), F32))
            out_specs.append(pl.BlockSpec((tm, d), lambda i: (i, 0)))
        else:
            assert rows_per_seq % tm == 0
            tiles_per_seq = rows_per_seq // tm
            emit_xn = "last_row"
            out_shapes.append(jax.ShapeDtypeStruct((m // rows_per_seq, 1, d), F32))
            out_specs.append(pl.BlockSpec((None, 1, d), lambda i: (i // tiles_per_seq, 0, 0)))
    kern = functools.partial(_inproj_kernel, normalize=normalize, segs=segs, emit_xn=emit_xn)
    return pl.pallas_call(
        kern,
        grid=(m // tm,),
        in_specs=[pl.BlockSpec((tm, d), lambda i: (i, 0)), _full((1, d)), _full(w_bf16.shape)],
        out_specs=out_specs,
        out_shape=out_shapes,
        compiler_params=_cparams(("arbitrary",)),
        name="inproj",
    )(h2d, norm_g.reshape(1, d), w_bf16)


def _outproj_kernel(h_ref, ya_ref, yb_ref, yc_ref, p_ref, wo_ref, wple_ref, wg_ref, fng_ref, o_ref, *, final):
    mixed = (_dot(ya_ref[...], wo_ref[0:D_A, :])
             + _dot(yb_ref[...], wo_ref[D_A:D_A + D_B, :])
             + _dot(yc_ref[...], wo_ref[D_A + D_B:, :]))
    h2 = h_ref[...] + mixed
    ple = _dot(p_ref[...].astype(BF16), wple_ref[...])
    gate = _sigmoid(_dot(h2.astype(BF16), wg_ref[...]))
    h3 = h2 + ple * gate
    o_ref[...] = _rms(h3, fng_ref[...], EPS) if final else h3


def _outproj(h2d, ya, yb, yc, p2d, wo, wple, wg, final_g, *, final, tm):
    m, d = h2d.shape
    row = lambda w: pl.BlockSpec((tm, w), lambda i: (i, 0))
    return pl.pallas_call(
        functools.partial(_outproj_kernel, final=final),
        grid=(m // tm,),
        in_specs=[row(d), row(D_A), row(D_B), row(D_C), row(PLE_DIM),
                  _full(wo.shape), _full(wple.shape), _full(wg.shape), _full((1, d))],
        out_specs=row(d),
        out_shape=jax.ShapeDtypeStruct((m, d), F32),
        compiler_params=_cparams(("arbitrary",)),
        name="outproj",
    )(h2d, ya, yb, yc, p2d, wo, wple, wg, final_g.reshape(1, d))


def _rwkv_prep(um, w0, w2, a0, a2, k_k, k_a):
    r = um[:, 0:D_A]
    k = um[:, D_A:2 * D_A]
    v = um[:, 2 * D_A:3 * D_A]
    w_lo = um[:, 3 * D_A:3 * D_A + W_LORA]
    a_lo = um[:, 3 * D_A + W_LORA:A_IN]
    wl = w0 + _dot(jnp.tanh(w_lo), w2, precision=HIGHEST)
    neg = -wl
    softplus = jnp.maximum(neg, 0.0) + jnp.log(1.0 + jnp.exp(-jnp.abs(neg)))
    w = -softplus - 0.5
    log_decay = -jnp.exp(w)
    a = _sigmoid(a0 + _dot(a_lo, a2, precision=HIGHEST))
    kk = k * k_k
    k = k * (1.0 + (a - 1.0) * k_a)
    return r, k, v, kk, a, log_decay


def _normalize_kk(kk_h):
    norm = jnp.sqrt(jnp.sum(kk_h * kk_h, axis=-1, keepdims=True))
    return kk_h / jnp.maximum(norm, 1e-12)


def _rwkv_head_out(y, r_h, k_h, v_h, rk_h, gng_h, gnb_h, gate_h):
    mu = jnp.mean(y, axis=-1, keepdims=True)
    var = jnp.mean(jnp.square(y - mu), axis=-1, keepdims=True)
    yn = (y - mu) * lax.rsqrt(var + GN_EPS) * gng_h + gnb_h
    bonus = jnp.sum(r_h * k_h * rk_h, axis=-1, keepdims=True) * v_h
    return (yn + bonus) * _silu(gate_h)


def _rwkv_chunk_kernel(u_ref, uprev0_ref, ga_ref, s0_ref, mu_ref, w0_ref, w2_ref, a0_ref, a2_ref,
                       kk_ref, ka_ref, rk_ref, gng_ref, gnb_ref, y_ref, s_ref, prev_ref):
    c = pl.program_id(1)
    C = u_ref.shape[0]

    @pl.when(c == 0)
    def _():
        prev_ref[...] = uprev0_ref[...]
        s_ref[...] = s0_ref[...]

    u = u_ref[...]
    row = lax.broadcasted_iota(jnp.int32, (C, 1), 0)
    u_prev = jnp.where(row == 0, prev_ref[...], pltpu.roll(u, 1, axis=0))
    prev_ref[...] = u[C - 1:C, :]
    um = u + mu_ref[...] * (u_prev - u)
    r, k, v, kk, a, log_decay = _rwkv_prep(um, w0_ref[...], w2_ref[...], a0_ref[...], a2_ref[...],
                                           kk_ref[...], ka_ref[...])

    ti = lax.broadcasted_iota(jnp.int32, (C, C), 0)
    tj = lax.broadcasted_iota(jnp.int32, (C, C), 1)
    incl = tj <= ti
    strict = tj < ti
    cs = _dot(incl.astype(F32), log_decay, precision=HIGHEST)
    cs_end = cs[C - 1:C, :]
    e_incl = jnp.exp(cs)
    e_excl = jnp.exp(cs - log_decay)
    e_inv = jnp.exp(-cs)
    e_rem = jnp.exp(cs_end - cs)
    p_end = jnp.exp(cs_end)
    eye = (ti == tj).astype(F32)
    ga = ga_ref[...]

    for h in range(H_A):
        hs = slice(h * HEAD_A, (h + 1) * HEAD_A)
        kk_h = _normalize_kk(kk[:, hs])
        r_h, k_h, v_h = r[:, hs], k[:, hs], v[:, hs]
        b_h = kk_h * a[:, hs]
        a_t = -kk_h * e_excl[:, hs]
        r_t = r_h * e_incl[:, hs]
        b_t = b_h * e_inv[:, hs]
        k_t = k_h * e_inv[:, hs]
        b_end = b_h * e_rem[:, hs]
        k_end = k_h * e_rem[:, hs]

        l_ab = jnp.where(strict, _dot_nt(a_t, b_t, precision=HIGHEST), 0.0)
        l_ak = jnp.where(strict, _dot_nt(a_t, k_t, precision=HIGHEST), 0.0)
        m_rb = jnp.where(incl, _dot_nt(r_t, b_t, precision=HIGHEST), 0.0)
        m_rk = jnp.where(incl, _dot_nt(r_t, k_t, precision=HIGHEST), 0.0)

        inv = eye + l_ab
        pw = l_ab
        for _ in range(int(math.log2(C)) - 1):
            pw = _dot(pw, pw, precision=HIGHEST)
            inv = inv + _dot(inv, pw, precision=HIGHEST)

        w_mat = _dot(inv, a_t, precision=HIGHEST)
        u_v = _dot(inv, _dot(l_ak, v_h, precision=HIGHEST), precision=HIGHEST)
        q_eff = r_t + _dot(m_rb, w_mat, precision=HIGHEST)
        y_v = _dot(m_rb, u_v, precision=HIGHEST) + _dot(m_rk, v_h, precision=HIGHEST)

        s_old = s_ref[h]
        y = _dot_nt(q_eff, s_old, precision=HIGHEST) + y_v
        trans = _dot_tn(w_mat, b_end, precision=HIGHEST)
        kr = lax.broadcasted_iota(jnp.int32, (HEAD_A, HEAD_A), 0)
        kc = lax.broadcasted_iota(jnp.int32, (HEAD_A, HEAD_A), 1)
        trans = trans + jnp.where(kr == kc, p_end[:, hs], 0.0)
        add = _dot_tn(u_v, b_end, precision=HIGHEST) + _dot_tn(v_h, k_end, precision=HIGHEST)
        s_ref[h] = _dot(s_old, trans, precision=HIGHEST) + add

        out = _rwkv_head_out(y, r_h, k_h, v_h, rk_ref[:, hs], gng_ref[:, hs], gnb_ref[:, hs], ga[:, hs])
        y_ref[:, hs] = out.astype(y_ref.dtype)


def _rwkv_chunked(u, uprev0, ga, s0, params):
    b, t, _ = u.shape
    c = RWKV_CHUNK
    assert t % c == 0
    tok = lambda w: pl.BlockSpec((None, c, w), lambda i, j: (i, j, 0))
    state = pl.BlockSpec((None, H_A, HEAD_A, HEAD_A), lambda i, j: (i, 0, 0, 0))
    return pl.pallas_call(
        _rwkv_chunk_kernel,
        grid=(b, t // c),
        in_specs=[tok(A_IN), pl.BlockSpec((None, 1, A_IN), lambda i, j: (i, 0, 0)), tok(D_A), state]
                 + [_full(p.shape) for p in params],
        out_specs=[tok(D_A), state],
        out_shape=[jax.ShapeDtypeStruct((b, t, D_A), BF16), jax.ShapeDtypeStruct(s0.shape, F32)],
        scratch_shapes=[pltpu.VMEM((1, A_IN), F32)],
        compiler_params=_cparams(("parallel", "arbitrary")),
        name="rwkv_chunked",
    )(u, uprev0, ga, s0, *params)


def _rwkv_step_kernel(u_ref, uprev_ref, ga_ref, s0_ref, mu_ref, w0_ref, w2_ref, a0_ref, a2_ref,
                      kk_ref, ka_ref, rk_ref, gng_ref, gnb_ref, y_ref, s_ref):
    nb = u_ref.shape[0]
    u = u_ref[...]
    um = u + mu_ref[...] * (uprev_ref[...] - u)
    r, k, v, kk, a, log_decay = _rwkv_prep(um, w0_ref[...], w2_ref[...], a0_ref[...], a2_ref[...],
                                           kk_ref[...], ka_ref[...])
    decay = jnp.exp(log_decay)
    ga = ga_ref[...]
    ri = lax.broadcasted_iota(jnp.int32, (HEAD_A, HEAD_A), 0)
    ci = lax.broadcasted_iota(jnp.int32, (HEAD_A, HEAD_A), 1)
    eye = ri == ci
    for h in range(H_A):
        hs = slice(h * HEAD_A, (h + 1) * HEAD_A)
        kk_h = _normalize_kk(kk[:, hs])
        b_h = kk_h * a[:, hs]
        outs = []
        for i in range(nb):
            rs = slice(i, i + 1)
            s = s0_ref[i, h]
            sa = jnp.sum(s * (-kk_h[rs]), axis=-1, keepdims=True)
            v_col = jnp.sum(jnp.where(eye, v[rs, hs], 0.0), axis=-1, keepdims=True)
            s_new = s * decay[rs, hs] + sa * b_h[rs] + v_col * k[rs, hs]
            s_ref[i, h] = s_new
            y_col = jnp.sum(s_new * r[rs, hs], axis=-1, keepdims=True)
            outs.append(jnp.sum(jnp.where(eye, y_col, 0.0), axis=0, keepdims=True))
        y = jnp.concatenate(outs, axis=0)
        out = _rwkv_head_out(y, r[:, hs], k[:, hs], v[:, hs], rk_ref[:, hs], gng_ref[:, hs],
                             gnb_ref[:, hs], ga[:, hs])
        y_ref[:, hs] = out.astype(y_ref.dtype)


def _rwkv_step(u, uprev, ga, s0, params, *, nb=SUBLANES):
    b = u.shape[0]
    assert b % nb == 0
    row = lambda w: pl.BlockSpec((nb, w), lambda i: (i, 0))
    state = pl.BlockSpec((nb, H_A, HEAD_A, HEAD_A), lambda i: (i, 0, 0, 0))
    return pl.pallas_call(
        _rwkv_step_kernel,
        grid=(b // nb,),
        in_specs=[row(A_IN), row(A_IN), row(D_A), state] + [_full(p.shape) for p in params],
        out_specs=[row(D_A), state],
        out_shape=[jax.ShapeDtypeStruct((b, D_A), BF16), jax.ShapeDtypeStruct(s0.shape, F32)],
        compiler_params=_cparams(("parallel",)),
        name="rwkv_step",
    )(u, uprev, ga, s0, *params)


def _mla_prep_kernel(cq_ref, ckv_ref, cos_ref, sin_ref, qg_ref, wuq_ref, kvg_ref, wukt_ref, q_ref, rows_ref):
    cos2 = cos_ref[...]
    sin2 = sin_ref[...]
    qn = _rms(cq_ref[...], qg_ref[...], EPS).astype(BF16)
    q = _dot(qn, wuq_ref[...])
    for h in range(H_B):
        qh = q[:, h * LANES:(h + 1) * LANES]
        q_lat = _dot(qh[:, :NOPE_B].astype(BF16), wukt_ref[h])
        q_rope = qh[:, NOPE_B:NOPE_B + ROPE_B] * cos2 + qh[:, NOPE_B + ROPE_B:] * sin2
        q_ref[h, :, 0:KV_RANK] = (q_lat * MLA_SCALE).astype(q_ref.dtype)
        q_ref[h, :, KV_RANK:MLA_W] = (q_rope * MLA_SCALE).astype(q_ref.dtype)
    ckv = ckv_ref[...]
    rows_ref[:, 0:KV_RANK] = _rms(ckv[:, 0:KV_RANK], kvg_ref[...], EPS)
    rows_ref[:, KV_RANK:MLA_W] = (ckv[:, KV_RANK:KV_RANK + ROPE_B] * cos2
                                  + ckv[:, KV_RANK + ROPE_B:KV_RANK + 2 * ROPE_B] * sin2)


def _mla_prep(cq, ckv, cos2, sin2, q_norm_g, wuq_ext, kv_norm_g, wuk_t, *, tm, pos_tiles):
    m = cq.shape[0]
    row = lambda w: pl.BlockSpec((tm, w), lambda i: (i, 0))
    pos = pl.BlockSpec((tm, ROPE_B), lambda i: (i % pos_tiles, 0))
    return pl.pallas_call(
        _mla_prep_kernel,
        grid=(m // tm,),
        in_specs=[row(Q_RANK), row(2 * LANES), pos, pos, _full((1, Q_RANK)), _full(wuq_ext.shape),
                  _full((1, KV_RANK)), _full(wuk_t.shape)],
        out_specs=[pl.BlockSpec((H_B, tm, MLA_W), lambda i: (0, i, 0)), row(MLA_W)],
        out_shape=[jax.ShapeDtypeStruct((H_B, m, MLA_W), BF16), jax.ShapeDtypeStruct((m, MLA_W), F32)],
        compiler_params=_cparams(("parallel",)),
        name="mla_prep",
    )(cq, ckv, cos2, sin2, q_norm_g.reshape(1, Q_RANK), wuq_ext, kv_norm_g.reshape(1, KV_RANK), wuk_t)


def _softmax_update(s, m_ref, l_ref):
    m_old = m_ref[...]
    m_new = jnp.maximum(m_old, jnp.max(s, axis=-1, keepdims=True))
    alpha = jnp.exp(m_old - m_new)
    p = jnp.exp(s - m_new)
    l_ref[...] = alpha * l_ref[...] + jnp.sum(p, axis=-1, keepdims=True)
    m_ref[...] = m_new
    return alpha, p


def _mla_out(o_lat, wuv_ref, gb, o_ref, rows_per_head):
    for h in range(H_B):
        o_h = _dot(o_lat[h * rows_per_head:(h + 1) * rows_per_head].astype(BF16), wuv_ref[h])
        hs = slice(h * DV_B, (h + 1) * DV_B)
        o_ref[:, hs] = (o_h * _silu(gb[:, hs])).astype(o_ref.dtype)


def _mla_flash_kernel(q_ref, k_ref, gb_ref, wuv_ref, o_ref, m_ref, l_ref, acc_ref):
    i = pl.program_id(1)
    j = pl.program_id(2)
    tq = q_ref.shape[1]
    tk = k_ref.shape[0]

    @pl.when(j == 0)
    def _():
        m_ref[...] = jnp.full_like(m_ref, NEG_INF)
        l_ref[...] = jnp.zeros_like(l_ref)
        acc_ref[...] = jnp.zeros_like(acc_ref)

    @pl.when(j <= i)
    def _():
        q = q_ref[...].reshape(H_B * tq, MLA_W)
        k = k_ref[...].astype(BF16)
        s = _dot_nt(q, k)
        q_pos = i * tq + lax.broadcasted_iota(jnp.int32, (H_B, tq, tk), 1).reshape(H_B * tq, tk)
        k_pos = j * tk + lax.broadcasted_iota(jnp.int32, (H_B * tq, tk), 1)
        s = jnp.where(k_pos <= q_pos, s, NEG_INF)
        alpha, p = _softmax_update(s, m_ref, l_ref)
        acc_ref[...] = alpha * acc_ref[...] + _dot(p.astype(BF16), k[:, 0:KV_RANK])

    @pl.when(j == i)
    def _():
        _mla_out(acc_ref[...] / l_ref[...], wuv_ref, gb_ref[...], o_ref, tq)


def _mla_flash(q, rows, gb, wuv):
    b, t, _ = rows.shape
    tq = tk = ATTN_TILE
    nq = t // tq
    return pl.pallas_call(
        _mla_flash_kernel,
        grid=(b, nq, t // tk),
        in_specs=[pl.BlockSpec((H_B, tq, MLA_W), lambda bi, i, j: (0, bi * nq + i, 0)),
                  pl.BlockSpec((None, tk, MLA_W), lambda bi, i, j: (bi, jnp.minimum(j, i), 0)),
                  pl.BlockSpec((None, tq, D_B), lambda bi, i, j: (bi, i, 0)),
                  _full(wuv.shape)],
        out_specs=pl.BlockSpec((None, tq, D_B), lambda bi, i, j: (bi, i, 0)),
        out_shape=jax.ShapeDtypeStruct((b, t, D_B), BF16),
        scratch_shapes=[pltpu.VMEM((H_B * tq, 1), F32), pltpu.VMEM((H_B * tq, 1), F32),
                        pltpu.VMEM((H_B * tq, KV_RANK), F32)],
        compiler_params=_cparams(("parallel", "parallel", "arbitrary")),
        name="mla_flash",
    )(q, rows, gb, wuv)


def _page_specs(block_tail, layer, pages_per_seq):
    n = len(block_tail)

    def spec(jj):
        return pl.BlockSpec((None, None) + block_tail,
                            lambda b, s, pt: (layer, pt[b * pages_per_seq + s * DECODE_PAGES_PER_STEP + jj])
                            + (0,) * n)
    return [spec(jj) for jj in range(DECODE_PAGES_PER_STEP)]


def _mla_decode_kernel(pt_ref, q_ref, row_ref, gb_ref, wuv_ref, *rest):
    pages = rest[:DECODE_PAGES_PER_STEP]
    o_ref, m_ref, l_ref, acc_ref = rest[DECODE_PAGES_PER_STEP:]
    step = pl.program_id(1)

    @pl.when(step == 0)
    def _():
        m_ref[...] = jnp.full_like(m_ref, NEG_INF)
        l_ref[...] = jnp.zeros_like(l_ref)
        acc_ref[...] = jnp.zeros_like(acc_ref)

    q = q_ref[...]
    ks = [pg[...].astype(BF16) for pg in pages]
    s = jnp.concatenate([_dot_nt(q, kp) for kp in ks], axis=1)
    alpha, p = _softmax_update(s, m_ref, l_ref)
    pv = _dot(p[:, 0:PAGE_SIZE].astype(BF16), ks[0][:, 0:KV_RANK])
    for jj in range(1, DECODE_PAGES_PER_STEP):
        pv = pv + _dot(p[:, jj * PAGE_SIZE:(jj + 1) * PAGE_SIZE].astype(BF16), ks[jj][:, 0:KV_RANK])
    acc_ref[...] = alpha * acc_ref[...] + pv

    @pl.when(step == pl.num_programs(1) - 1)
    def _():
        row = row_ref[...]
        s_new = jnp.sum(q.astype(F32) * row, axis=-1, keepdims=True)
        alpha2, p_new = _softmax_update(s_new, m_ref, l_ref)
        acc = alpha2 * acc_ref[...] + p_new * row[:, 0:KV_RANK]
        _mla_out(acc / l_ref[...], wuv_ref, gb_ref[...], o_ref, 1)


def _mla_decode(page_table_flat, q, rows_new, gb, wuv, cache, layer, pages_per_seq):
    b = q.shape[0]
    steps = pages_per_seq // DECODE_PAGES_PER_STEP
    per_b = lambda shape: pl.BlockSpec((None,) + shape, lambda bi, s, pt: (bi,) + (0,) * len(shape))
    grid_spec = pltpu.PrefetchScalarGridSpec(
        num_scalar_prefetch=1,
        grid=(b, steps),
        in_specs=[per_b((H_B, MLA_W)), per_b((1, MLA_W)), per_b((1, D_B)),
                  pl.BlockSpec(wuv.shape, lambda bi, s, pt: (0, 0, 0))]
                 + _page_specs((PAGE_SIZE, MLA_W), layer, pages_per_seq),
        out_specs=per_b((1, D_B)),
        scratch_shapes=[pltpu.VMEM((H_B, 1), F32), pltpu.VMEM((H_B, 1), F32), pltpu.VMEM((H_B, KV_RANK), F32)],
    )
    return pl.pallas_call(
        _mla_decode_kernel,
        grid_spec=grid_spec,
        out_shape=jax.ShapeDtypeStruct((b, 1, D_B), BF16),
        compiler_params=_cparams(("parallel", "arbitrary")),
        name="mla_decode",
    )(page_table_flat, q, rows_new, gb, wuv, *([cache] * DECODE_PAGES_PER_STEP))


def _bias_kernel(rb_ref, dist_ref, o_ref):
    dist = dist_ref[...]
    n = jnp.maximum(dist, 0)
    max_exact = NUM_BUCKETS // 2
    n_safe = jnp.maximum(n, max_exact).astype(F32)
    large = max_exact + (jnp.log(n_safe / max_exact) / math.log(MAX_DISTANCE / max_exact)
                         * (NUM_BUCKETS - max_exact)).astype(jnp.int32)
    large = jnp.minimum(large, NUM_BUCKETS - 1)
    bucket = jnp.where(n < max_exact, n, large)
    for h in range(H_C):
        bias = jnp.zeros(dist.shape, F32)
        for kb in range(NUM_BUCKETS):
            bias = jnp.where(bucket == kb, rb_ref[kb * H_C + h], bias)
        o_ref[h] = jnp.where(dist >= 0, bias, NEG_INF)


def _bias_tiles(rel_bias, dist):
    g, r, c = dist.shape
    grid_spec = pltpu.PrefetchScalarGridSpec(
        num_scalar_prefetch=1,
        grid=(g,),
        in_specs=[pl.BlockSpec((None, r, c), lambda i, rb: (i, 0, 0))],
        out_specs=pl.BlockSpec((None, H_C, r, c), lambda i, rb: (i, 0, 0, 0)),
    )
    return pl.pallas_call(
        _bias_kernel,
        grid_spec=grid_spec,
        out_shape=jax.ShapeDtypeStruct((g, H_C, r, c), F32),
        compiler_params=_cparams(("arbitrary",)),
        name="rel_bias_tiles",
    )(rel_bias.reshape(-1), dist)


def _diff_lambda(lam_ref, lam_init):
    lam = lam_ref[...]
    e1 = jnp.exp(jnp.sum(lam[0:1] * lam[1:2], axis=-1, keepdims=True))
    e2 = jnp.exp(jnp.sum(lam[2:3] * lam[3:4], axis=-1, keepdims=True))
    return e1 - e2 + lam_init


def _diff_queries(qc):
    lane = lax.broadcasted_iota(jnp.int32, qc.shape, 1)
    qs = qc * DIFF_SCALE
    groups = []
    for h in range(H_C):
        for c in range(2):
            lo = h * DV_C + c * DC
            groups.append(jnp.where((lane >= lo) & (lane < lo + DC), qs, 0.0))
    return jnp.concatenate(groups, axis=0)


def _diff_out(acc, l, lam, lam_init, sg, gc, o_ref, rows):
    for h in range(H_C):
        hs = slice(h * DV_C, (h + 1) * DV_C)
        r1 = slice((2 * h) * rows, (2 * h + 1) * rows)
        r2 = slice((2 * h + 1) * rows, (2 * h + 2) * rows)
        o = acc[r1, hs] / l[r1] - lam * (acc[r2, hs] / l[r2])
        o = _rms(o, sg, SUBLN_EPS) * (1.0 - lam_init)
        o_ref[:, hs] = (o * _silu(gc[:, hs])).astype(o_ref.dtype)


def _diff_flash_kernel(q_ref, k_ref, v_ref, gc_ref, bias_ref, lam_ref, sg_ref, o_ref,
                       qbd_ref, m_ref, l_ref, acc_ref, *, lam_init):
    i = pl.program_id(1)
    j = pl.program_id(2)
    tq = q_ref.shape[0]
    tk = k_ref.shape[0]

    @pl.when(j == 0)
    def _():
        qbd_ref[...] = _diff_queries(q_ref[...]).astype(BF16)
        m_ref[...] = jnp.full_like(m_ref, NEG_INF)
        l_ref[...] = jnp.zeros_like(l_ref)
        acc_ref[...] = jnp.zeros_like(acc_ref)

    @pl.when(j <= i)
    def _():
        k = k_ref[...].astype(BF16)
        s = _dot_nt(qbd_ref[...], k)
        bias = bias_ref[jnp.minimum(i - j, 2)]
        s = (s.reshape(H_C, 2, tq, tk) + bias[:, None]).reshape(2 * H_C * tq, tk)
        alpha, p = _softmax_update(s, m_ref, l_ref)
        acc_ref[...] = alpha * acc_ref[...] + _dot(p.astype(BF16), v_ref[...].astype(BF16))

    @pl.when(j == i)
    def _():
        _diff_out(acc_ref[...], l_ref[...], _diff_lambda(lam_ref, lam_init), lam_init, sg_ref[...],
                  gc_ref[...], o_ref, tq)


def _diff_flash(qc, kc, vc, gc, bias_tiles, lam_vecs, subln_g, lam_init):
    b, t, _ = qc.shape
    tq = tk = ATTN_TILE
    rows = 2 * H_C * tq
    qspec = pl.BlockSpec((None, tq, D_C), lambda bi, i, j: (bi, i, 0))
    kspec = pl.BlockSpec((None, tk, D_C), lambda bi, i, j: (bi, jnp.minimum(j, i), 0))
    return pl.pallas_call(
        functools.partial(_diff_flash_kernel, lam_init=lam_init),
        grid=(b, t // tq, t // tk),
        in_specs=[qspec, kspec, kspec, qspec, _full(bias_tiles.shape), _full(lam_vecs.shape), _full((1, DV_C))],
        out_specs=qspec,
        out_shape=jax.ShapeDtypeStruct((b, t, D_C), BF16),
        scratch_shapes=[pltpu.VMEM((rows, D_C), BF16), pltpu.VMEM((rows, 1), F32), pltpu.VMEM((rows, 1), F32),
                        pltpu.VMEM((rows, D_C), F32)],
        compiler_params=_cparams(("parallel", "parallel", "arbitrary")),
        name="diff_flash",
    )(qc, kc, vc, gc, bias_tiles, lam_vecs, subln_g.reshape(1, DV_C))


def _diff_decode_kernel(pt_ref, q_ref, kn_ref, vn_ref, gc_ref, bias_ref, lam_ref, sg_ref, *rest, lam_init):
    n = DECODE_PAGES_PER_STEP
    kpages, vpages = rest[:n], rest[n:2 * n]
    o_ref, m_ref, l_ref, acc_ref = rest[2 * n:]
    step = pl.program_id(1)
    last = pl.num_programs(1) - 1

    @pl.when(step == 0)
    def _():
        m_ref[...] = jnp.full_like(m_ref, NEG_INF)
        l_ref[...] = jnp.zeros_like(l_ref)
        acc_ref[...] = jnp.zeros_like(acc_ref)

    qbd = _diff_queries(q_ref[...])
    qb = qbd.astype(BF16)
    s = jnp.concatenate([_dot_nt(qb, kp[...].astype(BF16)) for kp in kpages], axis=1)
    s = s + bias_ref[jnp.where(step == last, 1, 0)]
    alpha, p = _softmax_update(s, m_ref, l_ref)
    pv = _dot(p[:, 0:PAGE_SIZE].astype(BF16), vpages[0][...].astype(BF16))
    for jj in range(1, n):
        pv = pv + _dot(p[:, jj * PAGE_SIZE:(jj + 1) * PAGE_SIZE].astype(BF16), vpages[jj][...].astype(BF16))
    acc_ref[...] = alpha * acc_ref[...] + pv

    @pl.when(step == last)
    def _():
        s_new = jnp.sum(qbd * kn_ref[...], axis=-1, keepdims=True) + bias_ref[2][:, 0:1]
        alpha2, p_new = _softmax_update(s_new, m_ref, l_ref)
        acc = alpha2 * acc_ref[...] + p_new * vn_ref[...]
        _diff_out(acc, l_ref[...], _diff_lambda(lam_ref, lam_init), lam_init, sg_ref[...], gc_ref[...], o_ref, 1)


def _diff_decode(page_table_flat, qc, kc, vc, gc, bias_rows, lam_vecs, subln_g, cache_k, cache_v, layer,
                 pages_per_seq, lam_init):
    b = qc.shape[0]
    steps = pages_per_seq // DECODE_PAGES_PER_STEP
    per_b = pl.BlockSpec((None, 1, D_C), lambda bi, s, pt: (bi, 0, 0))
    const = lambda shape: pl.BlockSpec(shape, lambda bi, s, pt: (0,) * len(shape))
    grid_spec = pltpu.PrefetchScalarGridSpec(
        num_scalar_prefetch=1,
        grid=(b, steps),
        in_specs=[per_b, per_b, per_b, per_b, const(bias_rows.shape), const(lam_vecs.shape), const((1, DV_C))]
                 + _page_specs((PAGE_SIZE, D_C), layer, pages_per_seq)
                 + _page_specs((PAGE_SIZE, D_C), layer, pages_per_seq),
        out_specs=per_b,
        scratch_shapes=[pltpu.VMEM((2 * H_C, 1), F32), pltpu.VMEM((2 * H_C, 1), F32),
                        pltpu.VMEM((2 * H_C, D_C), F32)],
    )
    return pl.pallas_call(
        functools.partial(_diff_decode_kernel, lam_init=lam_init),
        grid_spec=grid_spec,
        out_shape=jax.ShapeDtypeStruct((b, 1, D_C), BF16),
        compiler_params=_cparams(("parallel", "arbitrary")),
        name="diff_decode",
    )(page_table_flat, qc, kc, vc, gc, bias_rows, lam_vecs, subln_g.reshape(1, DV_C),
      *([cache_k] * DECODE_PAGES_PER_STEP), *([cache_v] * DECODE_PAGES_PER_STEP))


def _permute_w_in(w):
    o_ckv = A_IN + D_A + Q_RANK
    o_kr = o_ckv + KV_RANK
    o_gb = o_kr + ROPE_B
    half = ROPE_B // 2
    pad = jnp.zeros((w.shape[0], 2 * LANES - KV_RANK - 2 * ROPE_B), w.dtype)
    out = jnp.concatenate([w[:, :o_gb], w[:, o_kr + half:o_gb], w[:, o_kr:o_kr + half], pad, w[:, o_gb:]], axis=1)
    assert out.shape[1] == IN_COLS_PERM
    return out.astype(BF16)


def _extend_w_uq(w):
    w = w.reshape(Q_RANK, H_B, NOPE_B + ROPE_B)
    half = ROPE_B // 2
    rope = w[:, :, NOPE_B:]
    swapped = jnp.concatenate([rope[:, :, half:], rope[:, :, :half]], axis=-1)
    return jnp.concatenate([w, swapped], axis=-1).reshape(Q_RANK, H_B * LANES).astype(BF16)


def _rope_tables(pos):
    inv = ROPE_THETA ** (-jnp.arange(0, ROPE_B, 2, dtype=F32) / ROPE_B)
    ang = pos.astype(F32)[:, None] * inv[None, :]
    cos, sin = jnp.cos(ang), jnp.sin(ang)
    return jnp.concatenate([cos, cos], axis=-1), jnp.concatenate([-sin, sin], axis=-1)


def _layer_weights(l, W):
    row = lambda a: a.reshape(1, -1)
    rwkv = (row(W["mu_shift"][l]), row(W["rw_w0"][l]), W["rw_w2"][l], row(W["rw_a0"][l]), W["rw_a2"][l],
            row(W["rw_k_k"][l]), row(W["rw_k_a"][l]), row(W["rw_r_k"][l]), row(W["rw_gn_g"][l]),
            row(W["rw_gn_b"][l]))
    return dict(
        w_in=_permute_w_in(W["w_in"][l]),
        rwkv=rwkv,
        wuq=_extend_w_uq(W["mla_w_uq"][l]),
        wuk_t=jnp.transpose(W["mla_w_uk"][l], (1, 2, 0)).astype(BF16),
        wuv=jnp.transpose(W["mla_w_uv"][l], (1, 0, 2)).astype(BF16),
        lam_vecs=jnp.stack([W["diff_lam_q1"][l], W["diff_lam_k1"][l], W["diff_lam_q2"][l], W["diff_lam_k2"][l]]),
        wo=W["w_out"][l].astype(BF16),
        wple=W["w_ple"][l].astype(BF16),
        wg=W["w_ple_gate"][l].astype(BF16),
    )


def _run_prompt(x, p, W, LW, depth):
    b, t, d = x.shape
    m = b * t
    tm = 256
    cos2, sin2 = _rope_tables(jnp.arange(t, dtype=jnp.int32))
    tile = jnp.arange(ATTN_TILE, dtype=jnp.int32)
    dist = (jnp.arange(3, dtype=jnp.int32)[:, None, None] * ATTN_TILE + tile[None, :, None] - tile[None, None, :])
    bias_tiles = _bias_tiles(W["rel_bias"], dist)
    uprev0 = jnp.zeros((b, 1, A_IN), F32)
    s0 = jnp.zeros((b, H_A, HEAD_A, HEAD_A), F32)
    h = x.reshape(m, d)
    mla_rows, k_rows, v_rows, wkv_out, shift_out = [], [], [], [], []
    for l in range(depth):
        lw = LW[l]
        u, ga, cq, ckv, gb, qc, kc, vc, gc, xn_last = _inproj(
            h, W["norm_g"][l], lw["w_in"], ALL_SEGS, normalize=True, rows_per_seq=t, tm=tm)
        y_a, s_new = _rwkv_chunked(u.reshape(b, t, A_IN), uprev0, ga.reshape(b, t, D_A), s0, lw["rwkv"])
        q, rows = _mla_prep(cq, ckv, cos2, sin2, W["mla_q_norm_g"][l], lw["wuq"], W["mla_kv_norm_g"][l],
                            lw["wuk_t"], tm=tm, pos_tiles=t // tm)
        y_b = _mla_flash(q, rows.reshape(b, t, MLA_W), gb.reshape(b, t, D_B), lw["wuv"])
        lam_init = 0.8 - 0.6 * math.exp(-0.3 * l)
        y_c = _diff_flash(qc.reshape(b, t, D_C), kc.reshape(b, t, D_C), vc.reshape(b, t, D_C),
                          gc.reshape(b, t, D_C), bias_tiles, lw["lam_vecs"], W["diff_subln_g"][l], lam_init)
        h = _outproj(h, y_a.reshape(m, D_A), y_b.reshape(m, D_B), y_c.reshape(m, D_C), p[l].reshape(m, PLE_DIM),
                     lw["wo"], lw["wple"], lw["wg"], W["final_norm_g"], final=(l == depth - 1), tm=tm)
        mla_rows.append(rows.reshape(b, t, MLA_W))
        k_rows.append(kc.reshape(b, t, H_C, 2 * DC))
        v_rows.append(vc.reshape(b, t, H_C, DV_C))
        wkv_out.append(s_new)
        shift_out.append(xn_last.reshape(b, d))
    return (h.reshape(b, t, d), jnp.stack(mla_rows), jnp.stack(k_rows), jnp.stack(v_rows), jnp.stack(wkv_out),
            jnp.stack(shift_out))


def _run_sample(x, p, state_shift, state_wkv, cache_mla, cache_k, cache_v, page_table, W, LW, depth):
    b, t, d = x.shape
    assert t == 1
    pages_per_seq = page_table.shape[1]
    past_len = pages_per_seq * PAGE_SIZE
    tm = b
    cos2, sin2 = _rope_tables(jnp.full((b,), past_len, dtype=jnp.int32))
    pt_flat = page_table.reshape(-1).astype(jnp.int32)
    step_keys = DECODE_PAGES_PER_STEP * PAGE_SIZE
    key_in_step = jnp.arange(step_keys, dtype=jnp.int32)
    dist = jnp.stack([past_len - key_in_step,
                      past_len - (past_len - step_keys + key_in_step),
                      jnp.zeros((step_keys,), jnp.int32)])
    dist = jnp.broadcast_to(dist[:, None, :], (3, 2, step_keys))
    bias = _bias_tiles(W["rel_bias"], dist)
    bias_rows = bias.reshape(3, 2 * H_C, step_keys)
    cache_k2 = cache_k.reshape(cache_k.shape[:3] + (D_C,))
    cache_v2 = cache_v.reshape(cache_v.shape[:3] + (D_C,))
    h = x.reshape(b, d)
    mla_rows, k_rows, v_rows, wkv_out, shift_out = [], [], [], [], []
    for l in range(depth):
        lw = LW[l]
        u, ga, cq, ckv, gb, qc, kc, vc, gc, xn = _inproj(
            h, W["norm_g"][l], lw["w_in"], ALL_SEGS, normalize=True, rows_per_seq=1, tm=tm)
        (uprev,) = _inproj(state_shift[l], W["norm_g"][l], lw["w_in"][:, :A_IN], (SEG_U,), normalize=False,
                           rows_per_seq=1, tm=tm)
        y_a, s_new = _rwkv_step(u, uprev, ga, state_wkv[l], lw["rwkv"])
        q, rows = _mla_prep(cq, ckv, cos2, sin2, W["mla_q_norm_g"][l], lw["wuq"], W["mla_kv_norm_g"][l],
                            lw["wuk_t"], tm=tm, pos_tiles=1)
        y_b = _mla_decode(pt_flat, jnp.transpose(q, (1, 0, 2)), rows.reshape(b, 1, MLA_W), gb.reshape(b, 1, D_B),
                          lw["wuv"], cache_mla, l, pages_per_seq)
        lam_init = 0.8 - 0.6 * math.exp(-0.3 * l)
        y_c = _diff_decode(pt_flat, qc.reshape(b, 1, D_C), kc.reshape(b, 1, D_C), vc.reshape(b, 1, D_C),
                           gc.reshape(b, 1, D_C), bias_rows, lw["lam_vecs"], W["diff_subln_g"][l], cache_k2,
                           cache_v2, l, pages_per_seq, lam_init)
        h = _outproj(h, y_a, y_b.reshape(b, D_B), y_c.reshape(b, D_C), p[l].reshape(b, PLE_DIM),
                     lw["wo"], lw["wple"], lw["wg"], W["final_norm_g"], final=(l == depth - 1), tm=tm)
        mla_rows.append(rows.reshape(b, 1, MLA_W))
        k_rows.append(kc.reshape(b, 1, H_C, 2 * DC))
        v_rows.append(vc.reshape(b, 1, H_C, DV_C))
        wkv_out.append(s_new)
        shift_out.append(xn)
    return (h.reshape(b, 1, d), jnp.stack(mla_rows), jnp.stack(k_rows), jnp.stack(v_rows), jnp.stack(wkv_out),
            jnp.stack(shift_out))


def kernel(x_prompt, x_sample, cache_mla, cache_diff_k, cache_diff_v, state_wkv, state_shift, page_table,
           p_prompt, p_sample, norm_g, w_in, mu_shift, rw_w0, rw_w2, rw_a0, rw_a2, rw_k_k, rw_k_a, rw_r_k,
           rw_gn_g, rw_gn_b, mla_q_norm_g, mla_w_uq, mla_kv_norm_g, mla_w_uk, mla_w_uv, diff_lam_q1,
           diff_lam_k1, diff_lam_q2, diff_lam_k2, diff_subln_g, rel_bias, w_out, w_ple, w_ple_gate,
           final_norm_g):
    W = {"norm_g": norm_g, "w_in": w_in, "mu_shift": mu_shift, "rw_w0": rw_w0, "rw_w2": rw_w2, "rw_a0": rw_a0,
         "rw_a2": rw_a2, "rw_k_k": rw_k_k, "rw_k_a": rw_k_a, "rw_r_k": rw_r_k, "rw_gn_g": rw_gn_g,
         "rw_gn_b": rw_gn_b, "mla_q_norm_g": mla_q_norm_g, "mla_w_uq": mla_w_uq, "mla_kv_norm_g": mla_kv_norm_g,
         "mla_w_uk": mla_w_uk, "mla_w_uv": mla_w_uv, "diff_lam_q1": diff_lam_q1, "diff_lam_k1": diff_lam_k1,
         "diff_lam_q2": diff_lam_q2, "diff_lam_k2": diff_lam_k2, "diff_subln_g": diff_subln_g,
         "rel_bias": rel_bias, "w_out": w_out, "w_ple": w_ple, "w_ple_gate": w_ple_gate,
         "final_norm_g": final_norm_g}
    depth = w_in.shape[0]
    LW = [_layer_weights(l, W) for l in range(depth)]
    y_p, mla_p, dk_p, dv_p, wkv_p, sh_p = _run_prompt(x_prompt, p_prompt, W, LW, depth)
    y_s, mla_s, dk_s, dv_s, wkv_s, sh_s = _run_sample(x_sample, p_sample, state_shift, state_wkv, cache_mla,
                                                      cache_diff_k, cache_diff_v, page_table, W, LW, depth)
    return (y_p, y_s, mla_p, mla_s, dk_p, dk_s, dv_p, dv_s, wkv_p, wkv_s, sh_p, sh_s)
```

```python
import functools
import math

import jax
import jax.numpy as jnp
from jax import lax
from jax.experimental import pallas as pl
from jax.experimental.pallas import tpu as pltpu

F32 = jnp.float32
BF16 = jnp.bfloat16
HIGHEST = lax.Precision.HIGHEST

LANES = 128
SUBLANES = 8
VMEM_LIMIT_BYTES = 56 * 1024 * 1024

D_MODEL = 1024
HEAD_A = 64
D_A = 512
H_A = D_A // HEAD_A
W_LORA = 64
A_LORA = 64
A_IN = 3 * D_A + W_LORA + A_LORA
D_B = 256
DV_B = 64
H_B = D_B // DV_B
NOPE_B = 64
ROPE_B = 32
Q_RANK = 256
KV_RANK = 128
MLA_W = KV_RANK + ROPE_B
ROPE_THETA = 10000.0
D_C = 256
DV_C = 64
H_C = D_C // DV_C
DC = DV_C // 2
NUM_BUCKETS = 32
MAX_DISTANCE = 128
PLE_DIM = 256
PAGE_SIZE = 128
NEG_INF = -1e30
EPS = 1e-6
GN_EPS = 64e-5
SUBLN_EPS = 1e-5
MLA_SCALE = (NOPE_B + ROPE_B) ** -0.5
DIFF_SCALE = DC ** -0.5

SEG_U = (0, A_IN)
SEG_GA = (A_IN, A_IN + D_A)
SEG_CQ = (SEG_GA[1], SEG_GA[1] + Q_RANK)
SEG_CKV = (SEG_CQ[1], SEG_CQ[1] + 2 * LANES)
SEG_GB = (SEG_CKV[1], SEG_CKV[1] + D_B)
SEG_QC = (SEG_GB[1], SEG_GB[1] + D_C)
SEG_KC = (SEG_QC[1], SEG_QC[1] + D_C)
SEG_VC = (SEG_KC[1], SEG_KC[1] + D_C)
SEG_GC = (SEG_VC[1], SEG_VC[1] + D_C)
ALL_SEGS = (SEG_U, SEG_GA, SEG_CQ, SEG_CKV, SEG_GB, SEG_QC, SEG_KC, SEG_VC, SEG_GC)
IN_COLS_PERM = SEG_GC[1]

RWKV_CHUNK = 64
ATTN_TILE = 256
DECODE_PAGES_PER_STEP = 8


def _cparams(semantics):
    return pltpu.CompilerParams(dimension_semantics=semantics, vmem_limit_bytes=VMEM_LIMIT_BYTES)


def _full(shape):
    n = len(shape)
    return pl.BlockSpec(shape, lambda *_: (0,) * n)


def _sigmoid(x):
    return 1.0 / (1.0 + jnp.exp(-x))


def _silu(x):
    return x * _sigmoid(x)


def _rms(x, g, eps):
    return x * lax.rsqrt(jnp.mean(x * x, axis=-1, keepdims=True) + eps) * g


def _dot(a, b, **kw):
    return jnp.dot(a, b, preferred_element_type=F32, **kw)


def _dot_nt(a, b, **kw):
    return lax.dot_general(a, b, (((1,), (1,)), ((), ())), preferred_element_type=F32, **kw)


def _dot_tn(a, b, **kw):
    return lax.dot_general(a, b, (((0,), (0,)), ((), ())), preferred_element_type=F32, **kw)


def _mm(a, b):
    return _dot(a.astype(BF16), b.astype(BF16))


def _mm_nt(a, b):
    return _dot_nt(a.astype(BF16), b.astype(BF16))


def _mm_tn(a, b):
    return _dot_tn(a.astype(BF16), b.astype(BF16))


def _inproj_kernel(h_ref, g_ref, w_ref, *out_refs, normalize, segs, emit_xn):
    x = h_ref[...]
    xn = _rms(x, g_ref[...], EPS) if normalize else x
    xb = xn.astype(BF16)
    for o_ref, (a, b) in zip(out_refs, segs):
        o_ref[...] = _dot(xb, w_ref[:, a:b])
    if emit_xn == "last_row":
        rows = x.shape[0]
        out_refs[len(segs)][...] = xn[rows - 1:rows, :]
    elif emit_xn == "all":
        out_refs[len(segs)][...] = xn


def _inproj(h2d, norm_g, w_bf16, segs, *, normalize, rows_per_seq, tm):
    m, d = h2d.shape
    assert m % tm == 0
    out_shapes = [jax.ShapeDtypeStruct((m, b - a), F32) for a, b in segs]
    out_specs = [pl.BlockSpec((tm, b - a), lambda i: (i, 0)) for a, b in segs]
    emit_xn = None
    if normalize:
        if rows_per_seq == 1:
            emit_xn = "all"
            out_shapes.append(jax.ShapeDtypeStruct((m, d), F32))
            out_specs.append(pl.BlockSpec((tm, d), lambda i: (i, 0)))
        else:
            assert rows_per_seq % tm == 0
            tiles_per_seq = rows_per_seq // tm
            emit_xn = "last_row"
            out_shapes.append(jax.ShapeDtypeStruct((m // rows_per_seq, 1, d), F32))
            out_specs.append(pl.BlockSpec((None, 1, d), lambda i: (i // tiles_per_seq, 0, 0)))
    kern = functools.partial(_inproj_kernel, normalize=normalize, segs=segs, emit_xn=emit_xn)
    return pl.pallas_call(
        kern,
        grid=(m // tm,),
        in_specs=[pl.BlockSpec((tm, d), lambda i: (i, 0)), _full((1, d)), _full(w_bf16.shape)],
        out_specs=out_specs,
        out_shape=out_shapes,
        compiler_params=_cparams(("arbitrary",)),
        name="inproj",
    )(h2d, norm_g.reshape(1, d), w_bf16)


def _outproj_kernel(h_ref, ya_ref, yb_ref, yc_ref, p_ref, wo_ref, wple_ref, wg_ref, fng_ref, o_ref, *, final):
    mixed = (_dot(ya_ref[...], wo_ref[0:D_A, :])
             + _dot(yb_ref[...], wo_ref[D_A:D_A + D_B, :])
             + _dot(yc_ref[...], wo_ref[D_A + D_B:, :]))
    h2 = h_ref[...] + mixed
    ple = _dot(p_ref[...].astype(BF16), wple_ref[...])
    gate = _sigmoid(_dot(h2.astype(BF16), wg_ref[...]))
    h3 = h2 + ple * gate
    o_ref[...] = _rms(h3, fng_ref[...], EPS) if final else h3


def _outproj(h2d, ya, yb, yc, p2d, wo, wple, wg, final_g, *, final, tm):
    m, d = h2d.shape
    row = lambda w: pl.BlockSpec((tm, w), lambda i: (i, 0))
    return pl.pallas_call(
        functools.partial(_outproj_kernel, final=final),
        grid=(m // tm,),
        in_specs=[row(d), row(D_A), row(D_B), row(D_C), row(PLE_DIM),
                  _full(wo.shape), _full(wple.shape), _full(wg.shape), _full((1, d))],
        out_specs=row(d),
        out_shape=jax.ShapeDtypeStruct((m, d), F32),
        compiler_params=_cparams(("arbitrary",)),
        name="outproj",
    )(h2d, ya, yb, yc, p2d, wo, wple, wg, final_g.reshape(1, d))


def _rwkv_prep(um, w0, w2, a0, a2, k_k, k_a):
    r = um[:, 0:D_A]
    k = um[:, D_A:2 * D_A]
    v = um[:, 2 * D_A:3 * D_A]
    w_lo = um[:, 3 * D_A:3 * D_A + W_LORA]
    a_lo = um[:, 3 * D_A + W_LORA:A_IN]
    wl = w0 + _dot(jnp.tanh(w_lo), w2, precision=HIGHEST)
    neg = -wl
    softplus = jnp.maximum(neg, 0.0) + jnp.log(1.0 + jnp.exp(-jnp.abs(neg)))
    w = -softplus - 0.5
    log_decay = -jnp.exp(w)
    a = _sigmoid(a0 + _dot(a_lo, a2, precision=HIGHEST))
    kk = k * k_k
    k = k * (1.0 + (a - 1.0) * k_a)
    return r, k, v, kk, a, log_decay


def _normalize_kk(kk_h):
    norm = jnp.sqrt(jnp.sum(kk_h * kk_h, axis=-1, keepdims=True))
    return kk_h / jnp.maximum(norm, 1e-12)


def _rwkv_head_out(y, r_h, k_h, v_h, rk_h, gng_h, gnb_h, gate_h):
    mu = jnp.mean(y, axis=-1, keepdims=True)
    var = jnp.mean(jnp.square(y - mu), axis=-1, keepdims=True)
    yn = (y - mu) * lax.rsqrt(var + GN_EPS) * gng_h + gnb_h
    bonus = jnp.sum(r_h * k_h * rk_h, axis=-1, keepdims=True) * v_h
    return (yn + bonus) * _silu(gate_h)


def _rwkv_chunk_kernel(u_ref, uprev0_ref, ga_ref, s0_ref, mu_ref, w0_ref, w2_ref, a0_ref, a2_ref,
                       kk_ref, ka_ref, rk_ref, gng_ref, gnb_ref, y_ref, s_ref, prev_ref):
    c = pl.program_id(1)
    C = u_ref.shape[0]

    @pl.when(c == 0)
    def _():
        prev_ref[...] = uprev0_ref[...]
        s_ref[...] = s0_ref[...]

    u = u_ref[...]
    row = lax.broadcasted_iota(jnp.int32, (C, 1), 0)
    u_prev = jnp.where(row == 0, prev_ref[...], pltpu.roll(u, 1, axis=0))
    prev_ref[...] = u[C - 1:C, :]
    um = u + mu_ref[...] * (u_prev - u)
    r, k, v, kk, a, log_decay = _rwkv_prep(um, w0_ref[...], w2_ref[...], a0_ref[...], a2_ref[...],
                                           kk_ref[...], ka_ref[...])

    ti = lax.broadcasted_iota(jnp.int32, (C, C), 0)
    tj = lax.broadcasted_iota(jnp.int32, (C, C), 1)
    incl = tj <= ti
    strict = tj < ti
    cs = _dot(incl.astype(F32), log_decay, precision=HIGHEST)
    cs_end = cs[C - 1:C, :]
    e_incl = jnp.exp(cs)
    e_excl = jnp.exp(cs - log_decay)
    e_inv = jnp.exp(-cs)
    e_rem = jnp.exp(cs_end - cs)
    p_end = jnp.exp(cs_end)
    eye = (ti == tj).astype(F32)
    ga = ga_ref[...]

    heads = range(H_A)
    hsl = [slice(h * HEAD_A, (h + 1) * HEAD_A) for h in heads]
    kk_n = [_normalize_kk(kk[:, hs]) for hs in hsl]
    b_f = [kk_n[h] * a[:, hsl[h]] for h in heads]
    a_t = [(-kk_n[h] * e_excl[:, hsl[h]]).astype(BF16) for h in heads]
    r_tf = [r[:, hsl[h]] * e_incl[:, hsl[h]] for h in heads]
    r_t = [x.astype(BF16) for x in r_tf]
    b_t = [(b_f[h] * e_inv[:, hsl[h]]).astype(BF16) for h in heads]
    k_t = [(k[:, hsl[h]] * e_inv[:, hsl[h]]).astype(BF16) for h in heads]
    b_end = [(b_f[h] * e_rem[:, hsl[h]]).astype(BF16) for h in heads]
    k_end = [(k[:, hsl[h]] * e_rem[:, hsl[h]]).astype(BF16) for h in heads]
    v_b = [v[:, hsl[h]].astype(BF16) for h in heads]

    l_ab = [jnp.where(strict, _mm_nt(a_t[h], b_t[h]), 0.0) for h in heads]
    l_ak = [jnp.where(strict, _mm_nt(a_t[h], k_t[h]), 0.0).astype(BF16) for h in heads]
    m_rb = [jnp.where(incl, _mm_nt(r_t[h], b_t[h]), 0.0).astype(BF16) for h in heads]
    m_rk = [jnp.where(incl, _mm_nt(r_t[h], k_t[h]), 0.0).astype(BF16) for h in heads]

    inv = [eye + l_ab[h] for h in heads]
    pw = [x.astype(BF16) for x in l_ab]
    for _ in range(int(math.log2(C)) - 1):
        pw = [_mm(pw[h], pw[h]).astype(BF16) for h in heads]
        inv = [inv[h] + _mm(inv[h], pw[h]) for h in heads]
    inv = [x.astype(BF16) for x in inv]

    lv = [_mm(l_ak[h], v_b[h]) for h in heads]
    w_mat = [_mm(inv[h], a_t[h]).astype(BF16) for h in heads]
    u_v = [_mm(inv[h], lv[h]).astype(BF16) for h in heads]
    q_eff = [r_tf[h] + _mm(m_rb[h], w_mat[h]) for h in heads]
    y_v = [_mm(m_rb[h], u_v[h]) + _mm(m_rk[h], v_b[h]) for h in heads]
    kr = lax.broadcasted_iota(jnp.int32, (HEAD_A, HEAD_A), 0)
    kc = lax.broadcasted_iota(jnp.int32, (HEAD_A, HEAD_A), 1)
    trans = [_mm_tn(w_mat[h], b_end[h]) + jnp.where(kr == kc, p_end[:, hsl[h]], 0.0) for h in heads]
    add = [_mm_tn(u_v[h], b_end[h]) + _mm_tn(v_b[h], k_end[h]) for h in heads]

    s_old = [s_ref[h] for h in heads]
    y = [_mm_nt(q_eff[h], s_old[h]) + y_v[h] for h in heads]
    for h in heads:
        s_ref[h] = _mm(s_old[h], trans[h]) + add[h]
    for h in heads:
        hs = hsl[h]
        out = _rwkv_head_out(y[h], r[:, hs], k[:, hs], v[:, hs], rk_ref[:, hs], gng_ref[:, hs], gnb_ref[:, hs],
                             ga[:, hs])
        y_ref[:, hs] = out.astype(y_ref.dtype)


def _rwkv_chunked(u, uprev0, ga, s0, params):
    b, t, _ = u.shape
    c = RWKV_CHUNK
    assert t % c == 0
    tok = lambda w: pl.BlockSpec((None, c, w), lambda i, j: (i, j, 0))
    state = pl.BlockSpec((None, H_A, HEAD_A, HEAD_A), lambda i, j: (i, 0, 0, 0))
    return pl.pallas_call(
        _rwkv_chunk_kernel,
        grid=(b, t // c),
        in_specs=[tok(A_IN), pl.BlockSpec((None, 1, A_IN), lambda i, j: (i, 0, 0)), tok(D_A), state]
                 + [_full(p.shape) for p in params],
        out_specs=[tok(D_A), state],
        out_shape=[jax.ShapeDtypeStruct((b, t, D_A), BF16), jax.ShapeDtypeStruct(s0.shape, F32)],
        scratch_shapes=[pltpu.VMEM((1, A_IN), F32)],
        compiler_params=_cparams(("parallel", "arbitrary")),
        name="rwkv_chunked",
    )(u, uprev0, ga, s0, *params)


def _rwkv_step_kernel(u_ref, uprev_ref, ga_ref, s0_ref, mu_ref, w0_ref, w2_ref, a0_ref, a2_ref,
                      kk_ref, ka_ref, rk_ref, gng_ref, gnb_ref, y_ref, s_ref):
    nb = u_ref.shape[0]
    u = u_ref[...]
    um = u + mu_ref[...] * (uprev_ref[...] - u)
    r, k, v, kk, a, log_decay = _rwkv_prep(um, w0_ref[...], w2_ref[...], a0_ref[...], a2_ref[...],
                                           kk_ref[...], ka_ref[...])
    decay = jnp.exp(log_decay)
    ga = ga_ref[...]
    ri = lax.broadcasted_iota(jnp.int32, (HEAD_A, HEAD_A), 0)
    ci = lax.broadcasted_iota(jnp.int32, (HEAD_A, HEAD_A), 1)
    eye = ri == ci
    for h in range(H_A):
        hs = slice(h * HEAD_A, (h + 1) * HEAD_A)
        kk_h = _normalize_kk(kk[:, hs])
        b_h = kk_h * a[:, hs]
        outs = []
        for i in range(nb):
            rs = slice(i, i + 1)
            s = s0_ref[i, h]
            sa = jnp.sum(s * (-kk_h[rs]), axis=-1, keepdims=True)
            v_col = jnp.sum(jnp.where(eye, v[rs, hs], 0.0), axis=-1, keepdims=True)
            s_new = s * decay[rs, hs] + sa * b_h[rs] + v_col * k[rs, hs]
            s_ref[i, h] = s_new
            y_col = jnp.sum(s_new * r[rs, hs], axis=-1, keepdims=True)
            outs.append(jnp.sum(jnp.where(eye, y_col, 0.0), axis=0, keepdims=True))
        y = jnp.concatenate(outs, axis=0)
        out = _rwkv_head_out(y, r[:, hs], k[:, hs], v[:, hs], rk_ref[:, hs], gng_ref[:, hs],
                             gnb_ref[:, hs], ga[:, hs])
        y_ref[:, hs] = out.astype(y_ref.dtype)


def _rwkv_step(u, uprev, ga, s0, params, *, nb=SUBLANES):
    b = u.shape[0]
    assert b % nb == 0
    row = lambda w: pl.BlockSpec((nb, w), lambda i: (i, 0))
    state = pl.BlockSpec((nb, H_A, HEAD_A, HEAD_A), lambda i: (i, 0, 0, 0))
    return pl.pallas_call(
        _rwkv_step_kernel,
        grid=(b // nb,),
        in_specs=[row(A_IN), row(A_IN), row(D_A), state] + [_full(p.shape) for p in params],
        out_specs=[row(D_A), state],
        out_shape=[jax.ShapeDtypeStruct((b, D_A), BF16), jax.ShapeDtypeStruct(s0.shape, F32)],
        compiler_params=_cparams(("parallel",)),
        name="rwkv_step",
    )(u, uprev, ga, s0, *params)


def _mla_prep_kernel(cq_ref, ckv_ref, cos_ref, sin_ref, qg_ref, wuq_ref, kvg_ref, wukt_ref, q_ref, rows_ref):
    cos2 = cos_ref[...]
    sin2 = sin_ref[...]
    qn = _rms(cq_ref[...], qg_ref[...], EPS).astype(BF16)
    q = _dot(qn, wuq_ref[...])
    for h in range(H_B):
        qh = q[:, h * LANES:(h + 1) * LANES]
        q_lat = _dot(qh[:, :NOPE_B].astype(BF16), wukt_ref[h])
        q_rope = qh[:, NOPE_B:NOPE_B + ROPE_B] * cos2 + qh[:, NOPE_B + ROPE_B:] * sin2
        q_ref[h, :, 0:KV_RANK] = (q_lat * MLA_SCALE).astype(q_ref.dtype)
        q_ref[h, :, KV_RANK:MLA_W] = (q_rope * MLA_SCALE).astype(q_ref.dtype)
    ckv = ckv_ref[...]
    rows_ref[:, 0:KV_RANK] = _rms(ckv[:, 0:KV_RANK], kvg_ref[...], EPS)
    rows_ref[:, KV_RANK:MLA_W] = (ckv[:, KV_RANK:KV_RANK + ROPE_B] * cos2
                                  + ckv[:, KV_RANK + ROPE_B:KV_RANK + 2 * ROPE_B] * sin2)


def _mla_prep(cq, ckv, cos2, sin2, q_norm_g, wuq_ext, kv_norm_g, wuk_t, *, tm, pos_tiles):
    m = cq.shape[0]
    row = lambda w: pl.BlockSpec((tm, w), lambda i: (i, 0))
    pos = pl.BlockSpec((tm, ROPE_B), lambda i: (i % pos_tiles, 0))
    return pl.pallas_call(
        _mla_prep_kernel,
        grid=(m // tm,),
        in_specs=[row(Q_RANK), row(2 * LANES), pos, pos, _full((1, Q_RANK)), _full(wuq_ext.shape),
                  _full((1, KV_RANK)), _full(wuk_t.shape)],
        out_specs=[pl.BlockSpec((H_B, tm, MLA_W), lambda i: (0, i, 0)), row(MLA_W)],
        out_shape=[jax.ShapeDtypeStruct((H_B, m, MLA_W), BF16), jax.ShapeDtypeStruct((m, MLA_W), F32)],
        compiler_params=_cparams(("parallel",)),
        name="mla_prep",
    )(cq, ckv, cos2, sin2, q_norm_g.reshape(1, Q_RANK), wuq_ext, kv_norm_g.reshape(1, KV_RANK), wuk_t)


def _softmax_update(s, m_ref, l_ref):
    m_old = m_ref[...]
    m_new = jnp.maximum(m_old, jnp.max(s, axis=-1, keepdims=True))
    alpha = jnp.exp(m_old - m_new)
    p = jnp.exp(s - m_new)
    l_ref[...] = alpha * l_ref[...] + jnp.sum(p, axis=-1, keepdims=True)
    m_ref[...] = m_new
    return alpha, p


def _mla_out(o_lat, wuv_ref, gb, o_ref, rows_per_head):
    for h in range(H_B):
        o_h = _dot(o_lat[h * rows_per_head:(h + 1) * rows_per_head].astype(BF16), wuv_ref[h])
        hs = slice(h * DV_B, (h + 1) * DV_B)
        o_ref[:, hs] = (o_h * _silu(gb[:, hs])).astype(o_ref.dtype)


def _mla_flash_kernel(q_ref, k_ref, gb_ref, wuv_ref, o_ref, m_ref, l_ref, acc_ref):
    i = pl.program_id(1)
    j = pl.program_id(2)
    tq = q_ref.shape[1]
    tk = k_ref.shape[0]

    @pl.when(j == 0)
    def _():
        m_ref[...] = jnp.full_like(m_ref, NEG_INF)
        l_ref[...] = jnp.zeros_like(l_ref)
        acc_ref[...] = jnp.zeros_like(acc_ref)

    @pl.when(j <= i)
    def _():
        q = q_ref[...].reshape(H_B * tq, MLA_W)
        k = k_ref[...].astype(BF16)
        s = _dot_nt(q, k)
        q_pos = i * tq + lax.broadcasted_iota(jnp.int32, (H_B, tq, tk), 1).reshape(H_B * tq, tk)
        k_pos = j * tk + lax.broadcasted_iota(jnp.int32, (H_B * tq, tk), 1)
        s = jnp.where(k_pos <= q_pos, s, NEG_INF)
        alpha, p = _softmax_update(s, m_ref, l_ref)
        acc_ref[...] = alpha * acc_ref[...] + _dot(p.astype(BF16), k[:, 0:KV_RANK])

    @pl.when(j == i)
    def _():
        _mla_out(acc_ref[...] / l_ref[...], wuv_ref, gb_ref[...], o_ref, tq)


def _mla_flash(q, rows, gb, wuv):
    b, t, _ = rows.shape
    tq = tk = ATTN_TILE
    nq = t // tq
    return pl.pallas_call(
        _mla_flash_kernel,
        grid=(b, nq, t // tk),
        in_specs=[pl.BlockSpec((H_B, tq, MLA_W), lambda bi, i, j: (0, bi * nq + i, 0)),
                  pl.BlockSpec((None, tk, MLA_W), lambda bi, i, j: (bi, jnp.minimum(j, i), 0)),
                  pl.BlockSpec((None, tq, D_B), lambda bi, i, j: (bi, i, 0)),
                  _full(wuv.shape)],
        out_specs=pl.BlockSpec((None, tq, D_B), lambda bi, i, j: (bi, i, 0)),
        out_shape=jax.ShapeDtypeStruct((b, t, D_B), BF16),
        scratch_shapes=[pltpu.VMEM((H_B * tq, 1), F32), pltpu.VMEM((H_B * tq, 1), F32),
                        pltpu.VMEM((H_B * tq, KV_RANK), F32)],
        compiler_params=_cparams(("parallel", "parallel", "arbitrary")),
        name="mla_flash",
    )(q, rows, gb, wuv)


def _page_specs(block_tail, layer, pages_per_seq):
    n = len(block_tail)

    def spec(jj):
        return pl.BlockSpec((None, None) + block_tail,
                            lambda b, s, pt: (layer, pt[b * pages_per_seq + s * DECODE_PAGES_PER_STEP + jj])
                            + (0,) * n)
    return [spec(jj) for jj in range(DECODE_PAGES_PER_STEP)]


def _mla_decode_kernel(pt_ref, q_ref, row_ref, gb_ref, wuv_ref, *rest):
    pages = rest[:DECODE_PAGES_PER_STEP]
    o_ref, m_ref, l_ref, acc_ref = rest[DECODE_PAGES_PER_STEP:]
    step = pl.program_id(1)

    @pl.when(step == 0)
    def _():
        m_ref[...] = jnp.full_like(m_ref, NEG_INF)
        l_ref[...] = jnp.zeros_like(l_ref)
        acc_ref[...] = jnp.zeros_like(acc_ref)

    q = q_ref[...]
    ks = [pg[...].astype(BF16) for pg in pages]
    s = jnp.concatenate([_dot(q, kp) for kp in ks], axis=1)
    alpha, p = _softmax_update(s, m_ref, l_ref)
    pv = _dot_nt(p[:, 0:PAGE_SIZE].astype(BF16), ks[0][0:KV_RANK, :])
    for jj in range(1, DECODE_PAGES_PER_STEP):
        pv = pv + _dot_nt(p[:, jj * PAGE_SIZE:(jj + 1) * PAGE_SIZE].astype(BF16), ks[jj][0:KV_RANK, :])
    acc_ref[...] = alpha * acc_ref[...] + pv

    @pl.when(step == pl.num_programs(1) - 1)
    def _():
        row = row_ref[...]
        s_new = jnp.sum(q.astype(F32) * row, axis=-1, keepdims=True)
        alpha2, p_new = _softmax_update(s_new, m_ref, l_ref)
        acc = alpha2 * acc_ref[...] + p_new * row[:, 0:KV_RANK]
        _mla_out(acc / l_ref[...], wuv_ref, gb_ref[...], o_ref, 1)


def _mla_decode(page_table_flat, q, rows_new, gb, wuv, cache, layer, pages_per_seq):
    b = q.shape[0]
    steps = pages_per_seq // DECODE_PAGES_PER_STEP
    per_b = lambda shape: pl.BlockSpec((None,) + shape, lambda bi, s, pt: (bi,) + (0,) * len(shape))
    grid_spec = pltpu.PrefetchScalarGridSpec(
        num_scalar_prefetch=1,
        grid=(b, steps),
        in_specs=[per_b((H_B, MLA_W)), per_b((1, MLA_W)), per_b((1, D_B)),
                  pl.BlockSpec(wuv.shape, lambda bi, s, pt: (0, 0, 0))]
                 + _page_specs((MLA_W, PAGE_SIZE), layer, pages_per_seq),
        out_specs=per_b((1, D_B)),
        scratch_shapes=[pltpu.VMEM((H_B, 1), F32), pltpu.VMEM((H_B, 1), F32), pltpu.VMEM((H_B, KV_RANK), F32)],
    )
    return pl.pallas_call(
        _mla_decode_kernel,
        grid_spec=grid_spec,
        out_shape=jax.ShapeDtypeStruct((b, 1, D_B), BF16),
        compiler_params=_cparams(("parallel", "arbitrary")),
        name="mla_decode",
    )(page_table_flat, q, rows_new, gb, wuv, *([cache] * DECODE_PAGES_PER_STEP))


def _bias_kernel(rb_ref, dist_ref, o_ref):
    dist = dist_ref[...]
    n = jnp.maximum(dist, 0)
    max_exact = NUM_BUCKETS // 2
    n_safe = jnp.maximum(n, max_exact).astype(F32)
    large = max_exact + (jnp.log(n_safe / max_exact) / math.log(MAX_DISTANCE / max_exact)
                         * (NUM_BUCKETS - max_exact)).astype(jnp.int32)
    large = jnp.minimum(large, NUM_BUCKETS - 1)
    bucket = jnp.where(n < max_exact, n, large)
    for h in range(H_C):
        bias = jnp.zeros(dist.shape, F32)
        for kb in range(NUM_BUCKETS):
            bias = jnp.where(bucket == kb, rb_ref[kb * H_C + h], bias)
        o_ref[h] = jnp.where(dist >= 0, bias, NEG_INF)


def _bias_tiles(rel_bias, dist):
    g, r, c = dist.shape
    grid_spec = pltpu.PrefetchScalarGridSpec(
        num_scalar_prefetch=1,
        grid=(g,),
        in_specs=[pl.BlockSpec((None, r, c), lambda i, rb: (i, 0, 0))],
        out_specs=pl.BlockSpec((None, H_C, r, c), lambda i, rb: (i, 0, 0, 0)),
    )
    return pl.pallas_call(
        _bias_kernel,
        grid_spec=grid_spec,
        out_shape=jax.ShapeDtypeStruct((g, H_C, r, c), F32),
        compiler_params=_cparams(("arbitrary",)),
        name="rel_bias_tiles",
    )(rel_bias.reshape(-1), dist)


def _diff_lambda(lam_ref, lam_init):
    lam = lam_ref[...]
    e1 = jnp.exp(jnp.sum(lam[0:1] * lam[1:2], axis=-1, keepdims=True))
    e2 = jnp.exp(jnp.sum(lam[2:3] * lam[3:4], axis=-1, keepdims=True))
    return e1 - e2 + lam_init


def _diff_queries(qc):
    lane = lax.broadcasted_iota(jnp.int32, qc.shape, 1)
    qs = qc * DIFF_SCALE
    groups = []
    for h in range(H_C):
        for c in range(2):
            lo = h * DV_C + c * DC
            groups.append(jnp.where((lane >= lo) & (lane < lo + DC), qs, 0.0))
    return jnp.concatenate(groups, axis=0)


def _diff_out(acc, l, lam, lam_init, sg, gc, o_ref, rows):
    for h in range(H_C):
        hs = slice(h * DV_C, (h + 1) * DV_C)
        r1 = slice((2 * h) * rows, (2 * h + 1) * rows)
        r2 = slice((2 * h + 1) * rows, (2 * h + 2) * rows)
        o = acc[r1, hs] / l[r1] - lam * (acc[r2, hs] / l[r2])
        o = _rms(o, sg, SUBLN_EPS) * (1.0 - lam_init)
        o_ref[:, hs] = (o * _silu(gc[:, hs])).astype(o_ref.dtype)


def _diff_flash_kernel(q_ref, k_ref, v_ref, gc_ref, bias_ref, lam_ref, sg_ref, o_ref,
                       qbd_ref, m_ref, l_ref, acc_ref, *, lam_init):
    i = pl.program_id(1)
    j = pl.program_id(2)
    tq = q_ref.shape[0]
    tk = k_ref.shape[0]

    @pl.when(j == 0)
    def _():
        qbd_ref[...] = _diff_queries(q_ref[...]).astype(BF16)
        m_ref[...] = jnp.full_like(m_ref, NEG_INF)
        l_ref[...] = jnp.zeros_like(l_ref)
        acc_ref[...] = jnp.zeros_like(acc_ref)

    @pl.when(j <= i)
    def _():
        k = k_ref[...].astype(BF16)
        s = _dot_nt(qbd_ref[...], k)
        bias = bias_ref[jnp.minimum(i - j, 2)]
        s = (s.reshape(H_C, 2, tq, tk) + bias[:, None]).reshape(2 * H_C * tq, tk)
        alpha, p = _softmax_update(s, m_ref, l_ref)
        acc_ref[...] = alpha * acc_ref[...] + _dot(p.astype(BF16), v_ref[...].astype(BF16))

    @pl.when(j == i)
    def _():
        _diff_out(acc_ref[...], l_ref[...], _diff_lambda(lam_ref, lam_init), lam_init, sg_ref[...],
                  gc_ref[...], o_ref, tq)


def _diff_flash(qc, kc, vc, gc, bias_tiles, lam_vecs, subln_g, lam_init):
    b, t, _ = qc.shape
    tq = tk = ATTN_TILE
    rows = 2 * H_C * tq
    qspec = pl.BlockSpec((None, tq, D_C), lambda bi, i, j: (bi, i, 0))
    kspec = pl.BlockSpec((None, tk, D_C), lambda bi, i, j: (bi, jnp.minimum(j, i), 0))
    return pl.pallas_call(
        functools.partial(_diff_flash_kernel, lam_init=lam_init),
        grid=(b, t // tq, t // tk),
        in_specs=[qspec, kspec, kspec, qspec, _full(bias_tiles.shape), _full(lam_vecs.shape), _full((1, DV_C))],
        out_specs=qspec,
        out_shape=jax.ShapeDtypeStruct((b, t, D_C), BF16),
        scratch_shapes=[pltpu.VMEM((rows, D_C), BF16), pltpu.VMEM((rows, 1), F32), pltpu.VMEM((rows, 1), F32),
                        pltpu.VMEM((rows, D_C), F32)],
        compiler_params=_cparams(("parallel", "parallel", "arbitrary")),
        name="diff_flash",
    )(qc, kc, vc, gc, bias_tiles, lam_vecs, subln_g.reshape(1, DV_C))


def _diff_decode_kernel(pt_ref, q_ref, kn_ref, vn_ref, gc_ref, bias_ref, lam_ref, sg_ref, *rest, lam_init):
    n = DECODE_PAGES_PER_STEP
    kpages, vpages = rest[:n], rest[n:2 * n]
    o_ref, m_ref, l_ref, acc_ref = rest[2 * n:]
    step = pl.program_id(1)
    last = pl.num_programs(1) - 1

    @pl.when(step == 0)
    def _():
        m_ref[...] = jnp.full_like(m_ref, NEG_INF)
        l_ref[...] = jnp.zeros_like(l_ref)
        acc_ref[...] = jnp.zeros_like(acc_ref)

    qbd = _diff_queries(q_ref[...])
    qb = qbd.astype(BF16)
    s = jnp.concatenate([_dot(qb, kp[...].astype(BF16)) for kp in kpages], axis=1)
    s = s + bias_ref[jnp.where(step == last, 1, 0)]
    alpha, p = _softmax_update(s, m_ref, l_ref)
    pv = _dot_nt(p[:, 0:PAGE_SIZE].astype(BF16), vpages[0][...].astype(BF16))
    for jj in range(1, n):
        pv = pv + _dot_nt(p[:, jj * PAGE_SIZE:(jj + 1) * PAGE_SIZE].astype(BF16), vpages[jj][...].astype(BF16))
    acc_ref[...] = alpha * acc_ref[...] + pv

    @pl.when(step == last)
    def _():
        s_new = jnp.sum(qbd * kn_ref[...], axis=-1, keepdims=True) + bias_ref[2][:, 0:1]
        alpha2, p_new = _softmax_update(s_new, m_ref, l_ref)
        acc = alpha2 * acc_ref[...] + p_new * vn_ref[...]
        _diff_out(acc, l_ref[...], _diff_lambda(lam_ref, lam_init), lam_init, sg_ref[...], gc_ref[...], o_ref, 1)


def _diff_decode(page_table_flat, qc, kc, vc, gc, bias_rows, lam_vecs, subln_g, cache_k, cache_v, layer,
                 pages_per_seq, lam_init):
    b = qc.shape[0]
    steps = pages_per_seq // DECODE_PAGES_PER_STEP
    per_b = pl.BlockSpec((None, 1, D_C), lambda bi, s, pt: (bi, 0, 0))
    const = lambda shape: pl.BlockSpec(shape, lambda bi, s, pt: (0,) * len(shape))
    grid_spec = pltpu.PrefetchScalarGridSpec(
        num_scalar_prefetch=1,
        grid=(b, steps),
        in_specs=[per_b, per_b, per_b, per_b, const(bias_rows.shape), const(lam_vecs.shape), const((1, DV_C))]
                 + _page_specs((D_C, PAGE_SIZE), layer, pages_per_seq)
                 + _page_specs((D_C, PAGE_SIZE), layer, pages_per_seq),
        out_specs=per_b,
        scratch_shapes=[pltpu.VMEM((2 * H_C, 1), F32), pltpu.VMEM((2 * H_C, 1), F32),
                        pltpu.VMEM((2 * H_C, D_C), F32)],
    )
    return pl.pallas_call(
        functools.partial(_diff_decode_kernel, lam_init=lam_init),
        grid_spec=grid_spec,
        out_shape=jax.ShapeDtypeStruct((b, 1, D_C), BF16),
        compiler_params=_cparams(("parallel", "arbitrary")),
        name="diff_decode",
    )(page_table_flat, qc, kc, vc, gc, bias_rows, lam_vecs, subln_g.reshape(1, DV_C),
      *([cache_k] * DECODE_PAGES_PER_STEP), *([cache_v] * DECODE_PAGES_PER_STEP))


def _permute_w_in(w):
    o_ckv = A_IN + D_A + Q_RANK
    o_kr = o_ckv + KV_RANK
    o_gb = o_kr + ROPE_B
    half = ROPE_B // 2
    pad = jnp.zeros((w.shape[0], 2 * LANES - KV_RANK - 2 * ROPE_B), w.dtype)
    out = jnp.concatenate([w[:, :o_gb], w[:, o_kr + half:o_gb], w[:, o_kr:o_kr + half], pad, w[:, o_gb:]], axis=1)
    assert out.shape[1] == IN_COLS_PERM
    return out.astype(BF16)


def _extend_w_uq(w):
    w = w.reshape(Q_RANK, H_B, NOPE_B + ROPE_B)
    half = ROPE_B // 2
    rope = w[:, :, NOPE_B:]
    swapped = jnp.concatenate([rope[:, :, half:], rope[:, :, :half]], axis=-1)
    return jnp.concatenate([w, swapped], axis=-1).reshape(Q_RANK, H_B * LANES).astype(BF16)


def _rope_tables(pos):
    inv = ROPE_THETA ** (-jnp.arange(0, ROPE_B, 2, dtype=F32) / ROPE_B)
    ang = pos.astype(F32)[:, None] * inv[None, :]
    cos, sin = jnp.cos(ang), jnp.sin(ang)
    return jnp.concatenate([cos, cos], axis=-1), jnp.concatenate([-sin, sin], axis=-1)


def _layer_weights(l, W):
    row = lambda a: a.reshape(1, -1)
    rwkv = (row(W["mu_shift"][l]), row(W["rw_w0"][l]), W["rw_w2"][l], row(W["rw_a0"][l]), W["rw_a2"][l],
            row(W["rw_k_k"][l]), row(W["rw_k_a"][l]), row(W["rw_r_k"][l]), row(W["rw_gn_g"][l]),
            row(W["rw_gn_b"][l]))
    return dict(
        w_in=_permute_w_in(W["w_in"][l]),
        rwkv=rwkv,
        wuq=_extend_w_uq(W["mla_w_uq"][l]),
        wuk_t=jnp.transpose(W["mla_w_uk"][l], (1, 2, 0)).astype(BF16),
        wuv=jnp.transpose(W["mla_w_uv"][l], (1, 0, 2)).astype(BF16),
        lam_vecs=jnp.stack([W["diff_lam_q1"][l], W["diff_lam_k1"][l], W["diff_lam_q2"][l], W["diff_lam_k2"][l]]),
        wo=W["w_out"][l].astype(BF16),
        wple=W["w_ple"][l].astype(BF16),
        wg=W["w_ple_gate"][l].astype(BF16),
    )


def _run_prompt(x, p, W, LW, depth):
    b, t, d = x.shape
    m = b * t
    tm = 256
    cos2, sin2 = _rope_tables(jnp.arange(t, dtype=jnp.int32))
    tile = jnp.arange(ATTN_TILE, dtype=jnp.int32)
    dist = (jnp.arange(3, dtype=jnp.int32)[:, None, None] * ATTN_TILE + tile[None, :, None] - tile[None, None, :])
    bias_tiles = _bias_tiles(W["rel_bias"], dist)
    uprev0 = jnp.zeros((b, 1, A_IN), F32)
    s0 = jnp.zeros((b, H_A, HEAD_A, HEAD_A), F32)
    h = x.reshape(m, d)
    mla_rows, k_rows, v_rows, wkv_out, shift_out = [], [], [], [], []
    for l in range(depth):
        lw = LW[l]
        u, ga, cq, ckv, gb, qc, kc, vc, gc, xn_last = _inproj(
            h, W["norm_g"][l], lw["w_in"], ALL_SEGS, normalize=True, rows_per_seq=t, tm=tm)
        y_a, s_new = _rwkv_chunked(u.reshape(b, t, A_IN), uprev0, ga.reshape(b, t, D_A), s0, lw["rwkv"])
        q, rows = _mla_prep(cq, ckv, cos2, sin2, W["mla_q_norm_g"][l], lw["wuq"], W["mla_kv_norm_g"][l],
                            lw["wuk_t"], tm=tm, pos_tiles=t // tm)
        y_b = _mla_flash(q, rows.reshape(b, t, MLA_W), gb.reshape(b, t, D_B), lw["wuv"])
        lam_init = 0.8 - 0.6 * math.exp(-0.3 * l)
        y_c = _diff_flash(qc.reshape(b, t, D_C), kc.reshape(b, t, D_C), vc.reshape(b, t, D_C),
                          gc.reshape(b, t, D_C), bias_tiles, lw["lam_vecs"], W["diff_subln_g"][l], lam_init)
        h = _outproj(h, y_a.reshape(m, D_A), y_b.reshape(m, D_B), y_c.reshape(m, D_C), p[l].reshape(m, PLE_DIM),
                     lw["wo"], lw["wple"], lw["wg"], W["final_norm_g"], final=(l == depth - 1), tm=tm)
        mla_rows.append(rows.reshape(b, t, MLA_W))
        k_rows.append(kc.reshape(b, t, H_C, 2 * DC))
        v_rows.append(vc.reshape(b, t, H_C, DV_C))
        wkv_out.append(s_new)
        shift_out.append(xn_last.reshape(b, d))
    return (h.reshape(b, t, d), jnp.stack(mla_rows), jnp.stack(k_rows), jnp.stack(v_rows), jnp.stack(wkv_out),
            jnp.stack(shift_out))


def _run_sample(x, p, state_shift, state_wkv, cache_mla, cache_k, cache_v, page_table, W, LW, depth):
    b, t, d = x.shape
    assert t == 1
    pages_per_seq = page_table.shape[1]
    past_len = pages_per_seq * PAGE_SIZE
    tm = b
    cos2, sin2 = _rope_tables(jnp.full((b,), past_len, dtype=jnp.int32))
    pt_flat = page_table.reshape(-1).astype(jnp.int32)
    step_keys = DECODE_PAGES_PER_STEP * PAGE_SIZE
    key_in_step = jnp.arange(step_keys, dtype=jnp.int32)
    dist = jnp.stack([past_len - key_in_step,
                      past_len - (past_len - step_keys + key_in_step),
                      jnp.zeros((step_keys,), jnp.int32)])
    dist = jnp.broadcast_to(dist[:, None, :], (3, 2, step_keys))
    bias = _bias_tiles(W["rel_bias"], dist)
    bias_rows = bias.reshape(3, 2 * H_C, step_keys)
    to_feature_major = lambda c: jnp.transpose(c, (0, 1, 3, 4, 2)).reshape(c.shape[:2] + (D_C, PAGE_SIZE))
    cache_k2 = to_feature_major(cache_k)
    cache_v2 = to_feature_major(cache_v)
    cache_mla_t = jnp.transpose(cache_mla, (0, 1, 3, 2))
    h = x.reshape(b, d)
    mla_rows, k_rows, v_rows, wkv_out, shift_out = [], [], [], [], []
    for l in range(depth):
        lw = LW[l]
        u, ga, cq, ckv, gb, qc, kc, vc, gc, xn = _inproj(
            h, W["norm_g"][l], lw["w_in"], ALL_SEGS, normalize=True, rows_per_seq=1, tm=tm)
        (uprev,) = _inproj(state_shift[l], W["norm_g"][l], lw["w_in"][:, :A_IN], (SEG_U,), normalize=False,
                           rows_per_seq=1, tm=tm)
        y_a, s_new = _rwkv_step(u, uprev, ga, state_wkv[l], lw["rwkv"])
        q, rows = _mla_prep(cq, ckv, cos2, sin2, W["mla_q_norm_g"][l], lw["wuq"], W["mla_kv_norm_g"][l],
                            lw["wuk_t"], tm=tm, pos_tiles=1)
        y_b = _mla_decode(pt_flat, jnp.transpose(q, (1, 0, 2)), rows.reshape(b, 1, MLA_W), gb.reshape(b, 1, D_B),
                          lw["wuv"], cache_mla_t, l, pages_per_seq)
        lam_init = 0.8 - 0.6 * math.exp(-0.3 * l)
        y_c = _diff_decode(pt_flat, qc.reshape(b, 1, D_C), kc.reshape(b, 1, D_C), vc.reshape(b, 1, D_C),
                           gc.reshape(b, 1, D_C), bias_rows, lw["lam_vecs"], W["diff_subln_g"][l], cache_k2,
                           cache_v2, l, pages_per_seq, lam_init)
        h = _outproj(h, y_a, y_b.reshape(b, D_B), y_c.reshape(b, D_C), p[l].reshape(b, PLE_DIM),
                     lw["wo"], lw["wple"], lw["wg"], W["final_norm_g"], final=(l == depth - 1), tm=tm)
        mla_rows.append(rows.reshape(b, 1, MLA_W))
        k_rows.append(kc.reshape(b, 1, H_C, 2 * DC))
        v_rows.append(vc.reshape(b, 1, H_C, DV_C))
        wkv_out.append(s_new)
        shift_out.append(xn)
    return (h.reshape(b, 1, d), jnp.stack(mla_rows), jnp.stack(k_rows), jnp.stack(v_rows), jnp.stack(wkv_out),
            jnp.stack(shift_out))


def kernel(x_prompt, x_sample, cache_mla, cache_diff_k, cache_diff_v, state_wkv, state_shift, page_table,
           p_prompt, p_sample, norm_g, w_in, mu_shift, rw_w0, rw_w2, rw_a0, rw_a2, rw_k_k, rw_k_a, rw_r_k,
           rw_gn_g, rw_gn_b, mla_q_norm_g, mla_w_uq, mla_kv_norm_g, mla_w_uk, mla_w_uv, diff_lam_q1,
           diff_lam_k1, diff_lam_q2, diff_lam_k2, diff_subln_g, rel_bias, w_out, w_ple, w_ple_gate,
           final_norm_g):
    W = {"norm_g": norm_g, "w_in": w_in, "mu_shift": mu_shift, "rw_w0": rw_w0, "rw_w2": rw_w2, "rw_a0": rw_a0,
         "rw_a2": rw_a2, "rw_k_k": rw_k_k, "rw_k_a": rw_k_a, "rw_r_k": rw_r_k, "rw_gn_g": rw_gn_g,
         "rw_gn_b": rw_gn_b, "mla_q_norm_g": mla_q_norm_g, "mla_w_uq": mla_w_uq, "mla_kv_norm_g": mla_kv_norm_g,
         "mla_w_uk": mla_w_uk, "mla_w_uv": mla_w_uv, "diff_lam_q1": diff_lam_q1, "diff_lam_k1": diff_lam_k1,
         "diff_lam_q2": diff_lam_q2, "diff_lam_k2": diff_lam_k2, "diff_subln_g": diff_subln_g,
         "rel_bias": rel_bias, "w_out": w_out, "w_ple": w_ple, "w_ple_gate": w_ple_gate,
         "final_norm_g": final_norm_g}
    depth = w_in.shape[0]
    LW = [_layer_weights(l, W) for l in range(depth)]
    y_p, mla_p, dk_p, dv_p, wkv_p, sh_p = _run_prompt(x_prompt, p_prompt, W, LW, depth)
    y_s, mla_s, dk_s, dv_s, wkv_s, sh_s = _run_sample(x_sample, p_sample, state_shift, state_wkv, cache_mla,
                                                      cache_diff_k, cache_diff_v, page_table, W, LW, depth)
    return (y_p, y_s, mla_p, mla_s, dk_p, dk_s, dv_p, dv_s, wkv_p, wkv_s, sh_p, sh_s)
```

```python
import functools
import math

import jax
import jax.numpy as jnp
from jax import lax
from jax.experimental import pallas as pl
from jax.experimental.pallas import tpu as pltpu

F32 = jnp.float32
BF16 = jnp.bfloat16
HIGHEST = lax.Precision.HIGHEST

LANES = 128
SUBLANES = 8
VMEM_LIMIT_BYTES = 56 * 1024 * 1024

D_MODEL = 1024
HEAD_A = 64
D_A = 512
H_A = D_A // HEAD_A
W_LORA = 64
A_LORA = 64
A_IN = 3 * D_A + W_LORA + A_LORA
D_B = 256
DV_B = 64
H_B = D_B // DV_B
NOPE_B = 64
ROPE_B = 32
Q_RANK = 256
KV_RANK = 128
MLA_W = KV_RANK + ROPE_B
ROPE_THETA = 10000.0
D_C = 256
DV_C = 64
H_C = D_C // DV_C
DC = DV_C // 2
NUM_BUCKETS = 32
MAX_DISTANCE = 128
PLE_DIM = 256
PAGE_SIZE = 128
NEG_INF = -1e30
EPS = 1e-6
GN_EPS = 64e-5
SUBLN_EPS = 1e-5
MLA_SCALE = (NOPE_B + ROPE_B) ** -0.5
DIFF_SCALE = DC ** -0.5

SEG_U = (0, A_IN)
SEG_GA = (A_IN, A_IN + D_A)
SEG_CQ = (SEG_GA[1], SEG_GA[1] + Q_RANK)
SEG_CKV = (SEG_CQ[1], SEG_CQ[1] + 2 * LANES)
SEG_GB = (SEG_CKV[1], SEG_CKV[1] + D_B)
SEG_QC = (SEG_GB[1], SEG_GB[1] + D_C)
SEG_KC = (SEG_QC[1], SEG_QC[1] + D_C)
SEG_VC = (SEG_KC[1], SEG_KC[1] + D_C)
SEG_GC = (SEG_VC[1], SEG_VC[1] + D_C)
ALL_SEGS = (SEG_U, SEG_GA, SEG_CQ, SEG_CKV, SEG_GB, SEG_QC, SEG_KC, SEG_VC, SEG_GC)
IN_COLS_PERM = SEG_GC[1]


def _pack_segments(segs):
    out, pos = [], 0
    for a, b in segs:
        out.append((pos, pos + b - a))
        pos += b - a
    return tuple(out)


PROMPT_SEGS_N = (SEG_U, SEG_GA, SEG_GB, SEG_GC, SEG_KC)
PROMPT_SEGS_T = (SEG_CQ, SEG_CKV, SEG_QC, SEG_KC, SEG_VC)

RWKV_CHUNK = 64
ATTN_TILE = 256
DECODE_PAGES_PER_STEP = 32


def _cparams(semantics):
    return pltpu.CompilerParams(dimension_semantics=semantics, vmem_limit_bytes=VMEM_LIMIT_BYTES)


def _full(shape):
    n = len(shape)
    return pl.BlockSpec(shape, lambda *_: (0,) * n)


def _sigmoid(x):
    return 1.0 / (1.0 + jnp.exp(-x))


def _silu(x):
    return x * _sigmoid(x)


def _rms(x, g, eps):
    return x * lax.rsqrt(jnp.mean(x * x, axis=-1, keepdims=True) + eps) * g


def _dot(a, b, **kw):
    return jnp.dot(a, b, preferred_element_type=F32, **kw)


def _dot_nt(a, b, **kw):
    return lax.dot_general(a, b, (((1,), (1,)), ((), ())), preferred_element_type=F32, **kw)


def _dot_tn(a, b, **kw):
    return lax.dot_general(a, b, (((0,), (0,)), ((), ())), preferred_element_type=F32, **kw)


def _mm(a, b):
    return _dot(a.astype(BF16), b.astype(BF16))


def _mm_nt(a, b):
    return _dot_nt(a.astype(BF16), b.astype(BF16))


def _mm_tn(a, b):
    return _dot_tn(a.astype(BF16), b.astype(BF16))


def _inproj_kernel(h_ref, g_ref, w_ref, wt_ref, *out_refs, normalize, segs, segs_t, emit_xn):
    x = h_ref[...]
    xn = _rms(x, g_ref[...], EPS) if normalize else x
    xb = xn.astype(BF16)
    for o_ref, (a, b) in zip(out_refs, segs):
        o_ref[...] = _dot(xb, w_ref[:, a:b])
    for o_ref, (a, b) in zip(out_refs[len(segs):], segs_t):
        o_ref[...] = _dot_nt(wt_ref[a:b, :], xb)
    n_proj = len(segs) + len(segs_t)
    if emit_xn == "last_row":
        rows = x.shape[0]
        out_refs[n_proj][...] = xn[rows - 1:rows, :]
    elif emit_xn == "all":
        out_refs[n_proj][...] = xn


def _inproj(h2d, norm_g, w_bf16, segs, *, normalize, rows_per_seq, tm, wt_bf16=None, segs_t=()):
    m, d = h2d.shape
    assert m % tm == 0 and rows_per_seq % tm == 0 or rows_per_seq == 1
    tiles_per_seq = max(rows_per_seq // tm, 1)
    if wt_bf16 is None:
        wt_bf16 = jnp.zeros((SUBLANES, d), BF16)
    out_shapes = [jax.ShapeDtypeStruct((m, b - a), F32) for a, b in segs]
    out_specs = [pl.BlockSpec((tm, b - a), lambda i: (i, 0)) for a, b in segs]
    for a, b in segs_t:
        out_shapes.append(jax.ShapeDtypeStruct((m // rows_per_seq, b - a, rows_per_seq), F32))
        out_specs.append(pl.BlockSpec((None, b - a, tm), lambda i: (i // tiles_per_seq, 0, i % tiles_per_seq)))
    emit_xn = None
    if normalize:
        if rows_per_seq == 1:
            emit_xn = "all"
            out_shapes.append(jax.ShapeDtypeStruct((m, d), F32))
            out_specs.append(pl.BlockSpec((tm, d), lambda i: (i, 0)))
        else:
            emit_xn = "last_row"
            out_shapes.append(jax.ShapeDtypeStruct((m // rows_per_seq, 1, d), F32))
            out_specs.append(pl.BlockSpec((None, 1, d), lambda i: (i // tiles_per_seq, 0, 0)))
    kern = functools.partial(_inproj_kernel, normalize=normalize, segs=segs, segs_t=segs_t, emit_xn=emit_xn)
    return pl.pallas_call(
        kern,
        grid=(m // tm,),
        in_specs=[pl.BlockSpec((tm, d), lambda i: (i, 0)), _full((1, d)), _full(w_bf16.shape), _full(wt_bf16.shape)],
        out_specs=out_specs,
        out_shape=out_shapes,
        compiler_params=_cparams(("arbitrary",)),
        name="inproj",
    )(h2d, norm_g.reshape(1, d), w_bf16, wt_bf16)


def _outproj_kernel(h_ref, ya_ref, yb_ref, yc_ref, p_ref, wo_ref, wple_ref, wg_ref, fng_ref, o_ref, *, final):
    mixed = (_dot(ya_ref[...], wo_ref[0:D_A, :])
             + _dot(yb_ref[...], wo_ref[D_A:D_A + D_B, :])
             + _dot(yc_ref[...], wo_ref[D_A + D_B:, :]))
    h2 = h_ref[...] + mixed
    ple = _dot(p_ref[...].astype(BF16), wple_ref[...])
    gate = _sigmoid(_dot(h2.astype(BF16), wg_ref[...]))
    h3 = h2 + ple * gate
    o_ref[...] = _rms(h3, fng_ref[...], EPS) if final else h3


def _outproj(h2d, ya, yb, yc, p2d, wo, wple, wg, final_g, *, final, tm):
    m, d = h2d.shape
    row = lambda w: pl.BlockSpec((tm, w), lambda i: (i, 0))
    return pl.pallas_call(
        functools.partial(_outproj_kernel, final=final),
        grid=(m // tm,),
        in_specs=[row(d), row(D_A), row(D_B), row(D_C), row(PLE_DIM),
                  _full(wo.shape), _full(wple.shape), _full(wg.shape), _full((1, d))],
        out_specs=row(d),
        out_shape=jax.ShapeDtypeStruct((m, d), F32),
        compiler_params=_cparams(("arbitrary",)),
        name="outproj",
    )(h2d, ya, yb, yc, p2d, wo, wple, wg, final_g.reshape(1, d))


def _rwkv_prep(um, w0, w2, a0, a2, k_k, k_a):
    r = um[:, 0:D_A]
    k = um[:, D_A:2 * D_A]
    v = um[:, 2 * D_A:3 * D_A]
    w_lo = um[:, 3 * D_A:3 * D_A + W_LORA]
    a_lo = um[:, 3 * D_A + W_LORA:A_IN]
    wl = w0 + _dot(jnp.tanh(w_lo), w2, precision=HIGHEST)
    neg = -wl
    softplus = jnp.maximum(neg, 0.0) + jnp.log(1.0 + jnp.exp(-jnp.abs(neg)))
    w = -softplus - 0.5
    log_decay = -jnp.exp(w)
    a = _sigmoid(a0 + _dot(a_lo, a2, precision=HIGHEST))
    kk = k * k_k
    k = k * (1.0 + (a - 1.0) * k_a)
    return r, k, v, kk, a, log_decay


def _normalize_kk(kk_h):
    norm = jnp.sqrt(jnp.sum(kk_h * kk_h, axis=-1, keepdims=True))
    return kk_h / jnp.maximum(norm, 1e-12)


def _rwkv_head_out(y, r_h, k_h, v_h, rk_h, gng_h, gnb_h, gate_h):
    mu = jnp.mean(y, axis=-1, keepdims=True)
    var = jnp.mean(jnp.square(y - mu), axis=-1, keepdims=True)
    yn = (y - mu) * lax.rsqrt(var + GN_EPS) * gng_h + gnb_h
    bonus = jnp.sum(r_h * k_h * rk_h, axis=-1, keepdims=True) * v_h
    return (yn + bonus) * _silu(gate_h)


def _rwkv_chunk_kernel(u_ref, uprev0_ref, ga_ref, s0_ref, mu_ref, w0_ref, w2_ref, a0_ref, a2_ref,
                       kk_ref, ka_ref, rk_ref, gng_ref, gnb_ref, y_ref, s_ref, prev_ref):
    c = pl.program_id(1)
    C = u_ref.shape[0]

    @pl.when(c == 0)
    def _():
        prev_ref[...] = uprev0_ref[...]
        s_ref[...] = s0_ref[...]

    u = u_ref[...]
    row = lax.broadcasted_iota(jnp.int32, (C, 1), 0)
    u_prev = jnp.where(row == 0, prev_ref[...], pltpu.roll(u, 1, axis=0))
    prev_ref[...] = u[C - 1:C, :]
    um = u + mu_ref[...] * (u_prev - u)
    r, k, v, kk, a, log_decay = _rwkv_prep(um, w0_ref[...], w2_ref[...], a0_ref[...], a2_ref[...],
                                           kk_ref[...], ka_ref[...])

    ti = lax.broadcasted_iota(jnp.int32, (C, C), 0)
    tj = lax.broadcasted_iota(jnp.int32, (C, C), 1)
    incl = tj <= ti
    strict = tj < ti
    cs = _dot(incl.astype(F32), log_decay, precision=HIGHEST)
    cs_end = cs[C - 1:C, :]
    e_incl = jnp.exp(cs)
    e_excl = jnp.exp(cs - log_decay)
    e_inv = jnp.exp(-cs)
    e_rem = jnp.exp(cs_end - cs)
    p_end = jnp.exp(cs_end)
    eye = (ti == tj).astype(F32)
    ga = ga_ref[...]

    heads = range(H_A)
    hsl = [slice(h * HEAD_A, (h + 1) * HEAD_A) for h in heads]
    kk_n = [_normalize_kk(kk[:, hs]) for hs in hsl]
    b_f = [kk_n[h] * a[:, hsl[h]] for h in heads]
    a_t = [(-kk_n[h] * e_excl[:, hsl[h]]).astype(BF16) for h in heads]
    r_tf = [r[:, hsl[h]] * e_incl[:, hsl[h]] for h in heads]
    r_t = [x.astype(BF16) for x in r_tf]
    b_t = [(b_f[h] * e_inv[:, hsl[h]]).astype(BF16) for h in heads]
    k_t = [(k[:, hsl[h]] * e_inv[:, hsl[h]]).astype(BF16) for h in heads]
    b_end = [(b_f[h] * e_rem[:, hsl[h]]).astype(BF16) for h in heads]
    k_end = [(k[:, hsl[h]] * e_rem[:, hsl[h]]).astype(BF16) for h in heads]
    v_b = [v[:, hsl[h]].astype(BF16) for h in heads]

    l_ab = [jnp.where(strict, _mm_nt(a_t[h], b_t[h]), 0.0) for h in heads]
    l_ak = [jnp.where(strict, _mm_nt(a_t[h], k_t[h]), 0.0).astype(BF16) for h in heads]
    m_rb = [jnp.where(incl, _mm_nt(r_t[h], b_t[h]), 0.0).astype(BF16) for h in heads]
    m_rk = [jnp.where(incl, _mm_nt(r_t[h], k_t[h]), 0.0).astype(BF16) for h in heads]

    inv = [eye + l_ab[h] for h in heads]
    pw = [x.astype(BF16) for x in l_ab]
    for _ in range(int(math.log2(C)) - 1):
        pw = [_mm(pw[h], pw[h]).astype(BF16) for h in heads]
        inv = [inv[h] + _mm(inv[h], pw[h]) for h in heads]
    inv = [x.astype(BF16) for x in inv]

    lv = [_mm(l_ak[h], v_b[h]) for h in heads]
    w_mat = [_mm(inv[h], a_t[h]).astype(BF16) for h in heads]
    u_v = [_mm(inv[h], lv[h]).astype(BF16) for h in heads]
    q_eff = [r_tf[h] + _mm(m_rb[h], w_mat[h]) for h in heads]
    y_v = [_mm(m_rb[h], u_v[h]) + _mm(m_rk[h], v_b[h]) for h in heads]
    kr = lax.broadcasted_iota(jnp.int32, (HEAD_A, HEAD_A), 0)
    kc = lax.broadcasted_iota(jnp.int32, (HEAD_A, HEAD_A), 1)
    trans = [_mm_tn(w_mat[h], b_end[h]) + jnp.where(kr == kc, p_end[:, hsl[h]], 0.0) for h in heads]
    add = [_mm_tn(u_v[h], b_end[h]) + _mm_tn(v_b[h], k_end[h]) for h in heads]

    s_old = [s_ref[h] for h in heads]
    y = [_mm_nt(q_eff[h], s_old[h]) + y_v[h] for h in heads]
    for h in heads:
        s_ref[h] = _mm(s_old[h], trans[h]) + add[h]
    for h in heads:
        hs = hsl[h]
        out = _rwkv_head_out(y[h], r[:, hs], k[:, hs], v[:, hs], rk_ref[:, hs], gng_ref[:, hs], gnb_ref[:, hs],
                             ga[:, hs])
        y_ref[:, hs] = out.astype(y_ref.dtype)


def _rwkv_chunked(u, uprev0, ga, s0, params):
    b, t, _ = u.shape
    c = RWKV_CHUNK
    assert t % c == 0
    tok = lambda w: pl.BlockSpec((None, c, w), lambda i, j: (i, j, 0))
    state = pl.BlockSpec((None, H_A, HEAD_A, HEAD_A), lambda i, j: (i, 0, 0, 0))
    return pl.pallas_call(
        _rwkv_chunk_kernel,
        grid=(b, t // c),
        in_specs=[tok(A_IN), pl.BlockSpec((None, 1, A_IN), lambda i, j: (i, 0, 0)), tok(D_A), state]
                 + [_full(p.shape) for p in params],
        out_specs=[tok(D_A), state],
        out_shape=[jax.ShapeDtypeStruct((b, t, D_A), BF16), jax.ShapeDtypeStruct(s0.shape, F32)],
        scratch_shapes=[pltpu.VMEM((1, A_IN), F32)],
        compiler_params=_cparams(("parallel", "arbitrary")),
        name="rwkv_chunked",
    )(u, uprev0, ga, s0, *params)


def _rwkv_step_kernel(u_ref, uprev_ref, ga_ref, s0_ref, mu_ref, w0_ref, w2_ref, a0_ref, a2_ref,
                      kk_ref, ka_ref, rk_ref, gng_ref, gnb_ref, y_ref, s_ref):
    nb = u_ref.shape[0]
    u = u_ref[...]
    um = u + mu_ref[...] * (uprev_ref[...] - u)
    r, k, v, kk, a, log_decay = _rwkv_prep(um, w0_ref[...], w2_ref[...], a0_ref[...], a2_ref[...],
                                           kk_ref[...], ka_ref[...])
    decay = jnp.exp(log_decay)
    ga = ga_ref[...]
    ri = lax.broadcasted_iota(jnp.int32, (HEAD_A, HEAD_A), 0)
    ci = lax.broadcasted_iota(jnp.int32, (HEAD_A, HEAD_A), 1)
    eye = ri == ci
    for h in range(H_A):
        hs = slice(h * HEAD_A, (h + 1) * HEAD_A)
        kk_h = _normalize_kk(kk[:, hs])
        b_h = kk_h * a[:, hs]
        outs = []
        for i in range(nb):
            rs = slice(i, i + 1)
            s = s0_ref[i, h]
            sa = jnp.sum(s * (-kk_h[rs]), axis=-1, keepdims=True)
            v_col = jnp.sum(jnp.where(eye, v[rs, hs], 0.0), axis=-1, keepdims=True)
            s_new = s * decay[rs, hs] + sa * b_h[rs] + v_col * k[rs, hs]
            s_ref[i, h] = s_new
            y_col = jnp.sum(s_new * r[rs, hs], axis=-1, keepdims=True)
            outs.append(jnp.sum(jnp.where(eye, y_col, 0.0), axis=0, keepdims=True))
        y = jnp.concatenate(outs, axis=0)
        out = _rwkv_head_out(y, r[:, hs], k[:, hs], v[:, hs], rk_ref[:, hs], gng_ref[:, hs],
                             gnb_ref[:, hs], ga[:, hs])
        y_ref[:, hs] = out.astype(y_ref.dtype)


def _rwkv_step(u, uprev, ga, s0, params, *, nb=SUBLANES):
    b = u.shape[0]
    assert b % nb == 0
    row = lambda w: pl.BlockSpec((nb, w), lambda i: (i, 0))
    state = pl.BlockSpec((nb, H_A, HEAD_A, HEAD_A), lambda i: (i, 0, 0, 0))
    return pl.pallas_call(
        _rwkv_step_kernel,
        grid=(b // nb,),
        in_specs=[row(A_IN), row(A_IN), row(D_A), state] + [_full(p.shape) for p in params],
        out_specs=[row(D_A), state],
        out_shape=[jax.ShapeDtypeStruct((b, D_A), BF16), jax.ShapeDtypeStruct(s0.shape, F32)],
        compiler_params=_cparams(("parallel",)),
        name="rwkv_step",
    )(u, uprev, ga, s0, *params)


def _mla_prep_kernel(cq_ref, ckv_ref, cos_ref, sin_ref, qg_ref, wuq_ref, kvg_ref, wukt_ref, q_ref, rows_ref):
    cos2 = cos_ref[...]
    sin2 = sin_ref[...]
    qn = _rms(cq_ref[...], qg_ref[...], EPS).astype(BF16)
    q = _dot(qn, wuq_ref[...])
    for h in range(H_B):
        qh = q[:, h * LANES:(h + 1) * LANES]
        q_lat = _dot(qh[:, :NOPE_B].astype(BF16), wukt_ref[h])
        q_rope = qh[:, NOPE_B:NOPE_B + ROPE_B] * cos2 + qh[:, NOPE_B + ROPE_B:] * sin2
        q_ref[h, :, 0:KV_RANK] = (q_lat * MLA_SCALE).astype(q_ref.dtype)
        q_ref[h, :, KV_RANK:MLA_W] = (q_rope * MLA_SCALE).astype(q_ref.dtype)
    ckv = ckv_ref[...]
    rows_ref[:, 0:KV_RANK] = _rms(ckv[:, 0:KV_RANK], kvg_ref[...], EPS)
    rows_ref[:, KV_RANK:MLA_W] = (ckv[:, KV_RANK:KV_RANK + ROPE_B] * cos2
                                  + ckv[:, KV_RANK + ROPE_B:KV_RANK + 2 * ROPE_B] * sin2)


def _mla_prep(cq, ckv, cos2, sin2, q_norm_g, wuq_ext, kv_norm_g, wuk_t, *, tm, pos_tiles):
    m = cq.shape[0]
    row = lambda w: pl.BlockSpec((tm, w), lambda i: (i, 0))
    pos = pl.BlockSpec((tm, ROPE_B), lambda i: (i % pos_tiles, 0))
    return pl.pallas_call(
        _mla_prep_kernel,
        grid=(m // tm,),
        in_specs=[row(Q_RANK), row(2 * LANES), pos, pos, _full((1, Q_RANK)), _full(wuq_ext.shape),
                  _full((1, KV_RANK)), _full(wuk_t.shape)],
        out_specs=[pl.BlockSpec((H_B, tm, MLA_W), lambda i: (0, i, 0)), row(MLA_W)],
        out_shape=[jax.ShapeDtypeStruct((H_B, m, MLA_W), BF16), jax.ShapeDtypeStruct((m, MLA_W), F32)],
        compiler_params=_cparams(("parallel",)),
        name="mla_prep",
    )(cq, ckv, cos2, sin2, q_norm_g.reshape(1, Q_RANK), wuq_ext, kv_norm_g.reshape(1, KV_RANK), wuk_t)


def _softmax_update(s, m_ref, l_ref):
    m_old = m_ref[...]
    m_new = jnp.maximum(m_old, jnp.max(s, axis=-1, keepdims=True))
    alpha = jnp.exp(m_old - m_new)
    p = jnp.exp(s - m_new)
    l_ref[...] = alpha * l_ref[...] + jnp.sum(p, axis=-1, keepdims=True)
    m_ref[...] = m_new
    return alpha, p


def _mla_out(o_lat, wuv_ref, gb, o_ref, rows_per_head):
    for h in range(H_B):
        o_h = _dot(o_lat[h * rows_per_head:(h + 1) * rows_per_head].astype(BF16), wuv_ref[h])
        hs = slice(h * DV_B, (h + 1) * DV_B)
        o_ref[:, hs] = (o_h * _silu(gb[:, hs])).astype(o_ref.dtype)


def _mla_prep_t_kernel(cq_ref, ckv_ref, cos_ref, sin_ref, qg_ref, wuqt_ref, kvg_ref, wuk_ref,
                       q_ref, rows_ref, rowst_ref):
    tm = cq_ref.shape[1]
    cos2 = cos_ref[...]
    sin2 = sin_ref[...]
    cq = cq_ref[...]
    qn = (cq * lax.rsqrt(jnp.mean(cq * cq, axis=0, keepdims=True) + EPS) * qg_ref[...]).astype(BF16)
    q = _dot(wuqt_ref[...], qn)
    pad = jnp.zeros((2 * LANES - MLA_W, tm), q_ref.dtype)
    for h in range(H_B):
        qh = q[h * LANES:(h + 1) * LANES]
        q_lat = _dot(wuk_ref[h], qh[0:NOPE_B].astype(BF16))
        q_rope = qh[NOPE_B:NOPE_B + ROPE_B] * cos2 + qh[NOPE_B + ROPE_B:] * sin2
        cols = slice(h * tm, (h + 1) * tm)
        q_ref[0:KV_RANK, cols] = (q_lat * MLA_SCALE).astype(q_ref.dtype)
        q_ref[KV_RANK:MLA_W, cols] = (q_rope * MLA_SCALE).astype(q_ref.dtype)
        q_ref[MLA_W:, cols] = pad
    ckv = ckv_ref[...]
    c = ckv[0:KV_RANK]
    cn = c * lax.rsqrt(jnp.mean(c * c, axis=0, keepdims=True) + EPS) * kvg_ref[...]
    kr = ckv[KV_RANK:KV_RANK + ROPE_B] * cos2 + ckv[KV_RANK + ROPE_B:KV_RANK + 2 * ROPE_B] * sin2
    rowst_ref[0:KV_RANK, :] = cn
    rowst_ref[KV_RANK:MLA_W, :] = kr
    rows_t = jnp.concatenate([cn, kr, jnp.zeros((2 * LANES - MLA_W, tm), F32)], axis=0)
    rows_ref[...] = rows_t.T


def _mla_prep_t(cq_t, ckv_t, cos2_t, sin2_t, q_norm_g, wuq_ext_t, kv_norm_g, wuk, *, tm):
    b, _, t = cq_t.shape
    nt = t // tm
    blk = lambda w: pl.BlockSpec((None, w, tm), lambda bi, i: (bi, 0, i))
    pos = pl.BlockSpec((ROPE_B, tm), lambda bi, i: (0, i))
    return pl.pallas_call(
        _mla_prep_t_kernel,
        grid=(b, nt),
        in_specs=[blk(Q_RANK), blk(2 * LANES), pos, pos, _full((Q_RANK, 1)), _full(wuq_ext_t.shape),
                  _full((KV_RANK, 1)), _full(wuk.shape)],
        out_specs=[pl.BlockSpec((None, 2 * LANES, H_B * tm), lambda bi, i: (bi * nt + i, 0, 0)),
                   pl.BlockSpec((None, tm, 2 * LANES), lambda bi, i: (bi, i, 0)),
                   blk(MLA_W)],
        out_shape=[jax.ShapeDtypeStruct((b * nt, 2 * LANES, H_B * tm), BF16),
                   jax.ShapeDtypeStruct((b, t, 2 * LANES), F32),
                   jax.ShapeDtypeStruct((b, MLA_W, t), F32)],
        compiler_params=_cparams(("parallel", "parallel")),
        name="mla_prep_t",
    )(cq_t, ckv_t, cos2_t, sin2_t, q_norm_g.reshape(Q_RANK, 1), wuq_ext_t, kv_norm_g.reshape(KV_RANK, 1), wuk)


def _softmax_update_t(s, m_ref, l_ref, cols):
    m_old = m_ref[:, cols]
    m_new = jnp.maximum(m_old, jnp.max(s, axis=0, keepdims=True))
    alpha = jnp.exp(m_old - m_new)
    p = jnp.exp(s - m_new)
    l_ref[:, cols] = alpha * l_ref[:, cols] + jnp.sum(p, axis=0, keepdims=True)
    m_ref[:, cols] = m_new
    return alpha, p


def _mla_flash_kernel(q_ref, k_ref, ct_ref, mask_ref, gb_ref, wuvt_ref, o_ref, m_ref, l_ref, acc_ref):
    i = pl.program_id(1)
    j = pl.program_id(2)
    tq = o_ref.shape[0]

    @pl.when(j == 0)
    def _():
        m_ref[...] = jnp.full_like(m_ref, NEG_INF)
        l_ref[...] = jnp.zeros_like(l_ref)
        acc_ref[...] = jnp.zeros_like(acc_ref)

    @pl.when(j <= i)
    def _():
        k = k_ref[...].astype(BF16)
        ct = ct_ref[...].astype(BF16)
        mask = mask_ref[jnp.minimum(i - j, 1)]
        heads = range(H_B)
        cols = [slice(h * tq, (h + 1) * tq) for h in heads]
        s = [_dot(k, q_ref[:, cols[h]]) + mask for h in heads]
        ap = [_softmax_update_t(s[h], m_ref, l_ref, cols[h]) for h in heads]
        pv = [_dot(ct, ap[h][1].astype(BF16)) for h in heads]
        for h in heads:
            acc_ref[:, cols[h]] = ap[h][0] * acc_ref[:, cols[h]] + pv[h]

    @pl.when(j == i)
    def _():
        outs = []
        for h in range(H_B):
            cols = slice(h * tq, (h + 1) * tq)
            o_lat = (acc_ref[:, cols] / l_ref[:, cols]).astype(BF16)
            outs.append(_dot(wuvt_ref[h], o_lat))
        o = jnp.concatenate(outs, axis=0).T
        o_ref[...] = (o * _silu(gb_ref[...])).astype(o_ref.dtype)


def _mla_flash(q_t, rows_pad, rows_t, mask, gb, wuv_t):
    b, t, _ = rows_pad.shape
    tq = tk = ATTN_TILE
    nq = t // tq
    return pl.pallas_call(
        _mla_flash_kernel,
        grid=(b, nq, t // tk),
        in_specs=[pl.BlockSpec((None, 2 * LANES, H_B * tq), lambda bi, i, j: (bi * nq + i, 0, 0)),
                  pl.BlockSpec((None, tk, 2 * LANES), lambda bi, i, j: (bi, jnp.minimum(j, i), 0)),
                  pl.BlockSpec((None, KV_RANK, tk), lambda bi, i, j: (bi, 0, jnp.minimum(j, i))),
                  _full(mask.shape),
                  pl.BlockSpec((None, tq, D_B), lambda bi, i, j: (bi, i, 0)),
                  _full(wuv_t.shape)],
        out_specs=pl.BlockSpec((None, tq, D_B), lambda bi, i, j: (bi, i, 0)),
        out_shape=jax.ShapeDtypeStruct((b, t, D_B), BF16),
        scratch_shapes=[pltpu.VMEM((1, H_B * tq), F32), pltpu.VMEM((1, H_B * tq), F32),
                        pltpu.VMEM((KV_RANK, H_B * tq), F32)],
        compiler_params=_cparams(("parallel", "parallel", "arbitrary")),
        name="mla_flash",
    )(q_t, rows_pad, rows_t, mask, gb, wuv_t)


def _page_specs(block_tail, layer, pages_per_seq):
    n = len(block_tail)

    def spec(jj):
        return pl.BlockSpec((None, None) + block_tail,
                            lambda b, s, pt: (layer, pt[b * pages_per_seq + s * DECODE_PAGES_PER_STEP + jj])
                            + (0,) * n)
    return [spec(jj) for jj in range(DECODE_PAGES_PER_STEP)]


def _mla_decode_kernel(pt_ref, q_ref, row_ref, gb_ref, wuv_ref, *rest):
    pages = rest[:DECODE_PAGES_PER_STEP]
    o_ref, m_ref, l_ref, acc_ref = rest[DECODE_PAGES_PER_STEP:]
    step = pl.program_id(1)

    @pl.when(step == 0)
    def _():
        m_ref[...] = jnp.full_like(m_ref, NEG_INF)
        l_ref[...] = jnp.zeros_like(l_ref)
        acc_ref[...] = jnp.zeros_like(acc_ref)

    q = q_ref[...]
    ks = [pg[...].astype(BF16) for pg in pages]
    s = jnp.concatenate([_dot(q, kp) for kp in ks], axis=1)
    alpha, p = _softmax_update(s, m_ref, l_ref)
    pv = _dot_nt(p[:, 0:PAGE_SIZE].astype(BF16), ks[0][0:KV_RANK, :])
    for jj in range(1, DECODE_PAGES_PER_STEP):
        pv = pv + _dot_nt(p[:, jj * PAGE_SIZE:(jj + 1) * PAGE_SIZE].astype(BF16), ks[jj][0:KV_RANK, :])
    acc_ref[...] = alpha * acc_ref[...] + pv

    @pl.when(step == pl.num_programs(1) - 1)
    def _():
        row = row_ref[...]
        s_new = jnp.sum(q.astype(F32) * row, axis=-1, keepdims=True)
        alpha2, p_new = _softmax_update(s_new, m_ref, l_ref)
        acc = alpha2 * acc_ref[...] + p_new * row[:, 0:KV_RANK]
        _mla_out(acc / l_ref[...], wuv_ref, gb_ref[...], o_ref, 1)


def _mla_decode(page_table_flat, q, rows_new, gb, wuv, cache, layer, pages_per_seq):
    b = q.shape[0]
    steps = pages_per_seq // DECODE_PAGES_PER_STEP
    per_b = lambda shape: pl.BlockSpec((None,) + shape, lambda bi, s, pt: (bi,) + (0,) * len(shape))
    grid_spec = pltpu.PrefetchScalarGridSpec(
        num_scalar_prefetch=1,
        grid=(b, steps),
        in_specs=[per_b((H_B, MLA_W)), per_b((1, MLA_W)), per_b((1, D_B)),
                  pl.BlockSpec(wuv.shape, lambda bi, s, pt: (0, 0, 0))]
                 + _page_specs((MLA_W, PAGE_SIZE), layer, pages_per_seq),
        out_specs=per_b((1, D_B)),
        scratch_shapes=[pltpu.VMEM((H_B, 1), F32), pltpu.VMEM((H_B, 1), F32), pltpu.VMEM((H_B, KV_RANK), F32)],
    )
    return pl.pallas_call(
        _mla_decode_kernel,
        grid_spec=grid_spec,
        out_shape=jax.ShapeDtypeStruct((b, 1, D_B), BF16),
        compiler_params=_cparams(("parallel", "arbitrary")),
        name="mla_decode",
    )(page_table_flat, q, rows_new, gb, wuv, *([cache] * DECODE_PAGES_PER_STEP))


def _bias_kernel(rb_ref, dist_ref, o_ref):
    dist = dist_ref[...]
    n = jnp.maximum(dist, 0)
    max_exact = NUM_BUCKETS // 2
    n_safe = jnp.maximum(n, max_exact).astype(F32)
    large = max_exact + (jnp.log(n_safe / max_exact) / math.log(MAX_DISTANCE / max_exact)
                         * (NUM_BUCKETS - max_exact)).astype(jnp.int32)
    large = jnp.minimum(large, NUM_BUCKETS - 1)
    bucket = jnp.where(n < max_exact, n, large)
    for h in range(H_C):
        bias = jnp.zeros(dist.shape, F32)
        for kb in range(NUM_BUCKETS):
            bias = jnp.where(bucket == kb, rb_ref[kb * H_C + h], bias)
        o_ref[h] = jnp.where(dist >= 0, bias, NEG_INF)


def _bias_tiles(rel_bias, dist):
    g, r, c = dist.shape
    grid_spec = pltpu.PrefetchScalarGridSpec(
        num_scalar_prefetch=1,
        grid=(g,),
        in_specs=[pl.BlockSpec((None, r, c), lambda i, rb: (i, 0, 0))],
        out_specs=pl.BlockSpec((None, H_C, r, c), lambda i, rb: (i, 0, 0, 0)),
    )
    return pl.pallas_call(
        _bias_kernel,
        grid_spec=grid_spec,
        out_shape=jax.ShapeDtypeStruct((g, H_C, r, c), F32),
        compiler_params=_cparams(("arbitrary",)),
        name="rel_bias_tiles",
    )(rel_bias.reshape(-1), dist)


def _diff_lambda(lam_ref, lam_init):
    lam = lam_ref[...]
    e1 = jnp.exp(jnp.sum(lam[0:1] * lam[1:2], axis=-1, keepdims=True))
    e2 = jnp.exp(jnp.sum(lam[2:3] * lam[3:4], axis=-1, keepdims=True))
    return e1 - e2 + lam_init


def _diff_queries(qc):
    lane = lax.broadcasted_iota(jnp.int32, qc.shape, 1)
    qs = qc * DIFF_SCALE
    groups = []
    for h in range(H_C):
        for c in range(2):
            lo = h * DV_C + c * DC
            groups.append(jnp.where((lane >= lo) & (lane < lo + DC), qs, 0.0))
    return jnp.concatenate(groups, axis=0)


def _diff_out(acc, l, lam, lam_init, sg, gc, o_ref, rows):
    for h in range(H_C):
        hs = slice(h * DV_C, (h + 1) * DV_C)
        r1 = slice((2 * h) * rows, (2 * h + 1) * rows)
        r2 = slice((2 * h + 1) * rows, (2 * h + 2) * rows)
        o = acc[r1, hs] / l[r1] - lam * (acc[r2, hs] / l[r2])
        o = _rms(o, sg, SUBLN_EPS) * (1.0 - lam_init)
        o_ref[:, hs] = (o * _silu(gc[:, hs])).astype(o_ref.dtype)


def _diff_flash_kernel(q_ref, k_ref, vt_ref, gc_ref, bias_ref, lam_ref, sg_ref, o_ref,
                       qbd_ref, m_ref, l_ref, acc_ref, *, lam_init):
    i = pl.program_id(1)
    j = pl.program_id(2)
    tq = o_ref.shape[0]

    @pl.when(j == 0)
    def _():
        q = q_ref[...] * DIFF_SCALE
        feat = lax.broadcasted_iota(jnp.int32, q.shape, 0)
        for h in range(H_C):
            for c in range(2):
                lo = h * DV_C + c * DC
                blk = 2 * h + c
                qbd_ref[:, blk * tq:(blk + 1) * tq] = jnp.where((feat >= lo) & (feat < lo + DC), q, 0.0).astype(BF16)
        m_ref[...] = jnp.full_like(m_ref, NEG_INF)
        l_ref[...] = jnp.zeros_like(l_ref)
        acc_ref[...] = jnp.zeros_like(acc_ref)

    @pl.when(j <= i)
    def _():
        k = k_ref[...].astype(BF16)
        vt = vt_ref[...].astype(BF16)
        tile = jnp.minimum(i - j, 2)
        blocks = range(2 * H_C)
        cols = [slice(blk * tq, (blk + 1) * tq) for blk in blocks]
        s = [_dot(k, qbd_ref[:, cols[blk]]) + bias_ref[tile, blk // 2] for blk in blocks]
        ap = [_softmax_update_t(s[blk], m_ref, l_ref, cols[blk]) for blk in blocks]
        pv = [_dot(vt[(blk // 2) * DV_C:(blk // 2 + 1) * DV_C], ap[blk][1].astype(BF16)) for blk in blocks]
        for blk in blocks:
            acc_ref[blk] = ap[blk][0] * acc_ref[blk] + pv[blk]

    @pl.when(j == i)
    def _():
        lam = _diff_lambda(lam_ref, lam_init)
        outs = []
        for h in range(H_C):
            c1 = slice((2 * h) * tq, (2 * h + 1) * tq)
            c2 = slice((2 * h + 1) * tq, (2 * h + 2) * tq)
            o = acc_ref[2 * h] / l_ref[:, c1] - lam * (acc_ref[2 * h + 1] / l_ref[:, c2])
            o = o * lax.rsqrt(jnp.mean(o * o, axis=0, keepdims=True) + SUBLN_EPS) * sg_ref[...]
            outs.append(o * (1.0 - lam_init))
        o = jnp.concatenate(outs, axis=0).T
        o_ref[...] = (o * _silu(gc_ref[...])).astype(o_ref.dtype)


def _diff_flash(qc_t, kc, vc_t, gc, bias_tiles, lam_vecs, subln_g, lam_init):
    b, t, _ = kc.shape
    tq = tk = ATTN_TILE
    nblk = 2 * H_C
    qspec = pl.BlockSpec((None, tq, D_C), lambda bi, i, j: (bi, i, 0))
    return pl.pallas_call(
        functools.partial(_diff_flash_kernel, lam_init=lam_init),
        grid=(b, t // tq, t // tk),
        in_specs=[pl.BlockSpec((None, D_C, tq), lambda bi, i, j: (bi, 0, i)),
                  pl.BlockSpec((None, tk, D_C), lambda bi, i, j: (bi, jnp.minimum(j, i), 0)),
                  pl.BlockSpec((None, D_C, tk), lambda bi, i, j: (bi, 0, jnp.minimum(j, i))),
                  qspec, _full(bias_tiles.shape), _full(lam_vecs.shape), _full((DV_C, 1))],
        out_specs=qspec,
        out_shape=jax.ShapeDtypeStruct((b, t, D_C), BF16),
        scratch_shapes=[pltpu.VMEM((D_C, nblk * tq), BF16), pltpu.VMEM((1, nblk * tq), F32),
                        pltpu.VMEM((1, nblk * tq), F32), pltpu.VMEM((nblk, DV_C, tq), F32)],
        compiler_params=_cparams(("parallel", "parallel", "arbitrary")),
        name="diff_flash",
    )(qc_t, kc, vc_t, gc, bias_tiles, lam_vecs, subln_g.reshape(DV_C, 1))


def _diff_decode_kernel(pt_ref, q_ref, kn_ref, vn_ref, gc_ref, bias_ref, lam_ref, sg_ref, *rest, lam_init):
    n = DECODE_PAGES_PER_STEP
    kpages, vpages = rest[:n], rest[n:2 * n]
    o_ref, m_ref, l_ref, acc_ref = rest[2 * n:]
    step = pl.program_id(1)
    last = pl.num_programs(1) - 1

    @pl.when(step == 0)
    def _():
        m_ref[...] = jnp.full_like(m_ref, NEG_INF)
        l_ref[...] = jnp.zeros_like(l_ref)
        acc_ref[...] = jnp.zeros_like(acc_ref)

    qbd = _diff_queries(q_ref[...])
    qb = qbd.astype(BF16)
    s = jnp.concatenate([_dot(qb, kp[...].astype(BF16)) for kp in kpages], axis=1)
    s = s + bias_ref[jnp.where(step == last, 1, 0)]
    alpha, p = _softmax_update(s, m_ref, l_ref)
    pv = _dot_nt(p[:, 0:PAGE_SIZE].astype(BF16), vpages[0][...].astype(BF16))
    for jj in range(1, n):
        pv = pv + _dot_nt(p[:, jj * PAGE_SIZE:(jj + 1) * PAGE_SIZE].astype(BF16), vpages[jj][...].astype(BF16))
    acc_ref[...] = alpha * acc_ref[...] + pv

    @pl.when(step == last)
    def _():
        s_new = jnp.sum(qbd * kn_ref[...], axis=-1, keepdims=True) + bias_ref[2][:, 0:1]
        alpha2, p_new = _softmax_update(s_new, m_ref, l_ref)
        acc = alpha2 * acc_ref[...] + p_new * vn_ref[...]
        _diff_out(acc, l_ref[...], _diff_lambda(lam_ref, lam_init), lam_init, sg_ref[...], gc_ref[...], o_ref, 1)


def _diff_decode(page_table_flat, qc, kc, vc, gc, bias_rows, lam_vecs, subln_g, cache_k, cache_v, layer,
                 pages_per_seq, lam_init):
    b = qc.shape[0]
    steps = pages_per_seq // DECODE_PAGES_PER_STEP
    per_b = pl.BlockSpec((None, 1, D_C), lambda bi, s, pt: (bi, 0, 0))
    const = lambda shape: pl.BlockSpec(shape, lambda bi, s, pt: (0,) * len(shape))
    grid_spec = pltpu.PrefetchScalarGridSpec(
        num_scalar_prefetch=1,
        grid=(b, steps),
        in_specs=[per_b, per_b, per_b, per_b, const(bias_rows.shape), const(lam_vecs.shape), const((1, DV_C))]
                 + _page_specs((D_C, PAGE_SIZE), layer, pages_per_seq)
                 + _page_specs((D_C, PAGE_SIZE), layer, pages_per_seq),
        out_specs=per_b,
        scratch_shapes=[pltpu.VMEM((2 * H_C, 1), F32), pltpu.VMEM((2 * H_C, 1), F32),
                        pltpu.VMEM((2 * H_C, D_C), F32)],
    )
    return pl.pallas_call(
        functools.partial(_diff_decode_kernel, lam_init=lam_init),
        grid_spec=grid_spec,
        out_shape=jax.ShapeDtypeStruct((b, 1, D_C), BF16),
        compiler_params=_cparams(("parallel", "arbitrary")),
        name="diff_decode",
    )(page_table_flat, qc, kc, vc, gc, bias_rows, lam_vecs, subln_g.reshape(1, DV_C),
      *([cache_k] * DECODE_PAGES_PER_STEP), *([cache_v] * DECODE_PAGES_PER_STEP))


def _permute_w_in(w):
    o_ckv = A_IN + D_A + Q_RANK
    o_kr = o_ckv + KV_RANK
    o_gb = o_kr + ROPE_B
    half = ROPE_B // 2
    pad = jnp.zeros((w.shape[0], 2 * LANES - KV_RANK - 2 * ROPE_B), w.dtype)
    out = jnp.concatenate([w[:, :o_gb], w[:, o_kr + half:o_gb], w[:, o_kr:o_kr + half], pad, w[:, o_gb:]], axis=1)
    assert out.shape[1] == IN_COLS_PERM
    return out.astype(BF16)


def _extend_w_uq(w):
    w = w.reshape(Q_RANK, H_B, NOPE_B + ROPE_B)
    half = ROPE_B // 2
    rope = w[:, :, NOPE_B:]
    swapped = jnp.concatenate([rope[:, :, half:], rope[:, :, :half]], axis=-1)
    return jnp.concatenate([w, swapped], axis=-1).reshape(Q_RANK, H_B * LANES).astype(BF16)


def _rope_tables(pos):
    inv = ROPE_THETA ** (-jnp.arange(0, ROPE_B, 2, dtype=F32) / ROPE_B)
    ang = pos.astype(F32)[:, None] * inv[None, :]
    cos, sin = jnp.cos(ang), jnp.sin(ang)
    return jnp.concatenate([cos, cos], axis=-1), jnp.concatenate([-sin, sin], axis=-1)


def _layer_weights(l, W):
    row = lambda a: a.reshape(1, -1)
    rwkv = (row(W["mu_shift"][l]), row(W["rw_w0"][l]), W["rw_w2"][l], row(W["rw_a0"][l]), W["rw_a2"][l],
            row(W["rw_k_k"][l]), row(W["rw_k_a"][l]), row(W["rw_r_k"][l]), row(W["rw_gn_g"][l]),
            row(W["rw_gn_b"][l]))
    w_in = _permute_w_in(W["w_in"][l])
    wuq = _extend_w_uq(W["mla_w_uq"][l])
    return dict(
        w_in=w_in,
        w_in_n=jnp.concatenate([w_in[:, a:b] for a, b in PROMPT_SEGS_N], axis=1),
        w_in_t=jnp.concatenate([w_in[:, a:b] for a, b in PROMPT_SEGS_T], axis=1).T,
        rwkv=rwkv,
        wuq=wuq,
        wuq_t=wuq.T,
        wuk=jnp.transpose(W["mla_w_uk"][l], (1, 0, 2)).astype(BF16),
        wuv_t=jnp.transpose(W["mla_w_uv"][l], (1, 2, 0)).astype(BF16),
        wuk_t=jnp.transpose(W["mla_w_uk"][l], (1, 2, 0)).astype(BF16),
        wuv=jnp.transpose(W["mla_w_uv"][l], (1, 0, 2)).astype(BF16),
        lam_vecs=jnp.stack([W["diff_lam_q1"][l], W["diff_lam_k1"][l], W["diff_lam_q2"][l], W["diff_lam_k2"][l]]),
        wo=W["w_out"][l].astype(BF16),
        wple=W["w_ple"][l].astype(BF16),
        wg=W["w_ple_gate"][l].astype(BF16),
    )


def _run_prompt(x, p, W, LW, depth):
    b, t, d = x.shape
    m = b * t
    tm = ATTN_TILE
    cos2, sin2 = _rope_tables(jnp.arange(t, dtype=jnp.int32))
    cos2_t, sin2_t = cos2.T, sin2.T
    tile = jnp.arange(ATTN_TILE, dtype=jnp.int32)
    dist = (jnp.arange(3, dtype=jnp.int32)[:, None, None] * ATTN_TILE + tile[None, None, :] - tile[None, :, None])
    bias_tiles = _bias_tiles(W["rel_bias"], dist)
    causal = jnp.stack([jnp.where(dist[0] >= 0, 0.0, NEG_INF).astype(F32), jnp.zeros(dist.shape[1:], F32)])
    uprev0 = jnp.zeros((b, 1, A_IN), F32)
    s0 = jnp.zeros((b, H_A, HEAD_A, HEAD_A), F32)
    segs_n = _pack_segments(PROMPT_SEGS_N)
    segs_t = _pack_segments(PROMPT_SEGS_T)
    h = x.reshape(m, d)
    mla_rows, k_rows, v_rows, wkv_out, shift_out = [], [], [], [], []
    for l in range(depth):
        lw = LW[l]
        u, ga, gb, gc, kc, cq_t, ckv_t, qc_t, kc_t, vc_t, xn_last = _inproj(
            h, W["norm_g"][l], lw["w_in_n"], segs_n, normalize=True, rows_per_seq=t, tm=tm,
            wt_bf16=lw["w_in_t"], segs_t=segs_t)
        y_a, s_new = _rwkv_chunked(u.reshape(b, t, A_IN), uprev0, ga.reshape(b, t, D_A), s0, lw["rwkv"])
        q_t, rows_pad, rows_t = _mla_prep_t(cq_t, ckv_t, cos2_t, sin2_t, W["mla_q_norm_g"][l], lw["wuq_t"],
                                            W["mla_kv_norm_g"][l], lw["wuk"], tm=tm)
        y_b = _mla_flash(q_t, rows_pad, rows_t, causal, gb.reshape(b, t, D_B), lw["wuv_t"])
        lam_init = 0.8 - 0.6 * math.exp(-0.3 * l)
        y_c = _diff_flash(qc_t, kc.reshape(b, t, D_C), vc_t, gc.reshape(b, t, D_C), bias_tiles, lw["lam_vecs"],
                          W["diff_subln_g"][l], lam_init)
        h = _outproj(h, y_a.reshape(m, D_A), y_b.reshape(m, D_B), y_c.reshape(m, D_C), p[l].reshape(m, PLE_DIM),
                     lw["wo"], lw["wple"], lw["wg"], W["final_norm_g"], final=(l == depth - 1), tm=tm)
        mla_rows.append(jnp.transpose(rows_t, (0, 2, 1)))
        k_rows.append(jnp.transpose(kc_t.reshape(b, H_C, 2 * DC, t), (0, 3, 1, 2)))
        v_rows.append(jnp.transpose(vc_t.reshape(b, H_C, DV_C, t), (0, 3, 1, 2)))
        wkv_out.append(s_new)
        shift_out.append(xn_last.reshape(b, d))
    return (h.reshape(b, t, d), jnp.stack(mla_rows), jnp.stack(k_rows), jnp.stack(v_rows), jnp.stack(wkv_out),
            jnp.stack(shift_out))


def _run_sample(x, p, state_shift, state_wkv, cache_mla, cache_k, cache_v, page_table, W, LW, depth):
    b, t, d = x.shape
    assert t == 1
    pages_per_seq = page_table.shape[1]
    past_len = pages_per_seq * PAGE_SIZE
    tm = b
    cos2, sin2 = _rope_tables(jnp.full((b,), past_len, dtype=jnp.int32))
    pt_flat = page_table.reshape(-1).astype(jnp.int32)
    step_keys = DECODE_PAGES_PER_STEP * PAGE_SIZE
    key_in_step = jnp.arange(step_keys, dtype=jnp.int32)
    dist = jnp.stack([past_len - key_in_step,
                      past_len - (past_len - step_keys + key_in_step),
                      jnp.zeros((step_keys,), jnp.int32)])
    dist = jnp.broadcast_to(dist[:, None, :], (3, 2, step_keys))
    bias = _bias_tiles(W["rel_bias"], dist)
    bias_rows = bias.reshape(3, 2 * H_C, step_keys)
    to_feature_major = lambda c: jnp.transpose(c, (0, 1, 3, 4, 2)).reshape(c.shape[:2] + (D_C, PAGE_SIZE))
    cache_k2 = to_feature_major(cache_k)
    cache_v2 = to_feature_major(cache_v)
    cache_mla_t = jnp.transpose(cache_mla, (0, 1, 3, 2))
    h = x.reshape(b, d)
    mla_rows, k_rows, v_rows, wkv_out, shift_out = [], [], [], [], []
    for l in range(depth):
        lw = LW[l]
        u, ga, cq, ckv, gb, qc, kc, vc, gc, xn = _inproj(
            h, W["norm_g"][l], lw["w_in"], ALL_SEGS, normalize=True, rows_per_seq=1, tm=tm)
        (uprev,) = _inproj(state_shift[l], W["norm_g"][l], lw["w_in"][:, :A_IN], (SEG_U,), normalize=False,
                           rows_per_seq=1, tm=tm)
        y_a, s_new = _rwkv_step(u, uprev, ga, state_wkv[l], lw["rwkv"])
        q, rows = _mla_prep(cq, ckv, cos2, sin2, W["mla_q_norm_g"][l], lw["wuq"], W["mla_kv_norm_g"][l],
                            lw["wuk_t"], tm=tm, pos_tiles=1)
        y_b = _mla_decode(pt_flat, jnp.transpose(q, (1, 0, 2)), rows.reshape(b, 1, MLA_W), gb.reshape(b, 1, D_B),
                          lw["wuv"], cache_mla_t, l, pages_per_seq)
        lam_init = 0.8 - 0.6 * math.exp(-0.3 * l)
        y_c = _diff_decode(pt_flat, qc.reshape(b, 1, D_C), kc.reshape(b, 1, D_C), vc.reshape(b, 1, D_C),
                           gc.reshape(b, 1, D_C), bias_rows, lw["lam_vecs"], W["diff_subln_g"][l], cache_k2,
                           cache_v2, l, pages_per_seq, lam_init)
        h = _outproj(h, y_a, y_b.reshape(b, D_B), y_c.reshape(b, D_C), p[l].reshape(b, PLE_DIM),
                     lw["wo"], lw["wple"], lw["wg"], W["final_norm_g"], final=(l == depth - 1), tm=tm)
        mla_rows.append(rows.reshape(b, 1, MLA_W))
        k_rows.append(kc.reshape(b, 1, H_C, 2 * DC))
        v_rows.append(vc.reshape(b, 1, H_C, DV_C))
        wkv_out.append(s_new)
        shift_out.append(xn)
    return (h.reshape(b, 1, d), jnp.stack(mla_rows), jnp.stack(k_rows), jnp.stack(v_rows), jnp.stack(wkv_out),
            jnp.stack(shift_out))


def kernel(x_prompt, x_sample, cache_mla, cache_diff_k, cache_diff_v, state_wkv, state_shift, page_table,
           p_prompt, p_sample, norm_g, w_in, mu_shift, rw_w0, rw_w2, rw_a0, rw_a2, rw_k_k, rw_k_a, rw_r_k,
           rw_gn_g, rw_gn_b, mla_q_norm_g, mla_w_uq, mla_kv_norm_g, mla_w_uk, mla_w_uv, diff_lam_q1,
           diff_lam_k1, diff_lam_q2, diff_lam_k2, diff_subln_g, rel_bias, w_out, w_ple, w_ple_gate,
           final_norm_g):
    W = {"norm_g": norm_g, "w_in": w_in, "mu_shift": mu_shift, "rw_w0": rw_w0, "rw_w2": rw_w2, "rw_a0": rw_a0,
         "rw_a2": rw_a2, "rw_k_k": rw_k_k, "rw_k_a": rw_k_a, "rw_r_k": rw_r_k, "rw_gn_g": rw_gn_g,
         "rw_gn_b": rw_gn_b, "mla_q_norm_g": mla_q_norm_g, "mla_w_uq": mla_w_uq, "mla_kv_norm_g": mla_kv_norm_g,
         "mla_w_uk": mla_w_uk, "mla_w_uv": mla_w_uv, "diff_lam_q1": diff_lam_q1, "diff_lam_k1": diff_lam_k1,
         "diff_lam_q2": diff_lam_q2, "diff_lam_k2": diff_lam_k2, "diff_subln_g": diff_subln_g,
         "rel_bias": rel_bias, "w_out": w_out, "w_ple": w_ple, "w_ple_gate": w_ple_gate,
         "final_norm_g": final_norm_g}
    depth = w_in.shape[0]
    LW = [_layer_weights(l, W) for l in range(depth)]
    y_p, mla_p, dk_p, dv_p, wkv_p, sh_p = _run_prompt(x_prompt, p_prompt, W, LW, depth)
    y_s, mla_s, dk_s, dv_s, wkv_s, sh_s = _run_sample(x_sample, p_sample, state_shift, state_wkv, cache_mla,
                                                      cache_diff_k, cache_diff_v, page_table, W, LW, depth)
    return (y_p, y_s, mla_p, mla_s, dk_p, dk_s, dv_p, dv_s, wkv_p, wkv_s, sh_p, sh_s)
```

```python
import functools
import math

import jax
import jax.numpy as jnp
from jax import lax
from jax.experimental import pallas as pl
from jax.experimental.pallas import tpu as pltpu

F32 = jnp.float32
BF16 = jnp.bfloat16
HIGHEST = lax.Precision.HIGHEST

LANES = 128
SUBLANES = 8
VMEM_LIMIT_BYTES = 56 * 1024 * 1024

D_MODEL = 1024
HEAD_A = 64
D_A = 512
H_A = D_A // HEAD_A
W_LORA = 64
A_LORA = 64
A_IN = 3 * D_A + W_LORA + A_LORA
D_B = 256
DV_B = 64
H_B = D_B // DV_B
NOPE_B = 64
ROPE_B = 32
Q_RANK = 256
KV_RANK = 128
MLA_W = KV_RANK + ROPE_B
ROPE_THETA = 10000.0
D_C = 256
DV_C = 64
H_C = D_C // DV_C
DC = DV_C // 2
NUM_BUCKETS = 32
MAX_DISTANCE = 128
PLE_DIM = 256
PAGE_SIZE = 128
NEG_INF = -1e30
EPS = 1e-6
GN_EPS = 64e-5
SUBLN_EPS = 1e-5
MLA_SCALE = (NOPE_B + ROPE_B) ** -0.5
DIFF_SCALE = DC ** -0.5

SEG_U = (0, A_IN)
SEG_GA = (A_IN, A_IN + D_A)
SEG_CQ = (SEG_GA[1], SEG_GA[1] + Q_RANK)
SEG_CKV = (SEG_CQ[1], SEG_CQ[1] + 2 * LANES)
SEG_GB = (SEG_CKV[1], SEG_CKV[1] + D_B)
SEG_QC = (SEG_GB[1], SEG_GB[1] + D_C)
SEG_KC = (SEG_QC[1], SEG_QC[1] + D_C)
SEG_VC = (SEG_KC[1], SEG_KC[1] + D_C)
SEG_GC = (SEG_VC[1], SEG_VC[1] + D_C)
ALL_SEGS = (SEG_U, SEG_GA, SEG_CQ, SEG_CKV, SEG_GB, SEG_QC, SEG_KC, SEG_VC, SEG_GC)
IN_COLS_PERM = SEG_GC[1]


def _pack_segments(segs):
    out, pos = [], 0
    for a, b in segs:
        out.append((pos, pos + b - a))
        pos += b - a
    return tuple(out)


PROMPT_SEGS_N = (SEG_U, SEG_GA, SEG_GB, SEG_GC, SEG_KC)
PROMPT_SEGS_T = (SEG_CQ, SEG_CKV, SEG_QC, SEG_KC, SEG_VC)

RWKV_CHUNK = 64
RWKV_BLOCK = 128
ATTN_TILE = 256
DECODE_PAGES_PER_STEP = 32


def _cparams(semantics):
    return pltpu.CompilerParams(dimension_semantics=semantics, vmem_limit_bytes=VMEM_LIMIT_BYTES)


def _full(shape):
    n = len(shape)
    return pl.BlockSpec(shape, lambda *_: (0,) * n)


def _sigmoid(x):
    return 1.0 / (1.0 + jnp.exp(-x))


def _silu(x):
    return x * _sigmoid(x)


def _rms(x, g, eps):
    return x * lax.rsqrt(jnp.mean(x * x, axis=-1, keepdims=True) + eps) * g


def _dot(a, b, **kw):
    return jnp.dot(a, b, preferred_element_type=F32, **kw)


def _dot_nt(a, b, **kw):
    return lax.dot_general(a, b, (((1,), (1,)), ((), ())), preferred_element_type=F32, **kw)


def _dot_tn(a, b, **kw):
    return lax.dot_general(a, b, (((0,), (0,)), ((), ())), preferred_element_type=F32, **kw)


def _split3(x):
    hi = x.astype(BF16)
    rest = x - hi.astype(F32)
    mid = rest.astype(BF16)
    lo = (rest - mid.astype(F32)).astype(BF16)
    return hi, mid, lo


def _dot_split_rhs(a_exact, b):
    hi, mid, lo = _split3(b)
    return _dot(a_exact, hi) + (_dot(a_exact, mid) + _dot(a_exact, lo))


def _dot_split_lhs(a, b_exact):
    hi, mid, lo = _split3(a)
    return _dot(hi, b_exact) + (_dot(mid, b_exact) + _dot(lo, b_exact))


def _mm(a, b):
    return _dot(a.astype(BF16), b.astype(BF16))


def _mm_nt(a, b):
    return _dot_nt(a.astype(BF16), b.astype(BF16))


def _mm_tn(a, b):
    return _dot_tn(a.astype(BF16), b.astype(BF16))


def _inproj_kernel(h_ref, g_ref, w_ref, wt_ref, *out_refs, normalize, segs, segs_t, emit_xn):
    x = h_ref[...]
    xn = _rms(x, g_ref[...], EPS) if normalize else x
    xb = xn.astype(BF16)
    for o_ref, (a, b) in zip(out_refs, segs):
        o_ref[...] = _dot(xb, w_ref[:, a:b])
    for o_ref, (a, b) in zip(out_refs[len(segs):], segs_t):
        o_ref[...] = _dot_nt(wt_ref[a:b, :], xb)
    n_proj = len(segs) + len(segs_t)
    if emit_xn == "last_row":
        rows = x.shape[0]
        out_refs[n_proj][...] = xn[rows - 1:rows, :]
    elif emit_xn == "all":
        out_refs[n_proj][...] = xn


def _inproj(h2d, norm_g, w_bf16, segs, *, normalize, rows_per_seq, tm, wt_bf16=None, segs_t=()):
    m, d = h2d.shape
    assert m % tm == 0 and rows_per_seq % tm == 0 or rows_per_seq == 1
    tiles_per_seq = max(rows_per_seq // tm, 1)
    if wt_bf16 is None:
        wt_bf16 = jnp.zeros((SUBLANES, d), BF16)
    out_shapes = [jax.ShapeDtypeStruct((m, b - a), F32) for a, b in segs]
    out_specs = [pl.BlockSpec((tm, b - a), lambda i: (i, 0)) for a, b in segs]
    for a, b in segs_t:
        out_shapes.append(jax.ShapeDtypeStruct((m // rows_per_seq, b - a, rows_per_seq), F32))
        out_specs.append(pl.BlockSpec((None, b - a, tm), lambda i: (i // tiles_per_seq, 0, i % tiles_per_seq)))
    emit_xn = None
    if normalize:
        if rows_per_seq == 1:
            emit_xn = "all"
            out_shapes.append(jax.ShapeDtypeStruct((m, d), F32))
            out_specs.append(pl.BlockSpec((tm, d), lambda i: (i, 0)))
        else:
            emit_xn = "last_row"
            out_shapes.append(jax.ShapeDtypeStruct((m // rows_per_seq, 1, d), F32))
            out_specs.append(pl.BlockSpec((None, 1, d), lambda i: (i // tiles_per_seq, 0, 0)))
    kern = functools.partial(_inproj_kernel, normalize=normalize, segs=segs, segs_t=segs_t, emit_xn=emit_xn)
    return pl.pallas_call(
        kern,
        grid=(m // tm,),
        in_specs=[pl.BlockSpec((tm, d), lambda i: (i, 0)), _full((1, d)), _full(w_bf16.shape), _full(wt_bf16.shape)],
        out_specs=out_specs,
        out_shape=out_shapes,
        compiler_params=_cparams(("arbitrary",)),
        name="inproj",
    )(h2d, norm_g.reshape(1, d), w_bf16, wt_bf16)


def _outproj_kernel(h_ref, ya_ref, yb_ref, yc_ref, p_ref, wo_ref, wple_ref, wg_ref, fng_ref, o_ref, *, final):
    mixed = (_dot(ya_ref[...], wo_ref[0:D_A, :])
             + _dot(yb_ref[...], wo_ref[D_A:D_A + D_B, :])
             + _dot(yc_ref[...], wo_ref[D_A + D_B:, :]))
    h2 = h_ref[...] + mixed
    ple = _dot(p_ref[...].astype(BF16), wple_ref[...])
    gate = _sigmoid(_dot(h2.astype(BF16), wg_ref[...]))
    h3 = h2 + ple * gate
    o_ref[...] = _rms(h3, fng_ref[...], EPS) if final else h3


def _outproj(h2d, ya, yb, yc, p2d, wo, wple, wg, final_g, *, final, tm):
    m, d = h2d.shape
    row = lambda w: pl.BlockSpec((tm, w), lambda i: (i, 0))
    return pl.pallas_call(
        functools.partial(_outproj_kernel, final=final),
        grid=(m // tm,),
        in_specs=[row(d), row(D_A), row(D_B), row(D_C), row(PLE_DIM),
                  _full(wo.shape), _full(wple.shape), _full(wg.shape), _full((1, d))],
        out_specs=row(d),
        out_shape=jax.ShapeDtypeStruct((m, d), F32),
        compiler_params=_cparams(("arbitrary",)),
        name="outproj",
    )(h2d, ya, yb, yc, p2d, wo, wple, wg, final_g.reshape(1, d))


def _rwkv_prep(um, w0, w2, a0, a2, k_k, k_a):
    r = um[:, 0:D_A]
    k = um[:, D_A:2 * D_A]
    v = um[:, 2 * D_A:3 * D_A]
    w_lo = um[:, 3 * D_A:3 * D_A + W_LORA]
    a_lo = um[:, 3 * D_A + W_LORA:A_IN]
    wl = w0 + _mm(jnp.tanh(w_lo), w2)
    neg = -wl
    softplus = jnp.maximum(neg, 0.0) + jnp.log(1.0 + jnp.exp(-jnp.abs(neg)))
    w = -softplus - 0.5
    log_decay = -jnp.exp(w)
    a = _sigmoid(a0 + _mm(a_lo, a2))
    kk = k * k_k
    k = k * (1.0 + (a - 1.0) * k_a)
    return r, k, v, kk, a, log_decay


def _normalize_kk(kk_h):
    norm = jnp.sqrt(jnp.sum(kk_h * kk_h, axis=-1, keepdims=True))
    return kk_h / jnp.maximum(norm, 1e-12)


def _rwkv_head_out(y, r_h, k_h, v_h, rk_h, gng_h, gnb_h, gate_h):
    mu = jnp.mean(y, axis=-1, keepdims=True)
    var = jnp.mean(jnp.square(y - mu), axis=-1, keepdims=True)
    yn = (y - mu) * lax.rsqrt(var + GN_EPS) * gng_h + gnb_h
    bonus = jnp.sum(r_h * k_h * rk_h, axis=-1, keepdims=True) * v_h
    return (yn + bonus) * _silu(gate_h)


def _rwkv_chunk_kernel(u_ref, uprev0_ref, ga_ref, s0_ref, ones_ref, mu_ref, w0_ref, w2_ref, a0_ref, a2_ref,
                       kk_ref, ka_ref, rk_ref, gng_ref, gnb_ref, y_ref, s_ref, prev_ref):
    step = pl.program_id(1)
    R = u_ref.shape[0]
    C = RWKV_CHUNK
    subs = range(R // C)

    @pl.when(step == 0)
    def _():
        prev_ref[...] = uprev0_ref[...]
        s_ref[...] = s0_ref[...]

    u = u_ref[...]
    row = lax.broadcasted_iota(jnp.int32, (R, 1), 0)
    u_prev = jnp.where(row == 0, prev_ref[...], pltpu.roll(u, 1, axis=0))
    prev_ref[...] = u[R - 1:R, :]
    um = u + mu_ref[...] * (u_prev - u)
    r, k, v, kk, a, log_decay = _rwkv_prep(um, w0_ref[...], w2_ref[...], a0_ref[...], a2_ref[...],
                                           kk_ref[...], ka_ref[...])

    ri = lax.broadcasted_iota(jnp.int32, (R, R), 0)
    rj = lax.broadcasted_iota(jnp.int32, (R, R), 1)
    same_chunk = (ri // C) == (rj // C)
    cs = _dot_split_rhs((same_chunk & (rj <= ri)).astype(BF16), log_decay)
    cs_last = [cs[(sb + 1) * C - 1:(sb + 1) * C, :] for sb in subs]
    cs_end = jnp.concatenate([jnp.broadcast_to(x, (C, x.shape[1])) for x in cs_last], axis=0)
    p_end = [jnp.exp(x) for x in cs_last]
    e_inv = jnp.exp(-cs)
    e_rem = jnp.exp(cs_end - cs)
    kk_n = kk / jnp.maximum(jnp.sqrt(_dot_split_lhs(kk * kk, ones_ref[...])), 1e-12)
    b_f = kk_n * a
    a_t = (-kk_n * jnp.exp(cs - log_decay)).astype(BF16)
    r_t = (r * jnp.exp(cs)).astype(BF16)
    b_t = (b_f * e_inv).astype(BF16)
    k_t = (k * e_inv).astype(BF16)
    b_end = (b_f * e_rem).astype(BF16)
    k_end = (k * e_rem).astype(BF16)
    v_b = v.astype(BF16)
    ga = ga_ref[...]

    t2 = lax.broadcasted_iota(jnp.int32, (C, 2 * C), 0)
    j2 = lax.broadcasted_iota(jnp.int32, (C, 2 * C), 1)
    j2 = jnp.where(j2 >= C, j2 - C, j2)
    strict2 = j2 < t2
    incl2 = j2 <= t2
    ti = lax.broadcasted_iota(jnp.int32, (C, C), 0)
    tj = lax.broadcasted_iota(jnp.int32, (C, C), 1)
    eye = (ti == tj).astype(F32)
    zeros = jnp.zeros((C, HEAD_A), BF16)

    units = [(sb, h) for sb in subs for h in range(H_A)]
    blk = lambda x, sb, h: x[sb * C:(sb + 1) * C, h * HEAD_A:(h + 1) * HEAD_A]
    a_h = {un: blk(a_t, *un) for un in units}
    r_h = {un: blk(r_t, *un) for un in units}
    v_h = {un: blk(v_b, *un) for un in units}
    gram = {un: _dot_nt(jnp.concatenate([a_h[un], r_h[un]], axis=0),
                        jnp.concatenate([blk(b_t, *un), blk(k_t, *un)], axis=0)) for un in units}
    l_top = {un: jnp.where(strict2, gram[un][0:C], 0.0) for un in units}
    m_bot = {un: jnp.where(incl2, gram[un][C:], 0.0).astype(BF16) for un in units}
    lv = {un: _dot(l_top[un].astype(BF16), jnp.concatenate([zeros, v_h[un]], axis=0)) for un in units}

    l_ab = {un: l_top[un][:, 0:C] for un in units}
    inv = {un: eye + l_ab[un] for un in units}
    pw = {un: l_ab[un].astype(BF16) for un in units}
    for _ in range(int(math.log2(C)) - 1):
        pw = {un: _mm(pw[un], pw[un]).astype(BF16) for un in units}
        inv = {un: inv[un] + _mm(inv[un], pw[un]) for un in units}
    inv = {un: inv[un].astype(BF16) for un in units}
    w_mat = {un: _mm(inv[un], a_h[un]).astype(BF16) for un in units}
    u_v = {un: _mm(inv[un], lv[un]) for un in units}

    state = [s_ref[h] for h in range(H_A)]
    for sb in subs:
        heads = [(sb, h) for h in range(H_A)]
        state_b = [x.astype(BF16) for x in state]
        uv = [jnp.concatenate([(_dot_nt(w_mat[un], state_b[un[1]]) + u_v[un]).astype(BF16), v_h[un]], axis=0)
              for un in heads]
        y = [_dot_nt(r_h[un], state_b[un[1]]) + _dot(m_bot[un], uv[un[1]]) for un in heads]
        state = [state[h] * p_end[sb][:, h * HEAD_A:(h + 1) * HEAD_A]
                 + _dot_tn(uv[h], jnp.concatenate([blk(b_end, sb, h), blk(k_end, sb, h)], axis=0))
                 for h in range(H_A)]
        rows = slice(sb * C, (sb + 1) * C)
        for h in range(H_A):
            hs = slice(h * HEAD_A, (h + 1) * HEAD_A)
            out = _rwkv_head_out(y[h], r[rows, hs], k[rows, hs], v[rows, hs], rk_ref[:, hs], gng_ref[:, hs],
                                 gnb_ref[:, hs], ga[rows, hs])
            y_ref[rows, hs] = out.astype(y_ref.dtype)
    for h in range(H_A):
        s_ref[h] = state[h]


def _rwkv_chunked(u, uprev0, ga, s0, params):
    b, t, _ = u.shape
    c = RWKV_BLOCK
    assert t % c == 0
    head_of_lane = jnp.arange(D_A, dtype=jnp.int32) // HEAD_A
    head_ones = (head_of_lane[:, None] == head_of_lane[None, :]).astype(BF16)
    tok = lambda w: pl.BlockSpec((None, c, w), lambda i, j: (i, j, 0))
    state = pl.BlockSpec((None, H_A, HEAD_A, HEAD_A), lambda i, j: (i, 0, 0, 0))
    return pl.pallas_call(
        _rwkv_chunk_kernel,
        grid=(b, t // c),
        in_specs=[tok(A_IN), pl.BlockSpec((None, 1, A_IN), lambda i, j: (i, 0, 0)), tok(D_A), state,
                  _full(head_ones.shape)] + [_full(p.shape) for p in params],
        out_specs=[tok(D_A), state],
        out_shape=[jax.ShapeDtypeStruct((b, t, D_A), BF16), jax.ShapeDtypeStruct(s0.shape, F32)],
        scratch_shapes=[pltpu.VMEM((1, A_IN), F32)],
        compiler_params=_cparams(("parallel", "arbitrary")),
        name="rwkv_chunked",
    )(u, uprev0, ga, s0, head_ones, *params)


def _rwkv_step_kernel(u_ref, uprev_ref, ga_ref, s0_ref, mu_ref, w0_ref, w2_ref, a0_ref, a2_ref,
                      kk_ref, ka_ref, rk_ref, gng_ref, gnb_ref, y_ref, s_ref):
    nb = u_ref.shape[0]
    u = u_ref[...]
    um = u + mu_ref[...] * (uprev_ref[...] - u)
    r, k, v, kk, a, log_decay = _rwkv_prep(um, w0_ref[...], w2_ref[...], a0_ref[...], a2_ref[...],
                                           kk_ref[...], ka_ref[...])
    decay = jnp.exp(log_decay)
    ga = ga_ref[...]
    ri = lax.broadcasted_iota(jnp.int32, (HEAD_A, HEAD_A), 0)
    ci = lax.broadcasted_iota(jnp.int32, (HEAD_A, HEAD_A), 1)
    eye = ri == ci
    for h in range(H_A):
        hs = slice(h * HEAD_A, (h + 1) * HEAD_A)
        kk_h = _normalize_kk(kk[:, hs])
        b_h = kk_h * a[:, hs]
        outs = []
        for i in range(nb):
            rs = slice(i, i + 1)
            s = s0_ref[i, h]
            sa = jnp.sum(s * (-kk_h[rs]), axis=-1, keepdims=True)
            v_col = jnp.sum(jnp.where(eye, v[rs, hs], 0.0), axis=-1, keepdims=True)
            s_new = s * decay[rs, hs] + sa * b_h[rs] + v_col * k[rs, hs]
            s_ref[i, h] = s_new
            y_col = jnp.sum(s_new * r[rs, hs], axis=-1, keepdims=True)
            outs.append(jnp.sum(jnp.where(eye, y_col, 0.0), axis=0, keepdims=True))
        y = jnp.concatenate(outs, axis=0)
        out = _rwkv_head_out(y, r[:, hs], k[:, hs], v[:, hs], rk_ref[:, hs], gng_ref[:, hs],
                             gnb_ref[:, hs], ga[:, hs])
        y_ref[:, hs] = out.astype(y_ref.dtype)


def _rwkv_step(u, uprev, ga, s0, params, *, nb=SUBLANES):
    b = u.shape[0]
    assert b % nb == 0
    row = lambda w: pl.BlockSpec((nb, w), lambda i: (i, 0))
    state = pl.BlockSpec((nb, H_A, HEAD_A, HEAD_A), lambda i: (i, 0, 0, 0))
    return pl.pallas_call(
        _rwkv_step_kernel,
        grid=(b // nb,),
        in_specs=[row(A_IN), row(A_IN), row(D_A), state] + [_full(p.shape) for p in params],
        out_specs=[row(D_A), state],
        out_shape=[jax.ShapeDtypeStruct((b, D_A), BF16), jax.ShapeDtypeStruct(s0.shape, F32)],
        compiler_params=_cparams(("parallel",)),
        name="rwkv_step",
    )(u, uprev, ga, s0, *params)


def _mla_prep_kernel(cq_ref, ckv_ref, cos_ref, sin_ref, qg_ref, wuq_ref, kvg_ref, wukt_ref, q_ref, rows_ref):
    cos2 = cos_ref[...]
    sin2 = sin_ref[...]
    qn = _rms(cq_ref[...], qg_ref[...], EPS).astype(BF16)
    q = _dot(qn, wuq_ref[...])
    for h in range(H_B):
        qh = q[:, h * LANES:(h + 1) * LANES]
        q_lat = _dot(qh[:, :NOPE_B].astype(BF16), wukt_ref[h])
        q_rope = qh[:, NOPE_B:NOPE_B + ROPE_B] * cos2 + qh[:, NOPE_B + ROPE_B:] * sin2
        q_ref[h, :, 0:KV_RANK] = (q_lat * MLA_SCALE).astype(q_ref.dtype)
        q_ref[h, :, KV_RANK:MLA_W] = (q_rope * MLA_SCALE).astype(q_ref.dtype)
    ckv = ckv_ref[...]
    rows_ref[:, 0:KV_RANK] = _rms(ckv[:, 0:KV_RANK], kvg_ref[...], EPS)
    rows_ref[:, KV_RANK:MLA_W] = (ckv[:, KV_RANK:KV_RANK + ROPE_B] * cos2
                                  + ckv[:, KV_RANK + ROPE_B:KV_RANK + 2 * ROPE_B] * sin2)


def _mla_prep(cq, ckv, cos2, sin2, q_norm_g, wuq_ext, kv_norm_g, wuk_t, *, tm, pos_tiles):
    m = cq.shape[0]
    row = lambda w: pl.BlockSpec((tm, w), lambda i: (i, 0))
    pos = pl.BlockSpec((tm, ROPE_B), lambda i: (i % pos_tiles, 0))
    return pl.pallas_call(
        _mla_prep_kernel,
        grid=(m // tm,),
        in_specs=[row(Q_RANK), row(2 * LANES), pos, pos, _full((1, Q_RANK)), _full(wuq_ext.shape),
                  _full((1, KV_RANK)), _full(wuk_t.shape)],
        out_specs=[pl.BlockSpec((H_B, tm, MLA_W), lambda i: (0, i, 0)), row(MLA_W)],
        out_shape=[jax.ShapeDtypeStruct((H_B, m, MLA_W), BF16), jax.ShapeDtypeStruct((m, MLA_W), F32)],
        compiler_params=_cparams(("parallel",)),
        name="mla_prep",
    )(cq, ckv, cos2, sin2, q_norm_g.reshape(1, Q_RANK), wuq_ext, kv_norm_g.reshape(1, KV_RANK), wuk_t)


def _softmax_update(s, m_ref, l_ref):
    m_old = m_ref[...]
    m_new = jnp.maximum(m_old, jnp.max(s, axis=-1, keepdims=True))
    alpha = jnp.exp(m_old - m_new)
    p = jnp.exp(s - m_new)
    l_ref[...] = alpha * l_ref[...] + jnp.sum(p, axis=-1, keepdims=True)
    m_ref[...] = m_new
    return alpha, p


def _mla_out(o_lat, wuv_ref, gb, o_ref, rows_per_head):
    for h in range(H_B):
        o_h = _dot(o_lat[h * rows_per_head:(h + 1) * rows_per_head].astype(BF16), wuv_ref[h])
        hs = slice(h * DV_B, (h + 1) * DV_B)
        o_ref[:, hs] = (o_h * _silu(gb[:, hs])).astype(o_ref.dtype)


def _mla_prep_t_kernel(cq_ref, ckv_ref, cos_ref, sin_ref, qg_ref, wuqt_ref, kvg_ref, wuk_ref,
                       q_ref, rows_ref, rowst_ref):
    tm = cq_ref.shape[1]
    cos2 = cos_ref[...]
    sin2 = sin_ref[...]
    cq = cq_ref[...]
    qn = (cq * lax.rsqrt(jnp.mean(cq * cq, axis=0, keepdims=True) + EPS) * qg_ref[...]).astype(BF16)
    q = _dot(wuqt_ref[...], qn)
    pad = jnp.zeros((2 * LANES - MLA_W, tm), q_ref.dtype)
    for h in range(H_B):
        qh = q[h * LANES:(h + 1) * LANES]
        q_lat = _dot(wuk_ref[h], qh[0:NOPE_B].astype(BF16))
        q_rope = qh[NOPE_B:NOPE_B + ROPE_B] * cos2 + qh[NOPE_B + ROPE_B:] * sin2
        cols = slice(h * tm, (h + 1) * tm)
        q_ref[0:KV_RANK, cols] = (q_lat * MLA_SCALE).astype(q_ref.dtype)
        q_ref[KV_RANK:MLA_W, cols] = (q_rope * MLA_SCALE).astype(q_ref.dtype)
        q_ref[MLA_W:, cols] = pad
    ckv = ckv_ref[...]
    c = ckv[0:KV_RANK]
    cn = c * lax.rsqrt(jnp.mean(c * c, axis=0, keepdims=True) + EPS) * kvg_ref[...]
    kr = ckv[KV_RANK:KV_RANK + ROPE_B] * cos2 + ckv[KV_RANK + ROPE_B:KV_RANK + 2 * ROPE_B] * sin2
    rowst_ref[0:KV_RANK, :] = cn
    rowst_ref[KV_RANK:MLA_W, :] = kr
    rows_t = jnp.concatenate([cn, kr, jnp.zeros((2 * LANES - MLA_W, tm), F32)], axis=0)
    rows_ref[...] = rows_t.T


def _mla_prep_t(cq_t, ckv_t, cos2_t, sin2_t, q_norm_g, wuq_ext_t, kv_norm_g, wuk, *, tm):
    b, _, t = cq_t.shape
    nt = t // tm
    blk = lambda w: pl.BlockSpec((None, w, tm), lambda bi, i: (bi, 0, i))
    pos = pl.BlockSpec((ROPE_B, tm), lambda bi, i: (0, i))
    return pl.pallas_call(
        _mla_prep_t_kernel,
        grid=(b, nt),
        in_specs=[blk(Q_RANK), blk(2 * LANES), pos, pos, _full((Q_RANK, 1)), _full(wuq_ext_t.shape),
                  _full((KV_RANK, 1)), _full(wuk.shape)],
        out_specs=[pl.BlockSpec((None, 2 * LANES, H_B * tm), lambda bi, i: (bi * nt + i, 0, 0)),
                   pl.BlockSpec((None, tm, 2 * LANES), lambda bi, i: (bi, i, 0)),
                   blk(MLA_W)],
        out_shape=[jax.ShapeDtypeStruct((b * nt, 2 * LANES, H_B * tm), BF16),
                   jax.ShapeDtypeStruct((b, t, 2 * LANES), F32),
                   jax.ShapeDtypeStruct((b, MLA_W, t), F32)],
        compiler_params=_cparams(("parallel", "parallel")),
        name="mla_prep_t",
    )(cq_t, ckv_t, cos2_t, sin2_t, q_norm_g.reshape(Q_RANK, 1), wuq_ext_t, kv_norm_g.reshape(KV_RANK, 1), wuk)


def _softmax_update_t(s, m_ref, l_ref, cols):
    m_old = m_ref[:, cols]
    m_new = jnp.maximum(m_old, jnp.max(s, axis=0, keepdims=True))
    alpha = jnp.exp(m_old - m_new)
    p = jnp.exp(s - m_new)
    l_ref[:, cols] = alpha * l_ref[:, cols] + jnp.sum(p, axis=0, keepdims=True)
    m_ref[:, cols] = m_new
    return alpha, p


def _page_specs(block_tail, layer, pages_per_seq):
    n = len(block_tail)

    def spec(jj):
        return pl.BlockSpec((None, None) + block_tail,
                            lambda b, s, pt: (layer, pt[b * pages_per_seq + s * DECODE_PAGES_PER_STEP + jj])
                            + (0,) * n)
    return [spec(jj) for jj in range(DECODE_PAGES_PER_STEP)]


def _mla_decode_kernel(pt_ref, q_ref, row_ref, gb_ref, wuv_ref, *rest):
    pages = rest[:DECODE_PAGES_PER_STEP]
    o_ref, m_ref, l_ref, acc_ref = rest[DECODE_PAGES_PER_STEP:]
    step = pl.program_id(1)

    @pl.when(step == 0)
    def _():
        m_ref[...] = jnp.full_like(m_ref, NEG_INF)
        l_ref[...] = jnp.zeros_like(l_ref)
        acc_ref[...] = jnp.zeros_like(acc_ref)

    q = q_ref[...]
    ks = [pg[...].astype(BF16) for pg in pages]
    s = jnp.concatenate([_dot(q, kp) for kp in ks], axis=1)
    alpha, p = _softmax_update(s, m_ref, l_ref)
    pv = _dot_nt(p[:, 0:PAGE_SIZE].astype(BF16), ks[0][0:KV_RANK, :])
    for jj in range(1, DECODE_PAGES_PER_STEP):
        pv = pv + _dot_nt(p[:, jj * PAGE_SIZE:(jj + 1) * PAGE_SIZE].astype(BF16), ks[jj][0:KV_RANK, :])
    acc_ref[...] = alpha * acc_ref[...] + pv

    @pl.when(step == pl.num_programs(1) - 1)
    def _():
        row = row_ref[...]
        s_new = jnp.sum(q.astype(F32) * row, axis=-1, keepdims=True)
        alpha2, p_new = _softmax_update(s_new, m_ref, l_ref)
        acc = alpha2 * acc_ref[...] + p_new * row[:, 0:KV_RANK]
        _mla_out(acc / l_ref[...], wuv_ref, gb_ref[...], o_ref, 1)


def _mla_decode(page_table_flat, q, rows_new, gb, wuv, cache, layer, pages_per_seq):
    b = q.shape[0]
    steps = pages_per_seq // DECODE_PAGES_PER_STEP
    per_b = lambda shape: pl.BlockSpec((None,) + shape, lambda bi, s, pt: (bi,) + (0,) * len(shape))
    grid_spec = pltpu.PrefetchScalarGridSpec(
        num_scalar_prefetch=1,
        grid=(b, steps),
        in_specs=[per_b((H_B, MLA_W)), per_b((1, MLA_W)), per_b((1, D_B)),
                  pl.BlockSpec(wuv.shape, lambda bi, s, pt: (0, 0, 0))]
                 + _page_specs((MLA_W, PAGE_SIZE), layer, pages_per_seq),
        out_specs=per_b((1, D_B)),
        scratch_shapes=[pltpu.VMEM((H_B, 1), F32), pltpu.VMEM((H_B, 1), F32), pltpu.VMEM((H_B, KV_RANK), F32)],
    )
    return pl.pallas_call(
        _mla_decode_kernel,
        grid_spec=grid_spec,
        out_shape=jax.ShapeDtypeStruct((b, 1, D_B), BF16),
        compiler_params=_cparams(("parallel", "arbitrary")),
        name="mla_decode",
    )(page_table_flat, q, rows_new, gb, wuv, *([cache] * DECODE_PAGES_PER_STEP))


def _bias_kernel(rb_ref, dist_ref, o_ref):
    dist = dist_ref[...]
    n = jnp.maximum(dist, 0)
    max_exact = NUM_BUCKETS // 2
    n_safe = jnp.maximum(n, max_exact).astype(F32)
    large = max_exact + (jnp.log(n_safe / max_exact) / math.log(MAX_DISTANCE / max_exact)
                         * (NUM_BUCKETS - max_exact)).astype(jnp.int32)
    large = jnp.minimum(large, NUM_BUCKETS - 1)
    bucket = jnp.where(n < max_exact, n, large)
    for h in range(H_C):
        bias = jnp.zeros(dist.shape, F32)
        for kb in range(NUM_BUCKETS):
            bias = jnp.where(bucket == kb, rb_ref[kb * H_C + h], bias)
        o_ref[h] = jnp.where(dist >= 0, bias, NEG_INF)


def _bias_tiles(rel_bias, dist):
    g, r, c = dist.shape
    grid_spec = pltpu.PrefetchScalarGridSpec(
        num_scalar_prefetch=1,
        grid=(g,),
        in_specs=[pl.BlockSpec((None, r, c), lambda i, rb: (i, 0, 0))],
        out_specs=pl.BlockSpec((None, H_C, r, c), lambda i, rb: (i, 0, 0, 0)),
    )
    return pl.pallas_call(
        _bias_kernel,
        grid_spec=grid_spec,
        out_shape=jax.ShapeDtypeStruct((g, H_C, r, c), F32),
        compiler_params=_cparams(("arbitrary",)),
        name="rel_bias_tiles",
    )(rel_bias.reshape(-1), dist)


def _diff_lambda(lam_ref, lam_init):
    lam = lam_ref[...]
    e1 = jnp.exp(jnp.sum(lam[0:1] * lam[1:2], axis=-1, keepdims=True))
    e2 = jnp.exp(jnp.sum(lam[2:3] * lam[3:4], axis=-1, keepdims=True))
    return e1 - e2 + lam_init


def _diff_queries(qc):
    lane = lax.broadcasted_iota(jnp.int32, qc.shape, 1)
    qs = qc * DIFF_SCALE
    groups = []
    for h in range(H_C):
        for c in range(2):
            lo = h * DV_C + c * DC
            groups.append(jnp.where((lane >= lo) & (lane < lo + DC), qs, 0.0))
    return jnp.concatenate(groups, axis=0)


def _diff_out(acc, l, lam, lam_init, sg, gc, o_ref, rows):
    for h in range(H_C):
        hs = slice(h * DV_C, (h + 1) * DV_C)
        r1 = slice((2 * h) * rows, (2 * h + 1) * rows)
        r2 = slice((2 * h + 1) * rows, (2 * h + 2) * rows)
        o = acc[r1, hs] / l[r1] - lam * (acc[r2, hs] / l[r2])
        o = _rms(o, sg, SUBLN_EPS) * (1.0 - lam_init)
        o_ref[:, hs] = (o * _silu(gc[:, hs])).astype(o_ref.dtype)


def _prompt_attn_kernel(pi_ref, pj_ref,
                        q_ref, k_ref, ct_ref, mask_ref, gb_ref, wuvt_ref,
                        qc_ref, kc_ref, vt_ref, gc_ref, bias_ref, lam_ref, sg_ref,
                        ob_ref, oc_ref,
                        m_ref, l_ref, acc_ref, qbd_ref, md_ref, ld_ref, accd_ref, *, lam_init):
    step = pl.program_id(1)
    i = pi_ref[step]
    j = pj_ref[step]
    tq = ob_ref.shape[0]
    nblk = 2 * H_C

    @pl.when(j == 0)
    def _():
        m_ref[...] = jnp.full_like(m_ref, NEG_INF)
        l_ref[...] = jnp.zeros_like(l_ref)
        acc_ref[...] = jnp.zeros_like(acc_ref)
        q = qc_ref[...] * DIFF_SCALE
        feat = lax.broadcasted_iota(jnp.int32, q.shape, 0)
        for blk in range(nblk):
            lo = (blk // 2) * DV_C + (blk % 2) * DC
            qbd_ref[:, blk * tq:(blk + 1) * tq] = jnp.where((feat >= lo) & (feat < lo + DC), q, 0.0).astype(BF16)
        md_ref[...] = jnp.full_like(md_ref, NEG_INF)
        ld_ref[...] = jnp.zeros_like(ld_ref)
        accd_ref[...] = jnp.zeros_like(accd_ref)

    k = k_ref[...].astype(BF16)
    ct = ct_ref[...].astype(BF16)
    mask = mask_ref[jnp.minimum(i - j, 1)]
    heads = range(H_B)
    cols = [slice(h * tq, (h + 1) * tq) for h in heads]
    kc = kc_ref[...].astype(BF16)
    vt = vt_ref[...].astype(BF16)
    tile = jnp.minimum(i - j, 2)
    blocks = range(nblk)
    dcols = [slice(blk * tq, (blk + 1) * tq) for blk in blocks]
    s_b = [_dot(k, q_ref[:, cols[h]]) + mask for h in heads]
    s_d = [_dot(kc, qbd_ref[:, dcols[blk]]) + bias_ref[tile, blk // 2] for blk in blocks]
    ap_b = [_softmax_update_t(s_b[h], m_ref, l_ref, cols[h]) for h in heads]
    ap_d = [_softmax_update_t(s_d[blk], md_ref, ld_ref, dcols[blk]) for blk in blocks]
    pv_b = [_dot(ct, ap_b[h][1].astype(BF16)) for h in heads]
    pv_d = [_dot(vt[(blk // 2) * DV_C:(blk // 2 + 1) * DV_C], ap_d[blk][1].astype(BF16)) for blk in blocks]
    for h in heads:
        acc_ref[:, cols[h]] = ap_b[h][0] * acc_ref[:, cols[h]] + pv_b[h]
    for blk in blocks:
        accd_ref[blk] = ap_d[blk][0] * accd_ref[blk] + pv_d[blk]

    @pl.when(j == i)
    def _():
        outs = []
        for h in heads:
            o_lat = (acc_ref[:, cols[h]] / l_ref[:, cols[h]]).astype(BF16)
            outs.append(_dot(wuvt_ref[h], o_lat))
        o = jnp.concatenate(outs, axis=0).T
        ob_ref[...] = (o * _silu(gb_ref[...])).astype(ob_ref.dtype)
        lam = _diff_lambda(lam_ref, lam_init)
        outs = []
        for h in range(H_C):
            o = (accd_ref[2 * h] / ld_ref[:, dcols[2 * h]]
                 - lam * (accd_ref[2 * h + 1] / ld_ref[:, dcols[2 * h + 1]]))
            o = o * lax.rsqrt(jnp.mean(o * o, axis=0, keepdims=True) + SUBLN_EPS) * sg_ref[...]
            outs.append(o * (1.0 - lam_init))
        o = jnp.concatenate(outs, axis=0).T
        oc_ref[...] = (o * _silu(gc_ref[...])).astype(oc_ref.dtype)


def _prompt_attn(q_t, rows_pad, rows_t, mask, gb, wuv_t, qc_t, kc, vc_t, gc, bias_tiles, lam_vecs, subln_g, lam_init):
    b, t, _ = kc.shape
    tq = ATTN_TILE
    nq = t // tq
    pairs = [(i, j) for i in range(nq) for j in range(i + 1)]
    pair_i = jnp.asarray([p[0] for p in pairs], jnp.int32)
    pair_j = jnp.asarray([p[1] for p in pairs], jnp.int32)
    nblk = 2 * H_C
    q_tile = lambda w: pl.BlockSpec((None, tq, w), lambda bi, s, pi, pj: (bi, pi[s], 0))
    k_tile = lambda w: pl.BlockSpec((None, tq, w), lambda bi, s, pi, pj: (bi, pj[s], 0))
    qt_tile = lambda w: pl.BlockSpec((None, w, tq), lambda bi, s, pi, pj: (bi, 0, pi[s]))
    kt_tile = lambda w: pl.BlockSpec((None, w, tq), lambda bi, s, pi, pj: (bi, 0, pj[s]))
    const = lambda shape: pl.BlockSpec(shape, lambda bi, s, pi, pj: (0,) * len(shape))
    grid_spec = pltpu.PrefetchScalarGridSpec(
        num_scalar_prefetch=2,
        grid=(b, len(pairs)),
        in_specs=[pl.BlockSpec((None, 2 * LANES, H_B * tq), lambda bi, s, pi, pj: (bi * nq + pi[s], 0, 0)),
                  k_tile(2 * LANES), kt_tile(KV_RANK), const(mask.shape), q_tile(D_B), const(wuv_t.shape),
                  qt_tile(D_C), k_tile(D_C), kt_tile(D_C), q_tile(D_C), const(bias_tiles.shape),
                  const(lam_vecs.shape), const((DV_C, 1))],
        out_specs=[q_tile(D_B), q_tile(D_C)],
        scratch_shapes=[pltpu.VMEM((1, H_B * tq), F32), pltpu.VMEM((1, H_B * tq), F32),
                        pltpu.VMEM((KV_RANK, H_B * tq), F32),
                        pltpu.VMEM((D_C, nblk * tq), BF16), pltpu.VMEM((1, nblk * tq), F32),
                        pltpu.VMEM((1, nblk * tq), F32), pltpu.VMEM((nblk, DV_C, tq), F32)],
    )
    return pl.pallas_call(
        functools.partial(_prompt_attn_kernel, lam_init=lam_init),
        grid_spec=grid_spec,
        out_shape=[jax.ShapeDtypeStruct((b, t, D_B), BF16), jax.ShapeDtypeStruct((b, t, D_C), BF16)],
        compiler_params=_cparams(("parallel", "arbitrary")),
        name="prompt_attn",
    )(pair_i, pair_j, q_t, rows_pad, rows_t, mask, gb, wuv_t, qc_t, kc, vc_t, gc, bias_tiles, lam_vecs,
      subln_g.reshape(DV_C, 1))


def _diff_decode_kernel(pt_ref, q_ref, kn_ref, vn_ref, gc_ref, bias_ref, lam_ref, sg_ref, *rest, lam_init):
    n = DECODE_PAGES_PER_STEP
    kpages, vpages = rest[:n], rest[n:2 * n]
    o_ref, m_ref, l_ref, acc_ref = rest[2 * n:]
    step = pl.program_id(1)
    last = pl.num_programs(1) - 1

    @pl.when(step == 0)
    def _():
        m_ref[...] = jnp.full_like(m_ref, NEG_INF)
        l_ref[...] = jnp.zeros_like(l_ref)
        acc_ref[...] = jnp.zeros_like(acc_ref)

    qbd = _diff_queries(q_ref[...])
    qb = qbd.astype(BF16)
    s = jnp.concatenate([_dot(qb, kp[...].astype(BF16)) for kp in kpages], axis=1)
    s = s + bias_ref[jnp.where(step == last, 1, 0)]
    alpha, p = _softmax_update(s, m_ref, l_ref)
    pv = _dot_nt(p[:, 0:PAGE_SIZE].astype(BF16), vpages[0][...].astype(BF16))
    for jj in range(1, n):
        pv = pv + _dot_nt(p[:, jj * PAGE_SIZE:(jj + 1) * PAGE_SIZE].astype(BF16), vpages[jj][...].astype(BF16))
    acc_ref[...] = alpha * acc_ref[...] + pv

    @pl.when(step == last)
    def _():
        s_new = jnp.sum(qbd * kn_ref[...], axis=-1, keepdims=True) + bias_ref[2][:, 0:1]
        alpha2, p_new = _softmax_update(s_new, m_ref, l_ref)
        acc = alpha2 * acc_ref[...] + p_new * vn_ref[...]
        _diff_out(acc, l_ref[...], _diff_lambda(lam_ref, lam_init), lam_init, sg_ref[...], gc_ref[...], o_ref, 1)


def _diff_decode(page_table_flat, qc, kc, vc, gc, bias_rows, lam_vecs, subln_g, cache_k, cache_v, layer,
                 pages_per_seq, lam_init):
    b = qc.shape[0]
    steps = pages_per_seq // DECODE_PAGES_PER_STEP
    per_b = pl.BlockSpec((None, 1, D_C), lambda bi, s, pt: (bi, 0, 0))
    const = lambda shape: pl.BlockSpec(shape, lambda bi, s, pt: (0,) * len(shape))
    grid_spec = pltpu.PrefetchScalarGridSpec(
        num_scalar_prefetch=1,
        grid=(b, steps),
        in_specs=[per_b, per_b, per_b, per_b, const(bias_rows.shape), const(lam_vecs.shape), const((1, DV_C))]
                 + _page_specs((D_C, PAGE_SIZE), layer, pages_per_seq)
                 + _page_specs((D_C, PAGE_SIZE), layer, pages_per_seq),
        out_specs=per_b,
        scratch_shapes=[pltpu.VMEM((2 * H_C, 1), F32), pltpu.VMEM((2 * H_C, 1), F32),
                        pltpu.VMEM((2 * H_C, D_C), F32)],
    )
    return pl.pallas_call(
        functools.partial(_diff_decode_kernel, lam_init=lam_init),
        grid_spec=grid_spec,
        out_shape=jax.ShapeDtypeStruct((b, 1, D_C), BF16),
        compiler_params=_cparams(("parallel", "arbitrary")),
        name="diff_decode",
    )(page_table_flat, qc, kc, vc, gc, bias_rows, lam_vecs, subln_g.reshape(1, DV_C),
      *([cache_k] * DECODE_PAGES_PER_STEP), *([cache_v] * DECODE_PAGES_PER_STEP))


def _permute_w_in(w):
    o_ckv = A_IN + D_A + Q_RANK
    o_kr = o_ckv + KV_RANK
    o_gb = o_kr + ROPE_B
    half = ROPE_B // 2
    pad = jnp.zeros((w.shape[0], 2 * LANES - KV_RANK - 2 * ROPE_B), w.dtype)
    out = jnp.concatenate([w[:, :o_gb], w[:, o_kr + half:o_gb], w[:, o_kr:o_kr + half], pad, w[:, o_gb:]], axis=1)
    assert out.shape[1] == IN_COLS_PERM
    return out.astype(BF16)


def _extend_w_uq(w):
    w = w.reshape(Q_RANK, H_B, NOPE_B + ROPE_B)
    half = ROPE_B // 2
    rope = w[:, :, NOPE_B:]
    swapped = jnp.concatenate([rope[:, :, half:], rope[:, :, :half]], axis=-1)
    return jnp.concatenate([w, swapped], axis=-1).reshape(Q_RANK, H_B * LANES).astype(BF16)


def _rope_tables(pos):
    inv = ROPE_THETA ** (-jnp.arange(0, ROPE_B, 2, dtype=F32) / ROPE_B)
    ang = pos.astype(F32)[:, None] * inv[None, :]
    cos, sin = jnp.cos(ang), jnp.sin(ang)
    return jnp.concatenate([cos, cos], axis=-1), jnp.concatenate([-sin, sin], axis=-1)


def _layer_weights(l, W):
    row = lambda a: a.reshape(1, -1)
    rwkv = (row(W["mu_shift"][l]), row(W["rw_w0"][l]), W["rw_w2"][l], row(W["rw_a0"][l]), W["rw_a2"][l],
            row(W["rw_k_k"][l]), row(W["rw_k_a"][l]), row(W["rw_r_k"][l]), row(W["rw_gn_g"][l]),
            row(W["rw_gn_b"][l]))
    w_in = _permute_w_in(W["w_in"][l])
    wuq = _extend_w_uq(W["mla_w_uq"][l])
    return dict(
        w_in=w_in,
        w_in_n=jnp.concatenate([w_in[:, a:b] for a, b in PROMPT_SEGS_N], axis=1),
        w_in_t=jnp.concatenate([w_in[:, a:b] for a, b in PROMPT_SEGS_T], axis=1).T,
        rwkv=rwkv,
        wuq=wuq,
        wuq_t=wuq.T,
        wuk=jnp.transpose(W["mla_w_uk"][l], (1, 0, 2)).astype(BF16),
        wuv_t=jnp.transpose(W["mla_w_uv"][l], (1, 2, 0)).astype(BF16),
        wuk_t=jnp.transpose(W["mla_w_uk"][l], (1, 2, 0)).astype(BF16),
        wuv=jnp.transpose(W["mla_w_uv"][l], (1, 0, 2)).astype(BF16),
        lam_vecs=jnp.stack([W["diff_lam_q1"][l], W["diff_lam_k1"][l], W["diff_lam_q2"][l], W["diff_lam_k2"][l]]),
        wo=W["w_out"][l].astype(BF16),
        wple=W["w_ple"][l].astype(BF16),
        wg=W["w_ple_gate"][l].astype(BF16),
    )


def _run_prompt(x, p, W, LW, depth):
    b, t, d = x.shape
    m = b * t
    tm = ATTN_TILE
    cos2, sin2 = _rope_tables(jnp.arange(t, dtype=jnp.int32))
    cos2_t, sin2_t = cos2.T, sin2.T
    tile = jnp.arange(ATTN_TILE, dtype=jnp.int32)
    dist = (jnp.arange(3, dtype=jnp.int32)[:, None, None] * ATTN_TILE + tile[None, None, :] - tile[None, :, None])
    bias_tiles = _bias_tiles(W["rel_bias"], dist)
    causal = jnp.stack([jnp.where(dist[0] >= 0, 0.0, NEG_INF).astype(F32), jnp.zeros(dist.shape[1:], F32)])
    uprev0 = jnp.zeros((b, 1, A_IN), F32)
    s0 = jnp.zeros((b, H_A, HEAD_A, HEAD_A), F32)
    segs_n = _pack_segments(PROMPT_SEGS_N)
    segs_t = _pack_segments(PROMPT_SEGS_T)
    h = x.reshape(m, d)
    mla_rows, k_rows, v_rows, wkv_out, shift_out = [], [], [], [], []
    for l in range(depth):
        lw = LW[l]
        u, ga, gb, gc, kc, cq_t, ckv_t, qc_t, kc_t, vc_t, xn_last = _inproj(
            h, W["norm_g"][l], lw["w_in_n"], segs_n, normalize=True, rows_per_seq=t, tm=tm,
            wt_bf16=lw["w_in_t"], segs_t=segs_t)
        y_a, s_new = _rwkv_chunked(u.reshape(b, t, A_IN), uprev0, ga.reshape(b, t, D_A), s0, lw["rwkv"])
        q_t, rows_pad, rows_t = _mla_prep_t(cq_t, ckv_t, cos2_t, sin2_t, W["mla_q_norm_g"][l], lw["wuq_t"],
                                            W["mla_kv_norm_g"][l], lw["wuk"], tm=tm)
        lam_init = 0.8 - 0.6 * math.exp(-0.3 * l)
        y_b, y_c = _prompt_attn(q_t, rows_pad, rows_t, causal, gb.reshape(b, t, D_B), lw["wuv_t"],
                                qc_t, kc.reshape(b, t, D_C), vc_t, gc.reshape(b, t, D_C), bias_tiles,
                                lw["lam_vecs"], W["diff_subln_g"][l], lam_init)
        h = _outproj(h, y_a.reshape(m, D_A), y_b.reshape(m, D_B), y_c.reshape(m, D_C), p[l].reshape(m, PLE_DIM),
                     lw["wo"], lw["wple"], lw["wg"], W["final_norm_g"], final=(l == depth - 1), tm=tm)
        mla_rows.append(jnp.transpose(rows_t, (0, 2, 1)))
        k_rows.append(jnp.transpose(kc_t.reshape(b, H_C, 2 * DC, t), (0, 3, 1, 2)))
        v_rows.append(jnp.transpose(vc_t.reshape(b, H_C, DV_C, t), (0, 3, 1, 2)))
        wkv_out.append(s_new)
        shift_out.append(xn_last.reshape(b, d))
    return (h.reshape(b, t, d), jnp.stack(mla_rows), jnp.stack(k_rows), jnp.stack(v_rows), jnp.stack(wkv_out),
            jnp.stack(shift_out))


def _run_sample(x, p, state_shift, state_wkv, cache_mla, cache_k, cache_v, page_table, W, LW, depth):
    b, t, d = x.shape
    assert t == 1
    pages_per_seq = page_table.shape[1]
    past_len = pages_per_seq * PAGE_SIZE
    tm = b
    cos2, sin2 = _rope_tables(jnp.full((b,), past_len, dtype=jnp.int32))
    pt_flat = page_table.reshape(-1).astype(jnp.int32)
    step_keys = DECODE_PAGES_PER_STEP * PAGE_SIZE
    key_in_step = jnp.arange(step_keys, dtype=jnp.int32)
    dist = jnp.stack([past_len - key_in_step,
                      past_len - (past_len - step_keys + key_in_step),
                      jnp.zeros((step_keys,), jnp.int32)])
    dist = jnp.broadcast_to(dist[:, None, :], (3, 2, step_keys))
    bias = _bias_tiles(W["rel_bias"], dist)
    bias_rows = bias.reshape(3, 2 * H_C, step_keys)
    to_feature_major = lambda c: jnp.transpose(c, (0, 1, 3, 4, 2)).reshape(c.shape[:2] + (D_C, PAGE_SIZE))
    cache_k2 = to_feature_major(cache_k)
    cache_v2 = to_feature_major(cache_v)
    cache_mla_t = jnp.transpose(cache_mla, (0, 1, 3, 2))
    h = x.reshape(b, d)
    mla_rows, k_rows, v_rows, wkv_out, shift_out = [], [], [], [], []
    for l in range(depth):
        lw = LW[l]
        u, ga, cq, ckv, gb, qc, kc, vc, gc, xn = _inproj(
            h, W["norm_g"][l], lw["w_in"], ALL_SEGS, normalize=True, rows_per_seq=1, tm=tm)
        (uprev,) = _inproj(state_shift[l], W["norm_g"][l], lw["w_in"][:, :A_IN], (SEG_U,), normalize=False,
                           rows_per_seq=1, tm=tm)
        y_a, s_new = _rwkv_step(u, uprev, ga, state_wkv[l], lw["rwkv"])
        q, rows = _mla_prep(cq, ckv, cos2, sin2, W["mla_q_norm_g"][l], lw["wuq"], W["mla_kv_norm_g"][l],
                            lw["wuk_t"], tm=tm, pos_tiles=1)
        y_b = _mla_decode(pt_flat, jnp.transpose(q, (1, 0, 2)), rows.reshape(b, 1, MLA_W), gb.reshape(b, 1, D_B),
                          lw["wuv"], cache_mla_t, l, pages_per_seq)
        lam_init = 0.8 - 0.6 * math.exp(-0.3 * l)
        y_c = _diff_decode(pt_flat, qc.reshape(b, 1, D_C), kc.reshape(b, 1, D_C), vc.reshape(b, 1, D_C),
                           gc.reshape(b, 1, D_C), bias_rows, lw["lam_vecs"], W["diff_subln_g"][l], cache_k2,
                           cache_v2, l, pages_per_seq, lam_init)
        h = _outproj(h, y_a, y_b.reshape(b, D_B), y_c.reshape(b, D_C), p[l].reshape(b, PLE_DIM),
                     lw["wo"], lw["wple"], lw["wg"], W["final_norm_g"], final=(l == depth - 1), tm=tm)
        mla_rows.append(rows.reshape(b, 1, MLA_W))
        k_rows.append(kc.reshape(b, 1, H_C, 2 * DC))
        v_rows.append(vc.reshape(b, 1, H_C, DV_C))
        wkv_out.append(s_new)
        shift_out.append(xn)
    return (h.reshape(b, 1, d), jnp.stack(mla_rows), jnp.stack(k_rows), jnp.stack(v_rows), jnp.stack(wkv_out),
            jnp.stack(shift_out))


def kernel(x_prompt, x_sample, cache_mla, cache_diff_k, cache_diff_v, state_wkv, state_shift, page_table,
           p_prompt, p_sample, norm_g, w_in, mu_shift, rw_w0, rw_w2, rw_a0, rw_a2, rw_k_k, rw_k_a, rw_r_k,
           rw_gn_g, rw_gn_b, mla_q_norm_g, mla_w_uq, mla_kv_norm_g, mla_w_uk, mla_w_uv, diff_lam_q1,
           diff_lam_k1, diff_lam_q2, diff_lam_k2, diff_subln_g, rel_bias, w_out, w_ple, w_ple_gate,
           final_norm_g):
    W = {"norm_g": norm_g, "w_in": w_in, "mu_shift": mu_shift, "rw_w0": rw_w0, "rw_w2": rw_w2, "rw_a0": rw_a0,
         "rw_a2": rw_a2, "rw_k_k": rw_k_k, "rw_k_a": rw_k_a, "rw_r_k": rw_r_k, "rw_gn_g": rw_gn_g,
         "rw_gn_b": rw_gn_b, "mla_q_norm_g": mla_q_norm_g, "mla_w_uq": mla_w_uq, "mla_kv_norm_g": mla_kv_norm_g,
         "mla_w_uk": mla_w_uk, "mla_w_uv": mla_w_uv, "diff_lam_q1": diff_lam_q1, "diff_lam_k1": diff_lam_k1,
         "diff_lam_q2": diff_lam_q2, "diff_lam_k2": diff_lam_k2, "diff_subln_g": diff_subln_g,
         "rel_bias": rel_bias, "w_out": w_out, "w_ple": w_ple, "w_ple_gate": w_ple_gate,
         "final_norm_g": final_norm_g}
    depth = w_in.shape[0]
    LW = [_layer_weights(l, W) for l in range(depth)]
    y_p, mla_p, dk_p, dv_p, wkv_p, sh_p = _run_prompt(x_prompt, p_prompt, W, LW, depth)
    y_s, mla_s, dk_s, dv_s, wkv_s, sh_s = _run_sample(x_sample, p_sample, state_shift, state_wkv, cache_mla,
                                                      cache_diff_k, cache_diff_v, page_table, W, LW, depth)
    return (y_p, y_s, mla_p, mla_s, dk_p, dk_s, dv_p, dv_s, wkv_p, wkv_s, sh_p, sh_s)
```

```python
import functools
import math

import jax
import jax.numpy as jnp
from jax import lax
from jax.experimental import pallas as pl
from jax.experimental.pallas import tpu as pltpu

F32 = jnp.float32
BF16 = jnp.bfloat16
HIGHEST = lax.Precision.HIGHEST

LANES = 128
SUBLANES = 8
VMEM_LIMIT_BYTES = 56 * 1024 * 1024

D_MODEL = 1024
HEAD_A = 64
D_A = 512
H_A = D_A // HEAD_A
W_LORA = 64
A_LORA = 64
A_IN = 3 * D_A + W_LORA + A_LORA
D_B = 256
DV_B = 64
H_B = D_B // DV_B
NOPE_B = 64
ROPE_B = 32
Q_RANK = 256
KV_RANK = 128
MLA_W = KV_RANK + ROPE_B
ROPE_THETA = 10000.0
D_C = 256
DV_C = 64
H_C = D_C // DV_C
DC = DV_C // 2
NUM_BUCKETS = 32
MAX_DISTANCE = 128
PLE_DIM = 256
PAGE_SIZE = 128
NEG_INF = -1e30
EPS = 1e-6
GN_EPS = 64e-5
SUBLN_EPS = 1e-5
MLA_SCALE = (NOPE_B + ROPE_B) ** -0.5
DIFF_SCALE = DC ** -0.5

SEG_U = (0, A_IN)
SEG_GA = (A_IN, A_IN + D_A)
SEG_CQ = (SEG_GA[1], SEG_GA[1] + Q_RANK)
SEG_CKV = (SEG_CQ[1], SEG_CQ[1] + 2 * LANES)
SEG_GB = (SEG_CKV[1], SEG_CKV[1] + D_B)
SEG_QC = (SEG_GB[1], SEG_GB[1] + D_C)
SEG_KC = (SEG_QC[1], SEG_QC[1] + D_C)
SEG_VC = (SEG_KC[1], SEG_KC[1] + D_C)
SEG_GC = (SEG_VC[1], SEG_VC[1] + D_C)
ALL_SEGS = (SEG_U, SEG_GA, SEG_CQ, SEG_CKV, SEG_GB, SEG_QC, SEG_KC, SEG_VC, SEG_GC)
IN_COLS_PERM = SEG_GC[1]


def _pack_segments(segs):
    out, pos = [], 0
    for a, b in segs:
        out.append((pos, pos + b - a))
        pos += b - a
    return tuple(out)


PROMPT_SEGS_N = (SEG_U, SEG_GA, SEG_GB, SEG_GC, SEG_KC)
PROMPT_SEGS_T = (SEG_CQ, SEG_CKV, SEG_QC, SEG_KC, SEG_VC)

RWKV_CHUNK = 64
RWKV_BLOCK = 128
ATTN_TILE = 256
DECODE_PAGES_PER_CHUNK = 16


def _cparams(semantics):
    return pltpu.CompilerParams(dimension_semantics=semantics, vmem_limit_bytes=VMEM_LIMIT_BYTES)


def _full(shape):
    n = len(shape)
    return pl.BlockSpec(shape, lambda *_: (0,) * n)


def _sigmoid(x):
    return 1.0 / (1.0 + jnp.exp(-x))


def _silu(x):
    return x * _sigmoid(x)


def _rms(x, g, eps):
    return x * lax.rsqrt(jnp.mean(x * x, axis=-1, keepdims=True) + eps) * g


def _dot(a, b, **kw):
    return jnp.dot(a, b, preferred_element_type=F32, **kw)


def _dot_nt(a, b, **kw):
    return lax.dot_general(a, b, (((1,), (1,)), ((), ())), preferred_element_type=F32, **kw)


def _dot_tn(a, b, **kw):
    return lax.dot_general(a, b, (((0,), (0,)), ((), ())), preferred_element_type=F32, **kw)


def _split3(x):
    hi = x.astype(BF16)
    rest = x - hi.astype(F32)
    mid = rest.astype(BF16)
    lo = (rest - mid.astype(F32)).astype(BF16)
    return hi, mid, lo


def _dot_split_rhs(a_exact, b):
    hi, mid, lo = _split3(b)
    return _dot(a_exact, hi) + (_dot(a_exact, mid) + _dot(a_exact, lo))


def _dot_split_lhs(a, b_exact):
    hi, mid, lo = _split3(a)
    return _dot(hi, b_exact) + (_dot(mid, b_exact) + _dot(lo, b_exact))


def _mm(a, b):
    return _dot(a.astype(BF16), b.astype(BF16))


def _mm_nt(a, b):
    return _dot_nt(a.astype(BF16), b.astype(BF16))


def _mm_tn(a, b):
    return _dot_tn(a.astype(BF16), b.astype(BF16))


def _inproj_kernel(h_ref, g_ref, w_ref, wt_ref, *out_refs, normalize, segs, segs_t, emit_xn):
    x = h_ref[...]
    xn = _rms(x, g_ref[...], EPS) if normalize else x
    xb = xn.astype(BF16)
    for o_ref, (a, b) in zip(out_refs, segs):
        o_ref[...] = _dot(xb, w_ref[:, a:b])
    for o_ref, (a, b) in zip(out_refs[len(segs):], segs_t):
        o_ref[...] = _dot_nt(wt_ref[a:b, :], xb)
    n_proj = len(segs) + len(segs_t)
    if emit_xn == "last_row":
        rows = x.shape[0]
        out_refs[n_proj][...] = xn[rows - 1:rows, :]
    elif emit_xn == "all":
        out_refs[n_proj][...] = xn


def _inproj(h2d, norm_g, w_bf16, segs, *, normalize, rows_per_seq, tm, wt_bf16=None, segs_t=()):
    m, d = h2d.shape
    assert m % tm == 0 and rows_per_seq % tm == 0 or rows_per_seq == 1
    tiles_per_seq = max(rows_per_seq // tm, 1)
    if wt_bf16 is None:
        wt_bf16 = jnp.zeros((SUBLANES, d), BF16)
    out_shapes = [jax.ShapeDtypeStruct((m, b - a), F32) for a, b in segs]
    out_specs = [pl.BlockSpec((tm, b - a), lambda i: (i, 0)) for a, b in segs]
    for a, b in segs_t:
        out_shapes.append(jax.ShapeDtypeStruct((m // rows_per_seq, b - a, rows_per_seq), F32))
        out_specs.append(pl.BlockSpec((None, b - a, tm), lambda i: (i // tiles_per_seq, 0, i % tiles_per_seq)))
    emit_xn = None
    if normalize:
        if rows_per_seq == 1:
            emit_xn = "all"
            out_shapes.append(jax.ShapeDtypeStruct((m, d), F32))
            out_specs.append(pl.BlockSpec((tm, d), lambda i: (i, 0)))
        else:
            emit_xn = "last_row"
            out_shapes.append(jax.ShapeDtypeStruct((m // rows_per_seq, 1, d), F32))
            out_specs.append(pl.BlockSpec((None, 1, d), lambda i: (i // tiles_per_seq, 0, 0)))
    kern = functools.partial(_inproj_kernel, normalize=normalize, segs=segs, segs_t=segs_t, emit_xn=emit_xn)
    return pl.pallas_call(
        kern,
        grid=(m // tm,),
        in_specs=[pl.BlockSpec((tm, d), lambda i: (i, 0)), _full((1, d)), _full(w_bf16.shape), _full(wt_bf16.shape)],
        out_specs=out_specs,
        out_shape=out_shapes,
        compiler_params=_cparams(("arbitrary",)),
        name="inproj",
    )(h2d, norm_g.reshape(1, d), w_bf16, wt_bf16)


def _outproj_kernel(h_ref, ya_ref, yb_ref, yc_ref, p_ref, wo_ref, wple_ref, wg_ref, fng_ref, o_ref, *, final):
    mixed = (_dot(ya_ref[...], wo_ref[0:D_A, :])
             + _dot(yb_ref[...], wo_ref[D_A:D_A + D_B, :])
             + _dot(yc_ref[...], wo_ref[D_A + D_B:, :]))
    h2 = h_ref[...] + mixed
    ple = _dot(p_ref[...].astype(BF16), wple_ref[...])
    gate = _sigmoid(_dot(h2.astype(BF16), wg_ref[...]))
    h3 = h2 + ple * gate
    o_ref[...] = _rms(h3, fng_ref[...], EPS) if final else h3


def _outproj(h2d, ya, yb, yc, p2d, wo, wple, wg, final_g, *, final, tm):
    m, d = h2d.shape
    row = lambda w: pl.BlockSpec((tm, w), lambda i: (i, 0))
    return pl.pallas_call(
        functools.partial(_outproj_kernel, final=final),
        grid=(m // tm,),
        in_specs=[row(d), row(D_A), row(D_B), row(D_C), row(PLE_DIM),
                  _full(wo.shape), _full(wple.shape), _full(wg.shape), _full((1, d))],
        out_specs=row(d),
        out_shape=jax.ShapeDtypeStruct((m, d), F32),
        compiler_params=_cparams(("arbitrary",)),
        name="outproj",
    )(h2d, ya, yb, yc, p2d, wo, wple, wg, final_g.reshape(1, d))


def _rwkv_prep(um, w0, w2, a0, a2, k_k, k_a):
    r = um[:, 0:D_A]
    k = um[:, D_A:2 * D_A]
    v = um[:, 2 * D_A:3 * D_A]
    w_lo = um[:, 3 * D_A:3 * D_A + W_LORA]
    a_lo = um[:, 3 * D_A + W_LORA:A_IN]
    wl = w0 + _mm(jnp.tanh(w_lo), w2)
    neg = -wl
    softplus = jnp.maximum(neg, 0.0) + jnp.log(1.0 + jnp.exp(-jnp.abs(neg)))
    w = -softplus - 0.5
    log_decay = -jnp.exp(w)
    a = _sigmoid(a0 + _mm(a_lo, a2))
    kk = k * k_k
    k = k * (1.0 + (a - 1.0) * k_a)
    return r, k, v, kk, a, log_decay


def _normalize_kk(kk_h):
    norm = jnp.sqrt(jnp.sum(kk_h * kk_h, axis=-1, keepdims=True))
    return kk_h / jnp.maximum(norm, 1e-12)


def _rwkv_head_out(y, r_h, k_h, v_h, rk_h, gng_h, gnb_h, gate_h):
    mu = jnp.mean(y, axis=-1, keepdims=True)
    var = jnp.mean(jnp.square(y - mu), axis=-1, keepdims=True)
    yn = (y - mu) * lax.rsqrt(var + GN_EPS) * gng_h + gnb_h
    bonus = jnp.sum(r_h * k_h * rk_h, axis=-1, keepdims=True) * v_h
    return (yn + bonus) * _silu(gate_h)


def _rwkv_chunk_kernel(u_ref, uprev0_ref, ga_ref, s0_ref, ones_ref, mu_ref, w0_ref, w2_ref, a0_ref, a2_ref,
                       kk_ref, ka_ref, rk_ref, gng_ref, gnb_ref, y_ref, s_ref, prev_ref):
    step = pl.program_id(1)
    R = u_ref.shape[0]
    C = RWKV_CHUNK
    subs = range(R // C)

    @pl.when(step == 0)
    def _():
        prev_ref[...] = uprev0_ref[...]
        s_ref[...] = s0_ref[...]

    u = u_ref[...]
    row = lax.broadcasted_iota(jnp.int32, (R, 1), 0)
    u_prev = jnp.where(row == 0, prev_ref[...], pltpu.roll(u, 1, axis=0))
    prev_ref[...] = u[R - 1:R, :]
    um = u + mu_ref[...] * (u_prev - u)
    r, k, v, kk, a, log_decay = _rwkv_prep(um, w0_ref[...], w2_ref[...], a0_ref[...], a2_ref[...],
                                           kk_ref[...], ka_ref[...])

    ri = lax.broadcasted_iota(jnp.int32, (R, R), 0)
    rj = lax.broadcasted_iota(jnp.int32, (R, R), 1)
    same_chunk = (ri // C) == (rj // C)
    cs = _dot_split_rhs((same_chunk & (rj <= ri)).astype(BF16), log_decay)
    cs_last = [cs[(sb + 1) * C - 1:(sb + 1) * C, :] for sb in subs]
    cs_end = jnp.concatenate([jnp.broadcast_to(x, (C, x.shape[1])) for x in cs_last], axis=0)
    p_end = [jnp.exp(x) for x in cs_last]
    e_inv = jnp.exp(-cs)
    e_rem = jnp.exp(cs_end - cs)
    kk_n = kk / jnp.maximum(jnp.sqrt(_dot_split_lhs(kk * kk, ones_ref[...])), 1e-12)
    b_f = kk_n * a
    a_t = (-kk_n * jnp.exp(cs - log_decay)).astype(BF16)
    r_t = (r * jnp.exp(cs)).astype(BF16)
    b_t = (b_f * e_inv).astype(BF16)
    k_t = (k * e_inv).astype(BF16)
    b_end = (b_f * e_rem).astype(BF16)
    k_end = (k * e_rem).astype(BF16)
    v_b = v.astype(BF16)
    ga = ga_ref[...]

    t2 = lax.broadcasted_iota(jnp.int32, (C, 2 * C), 0)
    j2 = lax.broadcasted_iota(jnp.int32, (C, 2 * C), 1)
    j2 = jnp.where(j2 >= C, j2 - C, j2)
    strict2 = j2 < t2
    incl2 = j2 <= t2
    ti = lax.broadcasted_iota(jnp.int32, (C, C), 0)
    tj = lax.broadcasted_iota(jnp.int32, (C, C), 1)
    eye = (ti == tj).astype(F32)
    zeros = jnp.zeros((C, HEAD_A), BF16)

    units = [(sb, h) for sb in subs for h in range(H_A)]
    blk = lambda x, sb, h: x[sb * C:(sb + 1) * C, h * HEAD_A:(h + 1) * HEAD_A]
    a_h = {un: blk(a_t, *un) for un in units}
    r_h = {un: blk(r_t, *un) for un in units}
    v_h = {un: blk(v_b, *un) for un in units}
    gram = {un: _dot_nt(jnp.concatenate([a_h[un], r_h[un]], axis=0),
                        jnp.concatenate([blk(b_t, *un), blk(k_t, *un)], axis=0)) for un in units}
    l_top = {un: jnp.where(strict2, gram[un][0:C], 0.0) for un in units}
    m_bot = {un: jnp.where(incl2, gram[un][C:], 0.0).astype(BF16) for un in units}
    lv = {un: _dot(l_top[un].astype(BF16), jnp.concatenate([zeros, v_h[un]], axis=0)) for un in units}

    l_ab = {un: l_top[un][:, 0:C] for un in units}
    inv = {un: eye + l_ab[un] for un in units}
    pw = {un: l_ab[un].astype(BF16) for un in units}
    for _ in range(int(math.log2(C)) - 1):
        pw = {un: _mm(pw[un], pw[un]).astype(BF16) for un in units}
        inv = {un: inv[un] + _mm(inv[un], pw[un]) for un in units}
    inv = {un: inv[un].astype(BF16) for un in units}
    w_mat = {un: _mm(inv[un], a_h[un]).astype(BF16) for un in units}
    u_v = {un: _mm(inv[un], lv[un]) for un in units}

    state = [s_ref[h] for h in range(H_A)]
    for sb in subs:
        heads = [(sb, h) for h in range(H_A)]
        state_b = [x.astype(BF16) for x in state]
        uv = [jnp.concatenate([(_dot_nt(w_mat[un], state_b[un[1]]) + u_v[un]).astype(BF16), v_h[un]], axis=0)
              for un in heads]
        y = [_dot_nt(r_h[un], state_b[un[1]]) + _dot(m_bot[un], uv[un[1]]) for un in heads]
        state = [state[h] * p_end[sb][:, h * HEAD_A:(h + 1) * HEAD_A]
                 + _dot_tn(uv[h], jnp.concatenate([blk(b_end, sb, h), blk(k_end, sb, h)], axis=0))
                 for h in range(H_A)]
        rows = slice(sb * C, (sb + 1) * C)
        for h in range(H_A):
            hs = slice(h * HEAD_A, (h + 1) * HEAD_A)
            out = _rwkv_head_out(y[h], r[rows, hs], k[rows, hs], v[rows, hs], rk_ref[:, hs], gng_ref[:, hs],
                                 gnb_ref[:, hs], ga[rows, hs])
            y_ref[rows, hs] = out.astype(y_ref.dtype)
    for h in range(H_A):
        s_ref[h] = state[h]


def _rwkv_chunked(u, uprev0, ga, s0, params):
    b, t, _ = u.shape
    c = RWKV_BLOCK
    assert t % c == 0
    head_of_lane = jnp.arange(D_A, dtype=jnp.int32) // HEAD_A
    head_ones = (head_of_lane[:, None] == head_of_lane[None, :]).astype(BF16)
    tok = lambda w: pl.BlockSpec((None, c, w), lambda i, j: (i, j, 0))
    state = pl.BlockSpec((None, H_A, HEAD_A, HEAD_A), lambda i, j: (i, 0, 0, 0))
    return pl.pallas_call(
        _rwkv_chunk_kernel,
        grid=(b, t // c),
        in_specs=[tok(A_IN), pl.BlockSpec((None, 1, A_IN), lambda i, j: (i, 0, 0)), tok(D_A), state,
                  _full(head_ones.shape)] + [_full(p.shape) for p in params],
        out_specs=[tok(D_A), state],
        out_shape=[jax.ShapeDtypeStruct((b, t, D_A), BF16), jax.ShapeDtypeStruct(s0.shape, F32)],
        scratch_shapes=[pltpu.VMEM((1, A_IN), F32)],
        compiler_params=_cparams(("parallel", "arbitrary")),
        name="rwkv_chunked",
    )(u, uprev0, ga, s0, head_ones, *params)


def _rwkv_step_kernel(u_ref, uprev_ref, ga_ref, s0_ref, mu_ref, w0_ref, w2_ref, a0_ref, a2_ref,
                      kk_ref, ka_ref, rk_ref, gng_ref, gnb_ref, y_ref, s_ref):
    nb = u_ref.shape[0]
    u = u_ref[...]
    um = u + mu_ref[...] * (uprev_ref[...] - u)
    r, k, v, kk, a, log_decay = _rwkv_prep(um, w0_ref[...], w2_ref[...], a0_ref[...], a2_ref[...],
                                           kk_ref[...], ka_ref[...])
    decay = jnp.exp(log_decay)
    ga = ga_ref[...]
    ri = lax.broadcasted_iota(jnp.int32, (HEAD_A, HEAD_A), 0)
    ci = lax.broadcasted_iota(jnp.int32, (HEAD_A, HEAD_A), 1)
    eye = ri == ci
    for h in range(H_A):
        hs = slice(h * HEAD_A, (h + 1) * HEAD_A)
        kk_h = _normalize_kk(kk[:, hs])
        b_h = kk_h * a[:, hs]
        outs = []
        for i in range(nb):
            rs = slice(i, i + 1)
            s = s0_ref[i, h]
            sa = jnp.sum(s * (-kk_h[rs]), axis=-1, keepdims=True)
            v_col = jnp.sum(jnp.where(eye, v[rs, hs], 0.0), axis=-1, keepdims=True)
            s_new = s * decay[rs, hs] + sa * b_h[rs] + v_col * k[rs, hs]
            s_ref[i, h] = s_new
            y_col = jnp.sum(s_new * r[rs, hs], axis=-1, keepdims=True)
            outs.append(jnp.sum(jnp.where(eye, y_col, 0.0), axis=0, keepdims=True))
        y = jnp.concatenate(outs, axis=0)
        out = _rwkv_head_out(y, r[:, hs], k[:, hs], v[:, hs], rk_ref[:, hs], gng_ref[:, hs],
                             gnb_ref[:, hs], ga[:, hs])
        y_ref[:, hs] = out.astype(y_ref.dtype)


def _rwkv_step(u, uprev, ga, s0, params, *, nb=SUBLANES):
    b = u.shape[0]
    assert b % nb == 0
    row = lambda w: pl.BlockSpec((nb, w), lambda i: (i, 0))
    state = pl.BlockSpec((nb, H_A, HEAD_A, HEAD_A), lambda i: (i, 0, 0, 0))
    return pl.pallas_call(
        _rwkv_step_kernel,
        grid=(b // nb,),
        in_specs=[row(A_IN), row(A_IN), row(D_A), state] + [_full(p.shape) for p in params],
        out_specs=[row(D_A), state],
        out_shape=[jax.ShapeDtypeStruct((b, D_A), BF16), jax.ShapeDtypeStruct(s0.shape, F32)],
        compiler_params=_cparams(("parallel",)),
        name="rwkv_step",
    )(u, uprev, ga, s0, *params)


def _mla_prep_kernel(cq_ref, ckv_ref, cos_ref, sin_ref, qg_ref, wuq_ref, kvg_ref, wukt_ref, q_ref, rows_ref):
    cos2 = cos_ref[...]
    sin2 = sin_ref[...]
    qn = _rms(cq_ref[...], qg_ref[...], EPS).astype(BF16)
    q = _dot(qn, wuq_ref[...])
    for h in range(H_B):
        qh = q[:, h * LANES:(h + 1) * LANES]
        q_lat = _dot(qh[:, :NOPE_B].astype(BF16), wukt_ref[h])
        q_rope = qh[:, NOPE_B:NOPE_B + ROPE_B] * cos2 + qh[:, NOPE_B + ROPE_B:] * sin2
        q_ref[h, :, 0:KV_RANK] = (q_lat * MLA_SCALE).astype(q_ref.dtype)
        q_ref[h, :, KV_RANK:MLA_W] = (q_rope * MLA_SCALE).astype(q_ref.dtype)
    ckv = ckv_ref[...]
    rows_ref[:, 0:KV_RANK] = _rms(ckv[:, 0:KV_RANK], kvg_ref[...], EPS)
    rows_ref[:, KV_RANK:MLA_W] = (ckv[:, KV_RANK:KV_RANK + ROPE_B] * cos2
                                  + ckv[:, KV_RANK + ROPE_B:KV_RANK + 2 * ROPE_B] * sin2)


def _mla_prep(cq, ckv, cos2, sin2, q_norm_g, wuq_ext, kv_norm_g, wuk_t, *, tm, pos_tiles):
    m = cq.shape[0]
    row = lambda w: pl.BlockSpec((tm, w), lambda i: (i, 0))
    pos = pl.BlockSpec((tm, ROPE_B), lambda i: (i % pos_tiles, 0))
    return pl.pallas_call(
        _mla_prep_kernel,
        grid=(m // tm,),
        in_specs=[row(Q_RANK), row(2 * LANES), pos, pos, _full((1, Q_RANK)), _full(wuq_ext.shape),
                  _full((1, KV_RANK)), _full(wuk_t.shape)],
        out_specs=[pl.BlockSpec((H_B, tm, MLA_W), lambda i: (0, i, 0)), row(MLA_W)],
        out_shape=[jax.ShapeDtypeStruct((H_B, m, MLA_W), BF16), jax.ShapeDtypeStruct((m, MLA_W), F32)],
        compiler_params=_cparams(("parallel",)),
        name="mla_prep",
    )(cq, ckv, cos2, sin2, q_norm_g.reshape(1, Q_RANK), wuq_ext, kv_norm_g.reshape(1, KV_RANK), wuk_t)


def _mla_out(o_lat, wuv_ref, gb, o_ref, rows_per_head):
    for h in range(H_B):
        o_h = _dot(o_lat[h * rows_per_head:(h + 1) * rows_per_head].astype(BF16), wuv_ref[h])
        hs = slice(h * DV_B, (h + 1) * DV_B)
        o_ref[:, hs] = (o_h * _silu(gb[:, hs])).astype(o_ref.dtype)


def _mla_prep_t_kernel(cq_ref, ckv_ref, cos_ref, sin_ref, qg_ref, wuqt_ref, kvg_ref, wuk_ref,
                       q_ref, rows_ref, rowst_ref):
    tm = cq_ref.shape[1]
    cos2 = cos_ref[...]
    sin2 = sin_ref[...]
    cq = cq_ref[...]
    qn = (cq * lax.rsqrt(jnp.mean(cq * cq, axis=0, keepdims=True) + EPS) * qg_ref[...]).astype(BF16)
    q = _dot(wuqt_ref[...], qn)
    pad = jnp.zeros((2 * LANES - MLA_W, tm), q_ref.dtype)
    for h in range(H_B):
        qh = q[h * LANES:(h + 1) * LANES]
        q_lat = _dot(wuk_ref[h], qh[0:NOPE_B].astype(BF16))
        q_rope = qh[NOPE_B:NOPE_B + ROPE_B] * cos2 + qh[NOPE_B + ROPE_B:] * sin2
        cols = slice(h * tm, (h + 1) * tm)
        q_ref[0:KV_RANK, cols] = (q_lat * MLA_SCALE).astype(q_ref.dtype)
        q_ref[KV_RANK:MLA_W, cols] = (q_rope * MLA_SCALE).astype(q_ref.dtype)
        q_ref[MLA_W:, cols] = pad
    ckv = ckv_ref[...]
    c = ckv[0:KV_RANK]
    cn = c * lax.rsqrt(jnp.mean(c * c, axis=0, keepdims=True) + EPS) * kvg_ref[...]
    kr = ckv[KV_RANK:KV_RANK + ROPE_B] * cos2 + ckv[KV_RANK + ROPE_B:KV_RANK + 2 * ROPE_B] * sin2
    rowst_ref[0:KV_RANK, :] = cn
    rowst_ref[KV_RANK:MLA_W, :] = kr
    rows_t = jnp.concatenate([cn, kr, jnp.zeros((2 * LANES - MLA_W, tm), F32)], axis=0)
    rows_ref[...] = rows_t.T


def _mla_prep_t(cq_t, ckv_t, cos2_t, sin2_t, q_norm_g, wuq_ext_t, kv_norm_g, wuk, *, tm):
    b, _, t = cq_t.shape
    nt = t // tm
    blk = lambda w: pl.BlockSpec((None, w, tm), lambda bi, i: (bi, 0, i))
    pos = pl.BlockSpec((ROPE_B, tm), lambda bi, i: (0, i))
    return pl.pallas_call(
        _mla_prep_t_kernel,
        grid=(b, nt),
        in_specs=[blk(Q_RANK), blk(2 * LANES), pos, pos, _full((Q_RANK, 1)), _full(wuq_ext_t.shape),
                  _full((KV_RANK, 1)), _full(wuk.shape)],
        out_specs=[pl.BlockSpec((None, 2 * LANES, H_B * tm), lambda bi, i: (bi * nt + i, 0, 0)),
                   pl.BlockSpec((None, tm, 2 * LANES), lambda bi, i: (bi, i, 0)),
                   blk(MLA_W)],
        out_shape=[jax.ShapeDtypeStruct((b * nt, 2 * LANES, H_B * tm), BF16),
                   jax.ShapeDtypeStruct((b, t, 2 * LANES), F32),
                   jax.ShapeDtypeStruct((b, MLA_W, t), F32)],
        compiler_params=_cparams(("parallel", "parallel")),
        name="mla_prep_t",
    )(cq_t, ckv_t, cos2_t, sin2_t, q_norm_g.reshape(Q_RANK, 1), wuq_ext_t, kv_norm_g.reshape(KV_RANK, 1), wuk)


def _softmax_update_t(s, m_ref, l_ref, cols):
    m_old = m_ref[:, cols]
    m_new = jnp.maximum(m_old, jnp.max(s, axis=0, keepdims=True))
    alpha = jnp.exp(m_old - m_new)
    p = jnp.exp(s - m_new)
    l_ref[:, cols] = alpha * l_ref[:, cols] + jnp.sum(p, axis=0, keepdims=True)
    m_ref[:, cols] = m_new
    return alpha, p


def _bias_kernel(rb_ref, dist_ref, o_ref):
    dist = dist_ref[...]
    n = jnp.maximum(dist, 0)
    max_exact = NUM_BUCKETS // 2
    n_safe = jnp.maximum(n, max_exact).astype(F32)
    large = max_exact + (jnp.log(n_safe / max_exact) / math.log(MAX_DISTANCE / max_exact)
                         * (NUM_BUCKETS - max_exact)).astype(jnp.int32)
    large = jnp.minimum(large, NUM_BUCKETS - 1)
    bucket = jnp.where(n < max_exact, n, large)
    for h in range(H_C):
        bias = jnp.zeros(dist.shape, F32)
        for kb in range(NUM_BUCKETS):
            bias = jnp.where(bucket == kb, rb_ref[kb * H_C + h], bias)
        o_ref[h] = jnp.where(dist >= 0, bias, NEG_INF)


def _bias_tiles(rel_bias, dist):
    g, r, c = dist.shape
    grid_spec = pltpu.PrefetchScalarGridSpec(
        num_scalar_prefetch=1,
        grid=(g,),
        in_specs=[pl.BlockSpec((None, r, c), lambda i, rb: (i, 0, 0))],
        out_specs=pl.BlockSpec((None, H_C, r, c), lambda i, rb: (i, 0, 0, 0)),
    )
    return pl.pallas_call(
        _bias_kernel,
        grid_spec=grid_spec,
        out_shape=jax.ShapeDtypeStruct((g, H_C, r, c), F32),
        compiler_params=_cparams(("arbitrary",)),
        name="rel_bias_tiles",
    )(rel_bias.reshape(-1), dist)


def _diff_lambda(lam_ref, lam_init):
    lam = lam_ref[...]
    e1 = jnp.exp(jnp.sum(lam[0:1] * lam[1:2], axis=-1, keepdims=True))
    e2 = jnp.exp(jnp.sum(lam[2:3] * lam[3:4], axis=-1, keepdims=True))
    return e1 - e2 + lam_init


def _diff_queries(qc):
    lane = lax.broadcasted_iota(jnp.int32, qc.shape, 1)
    qs = qc * DIFF_SCALE
    groups = []
    for h in range(H_C):
        for c in range(2):
            lo = h * DV_C + c * DC
            groups.append(jnp.where((lane >= lo) & (lane < lo + DC), qs, 0.0))
    return jnp.concatenate(groups, axis=0)


def _diff_out(acc, l, lam, lam_init, sg, gc, o_ref, rows):
    for h in range(H_C):
        hs = slice(h * DV_C, (h + 1) * DV_C)
        r1 = slice((2 * h) * rows, (2 * h + 1) * rows)
        r2 = slice((2 * h + 1) * rows, (2 * h + 2) * rows)
        o = acc[r1, hs] / l[r1] - lam * (acc[r2, hs] / l[r2])
        o = _rms(o, sg, SUBLN_EPS) * (1.0 - lam_init)
        o_ref[:, hs] = (o * _silu(gc[:, hs])).astype(o_ref.dtype)


def _prompt_attn_kernel(pi_ref, pj_ref,
                        q_ref, k_ref, ct_ref, mask_ref, gb_ref, wuvt_ref,
                        qc_ref, kc_ref, vt_ref, gc_ref, bias_ref, lam_ref, sg_ref,
                        ob_ref, oc_ref,
                        m_ref, l_ref, acc_ref, qbd_ref, md_ref, ld_ref, accd_ref, *, lam_init):
    step = pl.program_id(1)
    i = pi_ref[step]
    j = pj_ref[step]
    tq = ob_ref.shape[0]
    nblk = 2 * H_C

    @pl.when(j == 0)
    def _():
        m_ref[...] = jnp.full_like(m_ref, NEG_INF)
        l_ref[...] = jnp.zeros_like(l_ref)
        acc_ref[...] = jnp.zeros_like(acc_ref)
        q = qc_ref[...] * DIFF_SCALE
        feat = lax.broadcasted_iota(jnp.int32, q.shape, 0)
        for blk in range(nblk):
            lo = (blk // 2) * DV_C + (blk % 2) * DC
            qbd_ref[:, blk * tq:(blk + 1) * tq] = jnp.where((feat >= lo) & (feat < lo + DC), q, 0.0).astype(BF16)
        md_ref[...] = jnp.full_like(md_ref, NEG_INF)
        ld_ref[...] = jnp.zeros_like(ld_ref)
        accd_ref[...] = jnp.zeros_like(accd_ref)

    k = k_ref[...].astype(BF16)
    ct = ct_ref[...].astype(BF16)
    mask = mask_ref[jnp.minimum(i - j, 1)]
    heads = range(H_B)
    cols = [slice(h * tq, (h + 1) * tq) for h in heads]
    kc = kc_ref[...].astype(BF16)
    vt = vt_ref[...].astype(BF16)
    tile = jnp.minimum(i - j, 2)
    blocks = range(nblk)
    dcols = [slice(blk * tq, (blk + 1) * tq) for blk in blocks]
    s_b = [_dot(k, q_ref[:, cols[h]]) + mask for h in heads]
    s_d = [_dot(kc, qbd_ref[:, dcols[blk]]) + bias_ref[tile, blk // 2] for blk in blocks]
    ap_b = [_softmax_update_t(s_b[h], m_ref, l_ref, cols[h]) for h in heads]
    ap_d = [_softmax_update_t(s_d[blk], md_ref, ld_ref, dcols[blk]) for blk in blocks]
    pv_b = [_dot(ct, ap_b[h][1].astype(BF16)) for h in heads]
    pv_d = [_dot(vt[(blk // 2) * DV_C:(blk // 2 + 1) * DV_C], ap_d[blk][1].astype(BF16)) for blk in blocks]
    for h in heads:
        acc_ref[:, cols[h]] = ap_b[h][0] * acc_ref[:, cols[h]] + pv_b[h]
    for blk in blocks:
        accd_ref[blk] = ap_d[blk][0] * accd_ref[blk] + pv_d[blk]

    @pl.when(j == i)
    def _():
        outs = []
        for h in heads:
            o_lat = (acc_ref[:, cols[h]] / l_ref[:, cols[h]]).astype(BF16)
            outs.append(_dot(wuvt_ref[h], o_lat))
        o = jnp.concatenate(outs, axis=0).T
        ob_ref[...] = (o * _silu(gb_ref[...])).astype(ob_ref.dtype)
        lam = _diff_lambda(lam_ref, lam_init)
        outs = []
        for h in range(H_C):
            o = (accd_ref[2 * h] / ld_ref[:, dcols[2 * h]]
                 - lam * (accd_ref[2 * h + 1] / ld_ref[:, dcols[2 * h + 1]]))
            o = o * lax.rsqrt(jnp.mean(o * o, axis=0, keepdims=True) + SUBLN_EPS) * sg_ref[...]
            outs.append(o * (1.0 - lam_init))
        o = jnp.concatenate(outs, axis=0).T
        oc_ref[...] = (o * _silu(gc_ref[...])).astype(oc_ref.dtype)


def _prompt_attn(q_t, rows_pad, rows_t, mask, gb, wuv_t, qc_t, kc, vc_t, gc, bias_tiles, lam_vecs, subln_g, lam_init):
    b, t, _ = kc.shape
    tq = ATTN_TILE
    nq = t // tq
    pairs = [(i, j) for i in range(nq) for j in range(i + 1)]
    pair_i = jnp.asarray([p[0] for p in pairs], jnp.int32)
    pair_j = jnp.asarray([p[1] for p in pairs], jnp.int32)
    nblk = 2 * H_C
    q_tile = lambda w: pl.BlockSpec((None, tq, w), lambda bi, s, pi, pj: (bi, pi[s], 0))
    k_tile = lambda w: pl.BlockSpec((None, tq, w), lambda bi, s, pi, pj: (bi, pj[s], 0))
    qt_tile = lambda w: pl.BlockSpec((None, w, tq), lambda bi, s, pi, pj: (bi, 0, pi[s]))
    kt_tile = lambda w: pl.BlockSpec((None, w, tq), lambda bi, s, pi, pj: (bi, 0, pj[s]))
    const = lambda shape: pl.BlockSpec(shape, lambda bi, s, pi, pj: (0,) * len(shape))
    grid_spec = pltpu.PrefetchScalarGridSpec(
        num_scalar_prefetch=2,
        grid=(b, len(pairs)),
        in_specs=[pl.BlockSpec((None, 2 * LANES, H_B * tq), lambda bi, s, pi, pj: (bi * nq + pi[s], 0, 0)),
                  k_tile(2 * LANES), kt_tile(KV_RANK), const(mask.shape), q_tile(D_B), const(wuv_t.shape),
                  qt_tile(D_C), k_tile(D_C), kt_tile(D_C), q_tile(D_C), const(bias_tiles.shape),
                  const(lam_vecs.shape), const((DV_C, 1))],
        out_specs=[q_tile(D_B), q_tile(D_C)],
        scratch_shapes=[pltpu.VMEM((1, H_B * tq), F32), pltpu.VMEM((1, H_B * tq), F32),
                        pltpu.VMEM((KV_RANK, H_B * tq), F32),
                        pltpu.VMEM((D_C, nblk * tq), BF16), pltpu.VMEM((1, nblk * tq), F32),
                        pltpu.VMEM((1, nblk * tq), F32), pltpu.VMEM((nblk, DV_C, tq), F32)],
    )
    return pl.pallas_call(
        functools.partial(_prompt_attn_kernel, lam_init=lam_init),
        grid_spec=grid_spec,
        out_shape=[jax.ShapeDtypeStruct((b, t, D_B), BF16), jax.ShapeDtypeStruct((b, t, D_C), BF16)],
        compiler_params=_cparams(("parallel", "arbitrary")),
        name="prompt_attn",
    )(pair_i, pair_j, q_t, rows_pad, rows_t, mask, gb, wuv_t, qc_t, kc, vc_t, gc, bias_tiles, lam_vecs,
      subln_g.reshape(DV_C, 1))


def _softmax_step(s, m, l):
    m_new = jnp.maximum(m, jnp.max(s, axis=-1, keepdims=True))
    alpha = jnp.exp(m - m_new)
    p = jnp.exp(s - m_new)
    return m_new, alpha * l + jnp.sum(p, axis=-1, keepdims=True), alpha, p


def _decode_attn_kernel(pt_ref, q_ref, row_ref, gb_ref, wuv_ref, qc_ref, kn_ref, vn_ref, gc_ref, bias_ref, lam_ref,
                        sg_ref, cm_hbm, ck_hbm, cv_hbm, ob_ref, oc_ref, mbuf, kbuf, vbuf, sems,
                        *, layer, pages_per_seq, lam_init):
    n_pages = DECODE_PAGES_PER_CHUNK
    seq = pl.program_id(0)
    chunks = pages_per_seq // n_pages
    total = pl.num_programs(0) * chunks

    def chunk_copies(g, slot):
        out = []
        for p in range(n_pages):
            page = pt_ref[g * n_pages + p]
            out.append(pltpu.make_async_copy(cm_hbm.at[layer, page], mbuf.at[slot, p], sems.at[slot, 0]))
            out.append(pltpu.make_async_copy(ck_hbm.at[layer, page], kbuf.at[slot, p], sems.at[slot, 1]))
            out.append(pltpu.make_async_copy(cv_hbm.at[layer, page], vbuf.at[slot, p], sems.at[slot, 2]))
        return out

    @pl.when(seq == 0)
    def _():
        for cp in chunk_copies(0, 0):
            cp.start()

    q = q_ref[...]
    qbd = _diff_queries(qc_ref[...])
    qb = qbd.astype(BF16)

    def chunk_body(c, carry):
        mb, lb, accb, md, ld, accd = carry
        g = seq * chunks + c
        slot = lax.rem(g, 2)

        @pl.when(g + 1 < total)
        def _():
            for cp in chunk_copies(g + 1, 1 - slot):
                cp.start()

        for cp in chunk_copies(g, slot):
            cp.wait()

        ks = [mbuf[slot, p].astype(BF16) for p in range(n_pages)]
        s_b = jnp.concatenate([_dot(q, kp) for kp in ks], axis=1)
        mb, lb, alpha_b, p_b = _softmax_step(s_b, mb, lb)
        pv_b = _dot_nt(p_b[:, 0:PAGE_SIZE].astype(BF16), ks[0][0:KV_RANK, :])
        for p in range(1, n_pages):
            pv_b = pv_b + _dot_nt(p_b[:, p * PAGE_SIZE:(p + 1) * PAGE_SIZE].astype(BF16), ks[p][0:KV_RANK, :])
        accb = alpha_b * accb + pv_b

        s_d = jnp.concatenate([_dot(qb, kbuf[slot, p].astype(BF16)) for p in range(n_pages)], axis=1)
        s_d = s_d + bias_ref[jnp.where(c == chunks - 1, 1, 0)]
        md, ld, alpha_d, p_d = _softmax_step(s_d, md, ld)
        pv_d = _dot_nt(p_d[:, 0:PAGE_SIZE].astype(BF16), vbuf[slot, 0].astype(BF16))
        for p in range(1, n_pages):
            pv_d = pv_d + _dot_nt(p_d[:, p * PAGE_SIZE:(p + 1) * PAGE_SIZE].astype(BF16), vbuf[slot, p].astype(BF16))
        accd = alpha_d * accd + pv_d
        return mb, lb, accb, md, ld, accd

    init = (jnp.full((H_B, 1), NEG_INF, F32), jnp.zeros((H_B, 1), F32), jnp.zeros((H_B, KV_RANK), F32),
            jnp.full((2 * H_C, 1), NEG_INF, F32), jnp.zeros((2 * H_C, 1), F32), jnp.zeros((2 * H_C, D_C), F32))
    mb, lb, accb, md, ld, accd = lax.fori_loop(0, chunks, chunk_body, init)

    row = row_ref[...]
    s_new = jnp.sum(q.astype(F32) * row, axis=-1, keepdims=True)
    mb, lb, alpha, p_new = _softmax_step(s_new, mb, lb)
    accb = alpha * accb + p_new * row[:, 0:KV_RANK]
    _mla_out(accb / lb, wuv_ref, gb_ref[...], ob_ref, 1)

    s_new = jnp.sum(qbd * kn_ref[...], axis=-1, keepdims=True) + bias_ref[2][:, 0:1]
    md, ld, alpha, p_new = _softmax_step(s_new, md, ld)
    accd = alpha * accd + p_new * vn_ref[...]
    _diff_out(accd, ld, _diff_lambda(lam_ref, lam_init), lam_init, sg_ref[...], gc_ref[...], oc_ref, 1)


def _decode_attn(page_table_flat, q, rows_new, gb, wuv, qc, kc, vc, gc, bias_rows, lam_vecs, subln_g,
                 cache_mla, cache_k, cache_v, layer, pages_per_seq, lam_init):
    b = q.shape[0]
    n_pages = DECODE_PAGES_PER_CHUNK
    assert pages_per_seq % n_pages == 0
    per_b = lambda shape: pl.BlockSpec((None,) + shape, lambda bi, pt: (bi,) + (0,) * len(shape))
    const = lambda shape: pl.BlockSpec(shape, lambda bi, pt: (0,) * len(shape))
    hbm = pl.BlockSpec(memory_space=pl.ANY)
    grid_spec = pltpu.PrefetchScalarGridSpec(
        num_scalar_prefetch=1,
        grid=(b,),
        in_specs=[per_b((H_B, MLA_W)), per_b((1, MLA_W)), per_b((1, D_B)), const(wuv.shape),
                  per_b((1, D_C)), per_b((1, D_C)), per_b((1, D_C)), per_b((1, D_C)), const(bias_rows.shape),
                  const(lam_vecs.shape), const((1, DV_C)), hbm, hbm, hbm],
        out_specs=[per_b((1, D_B)), per_b((1, D_C))],
        scratch_shapes=[pltpu.VMEM((2, n_pages, MLA_W, PAGE_SIZE), F32),
                        pltpu.VMEM((2, n_pages, D_C, PAGE_SIZE), F32),
                        pltpu.VMEM((2, n_pages, D_C, PAGE_SIZE), F32),
                        pltpu.SemaphoreType.DMA((2, 3))],
    )
    return pl.pallas_call(
        functools.partial(_decode_attn_kernel, layer=layer, pages_per_seq=pages_per_seq, lam_init=lam_init),
        grid_spec=grid_spec,
        out_shape=[jax.ShapeDtypeStruct((b, 1, D_B), BF16), jax.ShapeDtypeStruct((b, 1, D_C), BF16)],
        compiler_params=_cparams(("arbitrary",)),
        name="decode_attn",
    )(page_table_flat, q, rows_new, gb, wuv, qc, kc, vc, gc, bias_rows, lam_vecs, subln_g.reshape(1, DV_C),
      cache_mla, cache_k, cache_v)


def _permute_w_in(w):
    o_ckv = A_IN + D_A + Q_RANK
    o_kr = o_ckv + KV_RANK
    o_gb = o_kr + ROPE_B
    half = ROPE_B // 2
    pad = jnp.zeros((w.shape[0], 2 * LANES - KV_RANK - 2 * ROPE_B), w.dtype)
    out = jnp.concatenate([w[:, :o_gb], w[:, o_kr + half:o_gb], w[:, o_kr:o_kr + half], pad, w[:, o_gb:]], axis=1)
    assert out.shape[1] == IN_COLS_PERM
    return out.astype(BF16)


def _extend_w_uq(w):
    w = w.reshape(Q_RANK, H_B, NOPE_B + ROPE_B)
    half = ROPE_B // 2
    rope = w[:, :, NOPE_B:]
    swapped = jnp.concatenate([rope[:, :, half:], rope[:, :, :half]], axis=-1)
    return jnp.concatenate([w, swapped], axis=-1).reshape(Q_RANK, H_B * LANES).astype(BF16)


def _rope_tables(pos):
    inv = ROPE_THETA ** (-jnp.arange(0, ROPE_B, 2, dtype=F32) / ROPE_B)
    ang = pos.astype(F32)[:, None] * inv[None, :]
    cos, sin = jnp.cos(ang), jnp.sin(ang)
    return jnp.concatenate([cos, cos], axis=-1), jnp.concatenate([-sin, sin], axis=-1)


def _layer_weights(l, W):
    row = lambda a: a.reshape(1, -1)
    rwkv = (row(W["mu_shift"][l]), row(W["rw_w0"][l]), W["rw_w2"][l], row(W["rw_a0"][l]), W["rw_a2"][l],
            row(W["rw_k_k"][l]), row(W["rw_k_a"][l]), row(W["rw_r_k"][l]), row(W["rw_gn_g"][l]),
            row(W["rw_gn_b"][l]))
    w_in = _permute_w_in(W["w_in"][l])
    wuq = _extend_w_uq(W["mla_w_uq"][l])
    return dict(
        w_in=w_in,
        w_in_n=jnp.concatenate([w_in[:, a:b] for a, b in PROMPT_SEGS_N], axis=1),
        w_in_t=jnp.concatenate([w_in[:, a:b] for a, b in PROMPT_SEGS_T], axis=1).T,
        rwkv=rwkv,
        wuq=wuq,
        wuq_t=wuq.T,
        wuk=jnp.transpose(W["mla_w_uk"][l], (1, 0, 2)).astype(BF16),
        wuv_t=jnp.transpose(W["mla_w_uv"][l], (1, 2, 0)).astype(BF16),
        wuk_t=jnp.transpose(W["mla_w_uk"][l], (1, 2, 0)).astype(BF16),
        wuv=jnp.transpose(W["mla_w_uv"][l], (1, 0, 2)).astype(BF16),
        lam_vecs=jnp.stack([W["diff_lam_q1"][l], W["diff_lam_k1"][l], W["diff_lam_q2"][l], W["diff_lam_k2"][l]]),
        wo=W["w_out"][l].astype(BF16),
        wple=W["w_ple"][l].astype(BF16),
        wg=W["w_ple_gate"][l].astype(BF16),
    )


def _run_prompt(x, p, W, LW, depth):
    b, t, d = x.shape
    m = b * t
    tm = ATTN_TILE
    cos2, sin2 = _rope_tables(jnp.arange(t, dtype=jnp.int32))
    cos2_t, sin2_t = cos2.T, sin2.T
    tile = jnp.arange(ATTN_TILE, dtype=jnp.int32)
    dist = (jnp.arange(3, dtype=jnp.int32)[:, None, None] * ATTN_TILE + tile[None, None, :] - tile[None, :, None])
    bias_tiles = _bias_tiles(W["rel_bias"], dist)
    causal = jnp.stack([jnp.where(dist[0] >= 0, 0.0, NEG_INF).astype(F32), jnp.zeros(dist.shape[1:], F32)])
    uprev0 = jnp.zeros((b, 1, A_IN), F32)
    s0 = jnp.zeros((b, H_A, HEAD_A, HEAD_A), F32)
    segs_n = _pack_segments(PROMPT_SEGS_N)
    segs_t = _pack_segments(PROMPT_SEGS_T)
    h = x.reshape(m, d)
    mla_rows, k_rows, v_rows, wkv_out, shift_out = [], [], [], [], []
    for l in range(depth):
        lw = LW[l]
        u, ga, gb, gc, kc, cq_t, ckv_t, qc_t, kc_t, vc_t, xn_last = _inproj(
            h, W["norm_g"][l], lw["w_in_n"], segs_n, normalize=True, rows_per_seq=t, tm=tm,
            wt_bf16=lw["w_in_t"], segs_t=segs_t)
        y_a, s_new = _rwkv_chunked(u.reshape(b, t, A_IN), uprev0, ga.reshape(b, t, D_A), s0, lw["rwkv"])
        q_t, rows_pad, rows_t = _mla_prep_t(cq_t, ckv_t, cos2_t, sin2_t, W["mla_q_norm_g"][l], lw["wuq_t"],
                                            W["mla_kv_norm_g"][l], lw["wuk"], tm=tm)
        lam_init = 0.8 - 0.6 * math.exp(-0.3 * l)
        y_b, y_c = _prompt_attn(q_t, rows_pad, rows_t, causal, gb.reshape(b, t, D_B), lw["wuv_t"],
                                qc_t, kc.reshape(b, t, D_C), vc_t, gc.reshape(b, t, D_C), bias_tiles,
                                lw["lam_vecs"], W["diff_subln_g"][l], lam_init)
        h = _outproj(h, y_a.reshape(m, D_A), y_b.reshape(m, D_B), y_c.reshape(m, D_C), p[l].reshape(m, PLE_DIM),
                     lw["wo"], lw["wple"], lw["wg"], W["final_norm_g"], final=(l == depth - 1), tm=tm)
        mla_rows.append(jnp.transpose(rows_t, (0, 2, 1)))
        k_rows.append(jnp.transpose(kc_t.reshape(b, H_C, 2 * DC, t), (0, 3, 1, 2)))
        v_rows.append(jnp.transpose(vc_t.reshape(b, H_C, DV_C, t), (0, 3, 1, 2)))
        wkv_out.append(s_new)
        shift_out.append(xn_last.reshape(b, d))
    return (h.reshape(b, t, d), jnp.stack(mla_rows), jnp.stack(k_rows), jnp.stack(v_rows), jnp.stack(wkv_out),
            jnp.stack(shift_out))


def _run_sample(x, p, state_shift, state_wkv, cache_mla, cache_k, cache_v, page_table, W, LW, depth):
    b, t, d = x.shape
    assert t == 1
    pages_per_seq = page_table.shape[1]
    past_len = pages_per_seq * PAGE_SIZE
    tm = b
    cos2, sin2 = _rope_tables(jnp.full((b,), past_len, dtype=jnp.int32))
    pt_flat = page_table.reshape(-1).astype(jnp.int32)
    step_keys = DECODE_PAGES_PER_CHUNK * PAGE_SIZE
    key_in_step = jnp.arange(step_keys, dtype=jnp.int32)
    dist = jnp.stack([past_len - key_in_step,
                      past_len - (past_len - step_keys + key_in_step),
                      jnp.zeros((step_keys,), jnp.int32)])
    dist = jnp.broadcast_to(dist[:, None, :], (3, 2, step_keys))
    bias = _bias_tiles(W["rel_bias"], dist)
    bias_rows = bias.reshape(3, 2 * H_C, step_keys)
    to_feature_major = lambda c: jnp.transpose(c, (0, 1, 3, 4, 2)).reshape(c.shape[:2] + (D_C, PAGE_SIZE))
    cache_k2 = to_feature_major(cache_k)
    cache_v2 = to_feature_major(cache_v)
    cache_mla_t = jnp.transpose(cache_mla, (0, 1, 3, 2))
    h = x.reshape(b, d)
    mla_rows, k_rows, v_rows, wkv_out, shift_out = [], [], [], [], []
    for l in range(depth):
        lw = LW[l]
        u, ga, cq, ckv, gb, qc, kc, vc, gc, xn = _inproj(
            h, W["norm_g"][l], lw["w_in"], ALL_SEGS, normalize=True, rows_per_seq=1, tm=tm)
        (uprev,) = _inproj(state_shift[l], W["norm_g"][l], lw["w_in"][:, :A_IN], (SEG_U,), normalize=False,
                           rows_per_seq=1, tm=tm)
        y_a, s_new = _rwkv_step(u, uprev, ga, state_wkv[l], lw["rwkv"])
        q, rows = _mla_prep(cq, ckv, cos2, sin2, W["mla_q_norm_g"][l], lw["wuq"], W["mla_kv_norm_g"][l],
                            lw["wuk_t"], tm=tm, pos_tiles=1)
        lam_init = 0.8 - 0.6 * math.exp(-0.3 * l)
        y_b, y_c = _decode_attn(pt_flat, jnp.transpose(q, (1, 0, 2)), rows.reshape(b, 1, MLA_W),
                                gb.reshape(b, 1, D_B), lw["wuv"], qc.reshape(b, 1, D_C), kc.reshape(b, 1, D_C),
                                vc.reshape(b, 1, D_C), gc.reshape(b, 1, D_C), bias_rows, lw["lam_vecs"],
                                W["diff_subln_g"][l], cache_mla_t, cache_k2, cache_v2, l, pages_per_seq, lam_init)
        h = _outproj(h, y_a, y_b.reshape(b, D_B), y_c.reshape(b, D_C), p[l].reshape(b, PLE_DIM),
                     lw["wo"], lw["wple"], lw["wg"], W["final_norm_g"], final=(l == depth - 1), tm=tm)
        mla_rows.append(rows.reshape(b, 1, MLA_W))
        k_rows.append(kc.reshape(b, 1, H_C, 2 * DC))
        v_rows.append(vc.reshape(b, 1, H_C, DV_C))
        wkv_out.append(s_new)
        shift_out.append(xn)
    return (h.reshape(b, 1, d), jnp.stack(mla_rows), jnp.stack(k_rows), jnp.stack(v_rows), jnp.stack(wkv_out),
            jnp.stack(shift_out))


def kernel(x_prompt, x_sample, cache_mla, cache_diff_k, cache_diff_v, state_wkv, state_shift, page_table,
           p_prompt, p_sample, norm_g, w_in, mu_shift, rw_w0, rw_w2, rw_a0, rw_a2, rw_k_k, rw_k_a, rw_r_k,
           rw_gn_g, rw_gn_b, mla_q_norm_g, mla_w_uq, mla_kv_norm_g, mla_w_uk, mla_w_uv, diff_lam_q1,
           diff_lam_k1, diff_lam_q2, diff_lam_k2, diff_subln_g, rel_bias, w_out, w_ple, w_ple_gate,
           final_norm_g):
    W = {"norm_g": norm_g, "w_in": w_in, "mu_shift": mu_shift, "rw_w0": rw_w0, "rw_w2": rw_w2, "rw_a0": rw_a0,
         "rw_a2": rw_a2, "rw_k_k": rw_k_k, "rw_k_a": rw_k_a, "rw_r_k": rw_r_k, "rw_gn_g": rw_gn_g,
         "rw_gn_b": rw_gn_b, "mla_q_norm_g": mla_q_norm_g, "mla_w_uq": mla_w_uq, "mla_kv_norm_g": mla_kv_norm_g,
         "mla_w_uk": mla_w_uk, "mla_w_uv": mla_w_uv, "diff_lam_q1": diff_lam_q1, "diff_lam_k1": diff_lam_k1,
         "diff_lam_q2": diff_lam_q2, "diff_lam_k2": diff_lam_k2, "diff_subln_g": diff_subln_g,
         "rel_bias": rel_bias, "w_out": w_out, "w_ple": w_ple, "w_ple_gate": w_ple_gate,
         "final_norm_g": final_norm_g}
    depth = w_in.shape[0]
    LW = [_layer_weights(l, W) for l in range(depth)]
    y_p, mla_p, dk_p, dv_p, wkv_p, sh_p = _run_prompt(x_prompt, p_prompt, W, LW, depth)
    y_s, mla_s, dk_s, dv_s, wkv_s, sh_s = _run_sample(x_sample, p_sample, state_shift, state_wkv, cache_mla,
                                                      cache_diff_k, cache_diff_v, page_table, W, LW, depth)
    return (y_p, y_s, mla_p, mla_s, dk_p, dk_s, dv_p, dv_s, wkv_p, wkv_s, sh_p, sh_s)
```

```python
import functools
import math

import jax
import jax.numpy as jnp
from jax import lax
from jax.experimental import pallas as pl
from jax.experimental.pallas import tpu as pltpu

F32 = jnp.float32
BF16 = jnp.bfloat16

LANES = 128
SUBLANES = 8
VMEM_LIMIT_BYTES = 56 * 1024 * 1024

D_MODEL = 1024
HEAD_A = 64
D_A = 512
H_A = D_A // HEAD_A
W_LORA = 64
A_LORA = 64
A_IN = 3 * D_A + W_LORA + A_LORA
D_B = 256
DV_B = 64
H_B = D_B // DV_B
NOPE_B = 64
ROPE_B = 32
Q_RANK = 256
KV_RANK = 128
MLA_W = KV_RANK + ROPE_B
ROPE_THETA = 10000.0
D_C = 256
DV_C = 64
H_C = D_C // DV_C
DC = DV_C // 2
NUM_BUCKETS = 32
MAX_DISTANCE = 128
PLE_DIM = 256
PAGE_SIZE = 128
NEG_INF = -1e30
EPS = 1e-6
GN_EPS = 64e-5
SUBLN_EPS = 1e-5
MLA_SCALE = (NOPE_B + ROPE_B) ** -0.5
DIFF_SCALE = DC ** -0.5

SEG_U = (0, A_IN)
SEG_GA = (A_IN, A_IN + D_A)
SEG_CQ = (SEG_GA[1], SEG_GA[1] + Q_RANK)
SEG_CKV = (SEG_CQ[1], SEG_CQ[1] + 2 * LANES)
SEG_GB = (SEG_CKV[1], SEG_CKV[1] + D_B)
SEG_QC = (SEG_GB[1], SEG_GB[1] + D_C)
SEG_KC = (SEG_QC[1], SEG_QC[1] + D_C)
SEG_VC = (SEG_KC[1], SEG_KC[1] + D_C)
SEG_GC = (SEG_VC[1], SEG_VC[1] + D_C)
IN_COLS_PERM = SEG_GC[1]


def _pack_segments(segs):
    out, pos = [], 0
    for a, b in segs:
        out.append((pos, pos + b - a))
        pos += b - a
    return tuple(out)


PROMPT_SEGS_N = (SEG_U, SEG_GA, SEG_GB, SEG_GC, SEG_KC)
PROMPT_SEGS_T = (SEG_CQ, SEG_CKV, SEG_QC, SEG_KC, SEG_VC)
SAMPLE_SEGS_N = (SEG_CQ, SEG_CKV, SEG_GB, SEG_QC, SEG_KC, SEG_VC, SEG_GC)
SAMPLE_SEGS_T = (SEG_U, SEG_GA)

RWKV_CHUNK = 64
RWKV_BLOCK = 128
ATTN_TILE = 256
DECODE_PAGES_PER_CHUNK = 16


def _cparams(semantics):
    return pltpu.CompilerParams(dimension_semantics=semantics, vmem_limit_bytes=VMEM_LIMIT_BYTES)


def _full(shape):
    n = len(shape)
    return pl.BlockSpec(shape, lambda *_: (0,) * n)


def _sigmoid(x):
    return 1.0 / (1.0 + jnp.exp(-x))


def _silu(x):
    return x * _sigmoid(x)


def _rms(x, g, eps):
    return x * lax.rsqrt(jnp.mean(x * x, axis=-1, keepdims=True) + eps) * g


def _dot(a, b, **kw):
    return jnp.dot(a, b, preferred_element_type=F32, **kw)


def _dot_nt(a, b, **kw):
    return lax.dot_general(a, b, (((1,), (1,)), ((), ())), preferred_element_type=F32, **kw)


def _dot_tn(a, b, **kw):
    return lax.dot_general(a, b, (((0,), (0,)), ((), ())), preferred_element_type=F32, **kw)


def _split3(x):
    hi = x.astype(BF16)
    rest = x - hi.astype(F32)
    mid = rest.astype(BF16)
    lo = (rest - mid.astype(F32)).astype(BF16)
    return hi, mid, lo


def _dot_split_rhs(a_exact, b):
    hi, mid, lo = _split3(b)
    return _dot(a_exact, hi) + (_dot(a_exact, mid) + _dot(a_exact, lo))


def _dot_split_lhs(a, b_exact):
    hi, mid, lo = _split3(a)
    return _dot(hi, b_exact) + (_dot(mid, b_exact) + _dot(lo, b_exact))


def _mm(a, b):
    return _dot(a.astype(BF16), b.astype(BF16))


def _mm_nt(a, b):
    return _dot_nt(a.astype(BF16), b.astype(BF16))


def _mm_tn(a, b):
    return _dot_tn(a.astype(BF16), b.astype(BF16))


def _inproj_kernel(h_ref, g_ref, w_ref, wt_ref, *out_refs, normalize, segs, segs_t, emit_xn):
    x = h_ref[...]
    xn = _rms(x, g_ref[...], EPS) if normalize else x
    xb = xn.astype(BF16)
    for o_ref, (a, b) in zip(out_refs, segs):
        o_ref[...] = _dot(xb, w_ref[:, a:b])
    for o_ref, (a, b) in zip(out_refs[len(segs):], segs_t):
        o_ref[...] = _dot_nt(wt_ref[a:b, :], xb)
    n_proj = len(segs) + len(segs_t)
    if emit_xn == "last_row":
        rows = x.shape[0]
        out_refs[n_proj][...] = xn[rows - 1:rows, :]
    elif emit_xn == "all":
        out_refs[n_proj][...] = xn


def _inproj(h2d, norm_g, w_bf16, segs, *, normalize, rows_per_seq, tm, wt_bf16=None, segs_t=()):
    m, d = h2d.shape
    assert m % tm == 0 and rows_per_seq % tm == 0 or rows_per_seq == 1
    tiles_per_seq = max(rows_per_seq // tm, 1)
    t_cols = rows_per_seq if rows_per_seq > 1 else m
    t_tiles = t_cols // tm
    if w_bf16 is None:
        w_bf16 = jnp.zeros((d, LANES), BF16)
    if wt_bf16 is None:
        wt_bf16 = jnp.zeros((SUBLANES, d), BF16)
    out_shapes = [jax.ShapeDtypeStruct((m, b - a), F32) for a, b in segs]
    out_specs = [pl.BlockSpec((tm, b - a), lambda i: (i, 0)) for a, b in segs]
    for a, b in segs_t:
        out_shapes.append(jax.ShapeDtypeStruct((m // t_cols, b - a, t_cols), F32))
        out_specs.append(pl.BlockSpec((None, b - a, tm), lambda i: (i // t_tiles, 0, i % t_tiles)))
    emit_xn = None
    if normalize:
        if rows_per_seq == 1:
            emit_xn = "all"
            out_shapes.append(jax.ShapeDtypeStruct((m, d), F32))
            out_specs.append(pl.BlockSpec((tm, d), lambda i: (i, 0)))
        else:
            emit_xn = "last_row"
            out_shapes.append(jax.ShapeDtypeStruct((m // rows_per_seq, 1, d), F32))
            out_specs.append(pl.BlockSpec((None, 1, d), lambda i: (i // tiles_per_seq, 0, 0)))
    kern = functools.partial(_inproj_kernel, normalize=normalize, segs=segs, segs_t=segs_t, emit_xn=emit_xn)
    return pl.pallas_call(
        kern,
        grid=(m // tm,),
        in_specs=[pl.BlockSpec((tm, d), lambda i: (i, 0)), _full((1, d)), _full(w_bf16.shape), _full(wt_bf16.shape)],
        out_specs=out_specs,
        out_shape=out_shapes,
        compiler_params=_cparams(("arbitrary",)),
        name="inproj",
    )(h2d, norm_g.reshape(1, d), w_bf16, wt_bf16)


def _outproj_kernel(h_ref, ya_ref, yb_ref, yc_ref, p_ref, wo_ref, wple_ref, wg_ref, fng_ref, o_ref, *, final):
    mixed = (_dot(ya_ref[...], wo_ref[0:D_A, :])
             + _dot(yb_ref[...], wo_ref[D_A:D_A + D_B, :])
             + _dot(yc_ref[...], wo_ref[D_A + D_B:, :]))
    h2 = h_ref[...] + mixed
    ple = _dot(p_ref[...].astype(BF16), wple_ref[...])
    gate = _sigmoid(_dot(h2.astype(BF16), wg_ref[...]))
    h3 = h2 + ple * gate
    o_ref[...] = _rms(h3, fng_ref[...], EPS) if final else h3


def _outproj(h2d, ya, yb, yc, p2d, wo, wple, wg, final_g, *, final, tm):
    m, d = h2d.shape
    row = lambda w: pl.BlockSpec((tm, w), lambda i: (i, 0))
    return pl.pallas_call(
        functools.partial(_outproj_kernel, final=final),
        grid=(m // tm,),
        in_specs=[row(d), row(D_A), row(D_B), row(D_C), row(PLE_DIM),
                  _full(wo.shape), _full(wple.shape), _full(wg.shape), _full((1, d))],
        out_specs=row(d),
        out_shape=jax.ShapeDtypeStruct((m, d), F32),
        compiler_params=_cparams(("arbitrary",)),
        name="outproj",
    )(h2d, ya, yb, yc, p2d, wo, wple, wg, final_g.reshape(1, d))


def _rwkv_prep(um, w0, w2, a0, a2, k_k, k_a):
    r = um[:, 0:D_A]
    k = um[:, D_A:2 * D_A]
    v = um[:, 2 * D_A:3 * D_A]
    w_lo = um[:, 3 * D_A:3 * D_A + W_LORA]
    a_lo = um[:, 3 * D_A + W_LORA:A_IN]
    wl = w0 + _mm(jnp.tanh(w_lo), w2)
    neg = -wl
    softplus = jnp.maximum(neg, 0.0) + jnp.log(1.0 + jnp.exp(-jnp.abs(neg)))
    w = -softplus - 0.5
    log_decay = -jnp.exp(w)
    a = _sigmoid(a0 + _mm(a_lo, a2))
    kk = k * k_k
    k = k * (1.0 + (a - 1.0) * k_a)
    return r, k, v, kk, a, log_decay


def _rwkv_head_out(y, r_h, k_h, v_h, rk_h, gng_h, gnb_h, gate_h):
    mu = jnp.mean(y, axis=-1, keepdims=True)
    var = jnp.mean(jnp.square(y - mu), axis=-1, keepdims=True)
    yn = (y - mu) * lax.rsqrt(var + GN_EPS) * gng_h + gnb_h
    bonus = jnp.sum(r_h * k_h * rk_h, axis=-1, keepdims=True) * v_h
    return (yn + bonus) * _silu(gate_h)


def _rwkv_chunk_kernel(u_ref, uprev0_ref, ga_ref, s0_ref, ones_ref, mu_ref, w0_ref, w2_ref, a0_ref, a2_ref,
                       kk_ref, ka_ref, rk_ref, gng_ref, gnb_ref, y_ref, s_ref, prev_ref):
    step = pl.program_id(1)
    R = u_ref.shape[0]
    C = RWKV_CHUNK
    subs = range(R // C)

    @pl.when(step == 0)
    def _():
        prev_ref[...] = uprev0_ref[...]
        s_ref[...] = s0_ref[...]

    u = u_ref[...]
    row = lax.broadcasted_iota(jnp.int32, (R, 1), 0)
    u_prev = jnp.where(row == 0, prev_ref[...], pltpu.roll(u, 1, axis=0))
    prev_ref[...] = u[R - 1:R, :]
    um = u + mu_ref[...] * (u_prev - u)
    r, k, v, kk, a, log_decay = _rwkv_prep(um, w0_ref[...], w2_ref[...], a0_ref[...], a2_ref[...],
                                           kk_ref[...], ka_ref[...])

    ri = lax.broadcasted_iota(jnp.int32, (R, R), 0)
    rj = lax.broadcasted_iota(jnp.int32, (R, R), 1)
    same_chunk = (ri // C) == (rj // C)
    cs = _dot_split_rhs((same_chunk & (rj <= ri)).astype(BF16), log_decay)
    cs_last = [cs[(sb + 1) * C - 1:(sb + 1) * C, :] for sb in subs]
    cs_end = jnp.concatenate([jnp.broadcast_to(x, (C, x.shape[1])) for x in cs_last], axis=0)
    p_end = [jnp.exp(x) for x in cs_last]
    e_inv = jnp.exp(-cs)
    e_rem = jnp.exp(cs_end - cs)
    kk_n = kk / jnp.maximum(jnp.sqrt(_dot_split_lhs(kk * kk, ones_ref[...])), 1e-12)
    b_f = kk_n * a
    a_t = (-kk_n * jnp.exp(cs - log_decay)).astype(BF16)
    r_t = (r * jnp.exp(cs)).astype(BF16)
    b_t = (b_f * e_inv).astype(BF16)
    k_t = (k * e_inv).astype(BF16)
    b_end = (b_f * e_rem).astype(BF16)
    k_end = (k * e_rem).astype(BF16)
    v_b = v.astype(BF16)
    ga = ga_ref[...]

    t2 = lax.broadcasted_iota(jnp.int32, (C, 2 * C), 0)
    j2 = lax.broadcasted_iota(jnp.int32, (C, 2 * C), 1)
    j2 = jnp.where(j2 >= C, j2 - C, j2)
    strict2 = j2 < t2
    incl2 = j2 <= t2
    ti = lax.broadcasted_iota(jnp.int32, (C, C), 0)
    tj = lax.broadcasted_iota(jnp.int32, (C, C), 1)
    eye = (ti == tj).astype(F32)
    zeros = jnp.zeros((C, HEAD_A), BF16)

    units = [(sb, h) for sb in subs for h in range(H_A)]
    blk = lambda x, sb, h: x[sb * C:(sb + 1) * C, h * HEAD_A:(h + 1) * HEAD_A]
    a_h = {un: blk(a_t, *un) for un in units}
    r_h = {un: blk(r_t, *un) for un in units}
    v_h = {un: blk(v_b, *un) for un in units}
    gram = {un: _dot_nt(jnp.concatenate([a_h[un], r_h[un]], axis=0),
                        jnp.concatenate([blk(b_t, *un), blk(k_t, *un)], axis=0)) for un in units}
    l_top = {un: jnp.where(strict2, gram[un][0:C], 0.0) for un in units}
    m_bot = {un: jnp.where(incl2, gram[un][C:], 0.0).astype(BF16) for un in units}
    lv = {un: _dot(l_top[un].astype(BF16), jnp.concatenate([zeros, v_h[un]], axis=0)) for un in units}

    l_ab = {un: l_top[un][:, 0:C] for un in units}
    inv = {un: eye + l_ab[un] for un in units}
    pw = {un: l_ab[un].astype(BF16) for un in units}
    for _ in range(int(math.log2(C)) - 1):
        pw = {un: _mm(pw[un], pw[un]).astype(BF16) for un in units}
        inv = {un: inv[un] + _mm(inv[un], pw[un]) for un in units}
    inv = {un: inv[un].astype(BF16) for un in units}
    w_mat = {un: _mm(inv[un], a_h[un]).astype(BF16) for un in units}
    u_v = {un: _mm(inv[un], lv[un]) for un in units}

    state = [s_ref[h] for h in range(H_A)]
    for sb in subs:
        heads = [(sb, h) for h in range(H_A)]
        state_b = [x.astype(BF16) for x in state]
        uv = [jnp.concatenate([(_dot_nt(w_mat[un], state_b[un[1]]) + u_v[un]).astype(BF16), v_h[un]], axis=0)
              for un in heads]
        y = [_dot_nt(r_h[un], state_b[un[1]]) + _dot(m_bot[un], uv[un[1]]) for un in heads]
        state = [state[h] * p_end[sb][:, h * HEAD_A:(h + 1) * HEAD_A]
                 + _dot_tn(uv[h], jnp.concatenate([blk(b_end, sb, h), blk(k_end, sb, h)], axis=0))
                 for h in range(H_A)]
        rows = slice(sb * C, (sb + 1) * C)
        for h in range(H_A):
            hs = slice(h * HEAD_A, (h + 1) * HEAD_A)
            out = _rwkv_head_out(y[h], r[rows, hs], k[rows, hs], v[rows, hs], rk_ref[:, hs], gng_ref[:, hs],
                                 gnb_ref[:, hs], ga[rows, hs])
            y_ref[rows, hs] = out.astype(y_ref.dtype)
    for h in range(H_A):
        s_ref[h] = state[h]


def _rwkv_chunked(u, uprev0, ga, s0, params):
    b, t, _ = u.shape
    c = RWKV_BLOCK
    assert t % c == 0
    head_of_lane = jnp.arange(D_A, dtype=jnp.int32) // HEAD_A
    head_ones = (head_of_lane[:, None] == head_of_lane[None, :]).astype(BF16)
    tok = lambda w: pl.BlockSpec((None, c, w), lambda i, j: (i, j, 0))
    state = pl.BlockSpec((None, H_A, HEAD_A, HEAD_A), lambda i, j: (i, 0, 0, 0))
    return pl.pallas_call(
        _rwkv_chunk_kernel,
        grid=(b, t // c),
        in_specs=[tok(A_IN), pl.BlockSpec((None, 1, A_IN), lambda i, j: (i, 0, 0)), tok(D_A), state,
                  _full(head_ones.shape)] + [_full(p.shape) for p in params],
        out_specs=[tok(D_A), state],
        out_shape=[jax.ShapeDtypeStruct((b, t, D_A), BF16), jax.ShapeDtypeStruct(s0.shape, F32)],
        scratch_shapes=[pltpu.VMEM((1, A_IN), F32)],
        compiler_params=_cparams(("parallel", "arbitrary")),
        name="rwkv_chunked",
    )(u, uprev0, ga, s0, head_ones, *params)


def _rwkv_step_kernel(r_ref, k_ref, v_ref, lo_ref, rp_ref, kp_ref, vp_ref, lop_ref, ga_ref, s0_ref,
                      mur_ref, muk_ref, muv_ref, mulo_ref, w0_ref, w2t_ref, a0_ref, a2t_ref, kk_ref, ka_ref,
                      rk_ref, gng_ref, gnb_ref, y_ref, s_ref, y_scr):
    mix = lambda x_ref, p_ref, mu_ref: x_ref[...] + mu_ref[...] * (p_ref[...] - x_ref[...])
    r = mix(r_ref, rp_ref, mur_ref)
    k = mix(k_ref, kp_ref, muk_ref)
    v = mix(v_ref, vp_ref, muv_ref)
    lo = mix(lo_ref, lop_ref, mulo_ref)
    neg = -(w0_ref[...] + _mm(w2t_ref[...], jnp.tanh(lo[0:W_LORA])))
    w = -(jnp.maximum(neg, 0.0) + jnp.log(1.0 + jnp.exp(-jnp.abs(neg)))) - 0.5
    decay = jnp.exp(-jnp.exp(w))
    a = _sigmoid(a0_ref[...] + _mm(a2t_ref[...], lo[W_LORA:]))
    kk = k * kk_ref[...]
    kk = kk / jnp.maximum(jnp.sqrt(jnp.sum(kk * kk, axis=0, keepdims=True)), 1e-12)
    k = k * (1.0 + (a - 1.0) * ka_ref[...])
    b = kk * a
    nkk = -kk
    for i in range(HEAD_A):
        s = s0_ref[i]
        sa = jnp.sum(s * nkk, axis=0, keepdims=True)
        s_new = s * decay + sa * b + v[i:i + 1, :] * k
        s_ref[i] = s_new
        y_scr[i:i + 1, :] = jnp.sum(s_new * r, axis=0, keepdims=True)
    y = y_scr[...]
    mu = jnp.mean(y, axis=0, keepdims=True)
    var = jnp.mean(jnp.square(y - mu), axis=0, keepdims=True)
    yn = (y - mu) * lax.rsqrt(var + GN_EPS) * gng_ref[...] + gnb_ref[...]
    bonus = jnp.sum(r * k * rk_ref[...], axis=0, keepdims=True) * v
    y_ref[...] = ((yn + bonus) * _silu(ga_ref[...])).astype(y_ref.dtype)


def _rwkv_step(u_t, uprev_t, ga_t, s0_t, params_t):
    nb = u_t.shape[1]
    mu, w0, w2t, a0, a2t, k_k, k_a, r_k, gn_g, gn_b = params_t
    n_head_blocks = D_A // HEAD_A
    lora_block = 3 * D_A // (W_LORA + A_LORA)
    feat = lambda off: pl.BlockSpec((HEAD_A, nb), lambda h: (off * n_head_blocks + h, 0))
    lora = pl.BlockSpec((W_LORA + A_LORA, nb), lambda h: (lora_block, 0))
    col = lambda off: pl.BlockSpec((HEAD_A, 1), lambda h: (off * n_head_blocks + h, 0))
    lora_col = pl.BlockSpec((W_LORA + A_LORA, 1), lambda h: (lora_block, 0))
    head_rows = lambda w: pl.BlockSpec((HEAD_A, w), lambda h: (h, 0))
    state = pl.BlockSpec((None, HEAD_A, HEAD_A, nb), lambda h: (h, 0, 0, 0))
    return pl.pallas_call(
        _rwkv_step_kernel,
        grid=(H_A,),
        in_specs=[feat(0), feat(1), feat(2), lora, feat(0), feat(1), feat(2), lora, head_rows(nb), state,
                  col(0), col(1), col(2), lora_col, col(0), head_rows(W_LORA), col(0), head_rows(A_LORA),
                  col(0), col(0), col(0), col(0), col(0)],
        out_specs=[head_rows(nb), state],
        out_shape=[jax.ShapeDtypeStruct((D_A, nb), BF16), jax.ShapeDtypeStruct(s0_t.shape, F32)],
        scratch_shapes=[pltpu.VMEM((HEAD_A, nb), F32)],
        compiler_params=_cparams(("parallel",)),
        name="rwkv_step",
    )(u_t, u_t, u_t, u_t, uprev_t, uprev_t, uprev_t, uprev_t, ga_t, s0_t,
      mu, mu, mu, mu, w0, w2t, a0, a2t, k_k, k_a, r_k, gn_g, gn_b)


def _mla_prep_kernel(cq_ref, ckv_ref, cos_ref, sin_ref, qg_ref, wuq_ref, kvg_ref, wukt_ref, q_ref, rows_ref):
    cos2 = cos_ref[...]
    sin2 = sin_ref[...]
    qn = _rms(cq_ref[...], qg_ref[...], EPS).astype(BF16)
    q = _dot(qn, wuq_ref[...])
    for h in range(H_B):
        qh = q[:, h * LANES:(h + 1) * LANES]
        q_lat = _dot(qh[:, :NOPE_B].astype(BF16), wukt_ref[h])
        q_rope = qh[:, NOPE_B:NOPE_B + ROPE_B] * cos2 + qh[:, NOPE_B + ROPE_B:] * sin2
        q_ref[h, :, 0:KV_RANK] = (q_lat * MLA_SCALE).astype(q_ref.dtype)
        q_ref[h, :, KV_RANK:MLA_W] = (q_rope * MLA_SCALE).astype(q_ref.dtype)
    ckv = ckv_ref[...]
    rows_ref[:, 0:KV_RANK] = _rms(ckv[:, 0:KV_RANK], kvg_ref[...], EPS)
    rows_ref[:, KV_RANK:MLA_W] = (ckv[:, KV_RANK:KV_RANK + ROPE_B] * cos2
                                  + ckv[:, KV_RANK + ROPE_B:KV_RANK + 2 * ROPE_B] * sin2)


def _mla_prep(cq, ckv, cos2, sin2, q_norm_g, wuq_ext, kv_norm_g, wuk_t, *, tm, pos_tiles):
    m = cq.shape[0]
    row = lambda w: pl.BlockSpec((tm, w), lambda i: (i, 0))
    pos = pl.BlockSpec((tm, ROPE_B), lambda i: (i % pos_tiles, 0))
    return pl.pallas_call(
        _mla_prep_kernel,
        grid=(m // tm,),
        in_specs=[row(Q_RANK), row(2 * LANES), pos, pos, _full((1, Q_RANK)), _full(wuq_ext.shape),
                  _full((1, KV_RANK)), _full(wuk_t.shape)],
        out_specs=[pl.BlockSpec((H_B, tm, MLA_W), lambda i: (0, i, 0)), row(MLA_W)],
        out_shape=[jax.ShapeDtypeStruct((H_B, m, MLA_W), BF16), jax.ShapeDtypeStruct((m, MLA_W), F32)],
        compiler_params=_cparams(("parallel",)),
        name="mla_prep",
    )(cq, ckv, cos2, sin2, q_norm_g.reshape(1, Q_RANK), wuq_ext, kv_norm_g.reshape(1, KV_RANK), wuk_t)


def _mla_out(o_lat, wuv_ref, gb, o_ref, rows_per_head):
    for h in range(H_B):
        o_h = _dot(o_lat[h * rows_per_head:(h + 1) * rows_per_head].astype(BF16), wuv_ref[h])
        hs = slice(h * DV_B, (h + 1) * DV_B)
        o_ref[:, hs] = (o_h * _silu(gb[:, hs])).astype(o_ref.dtype)


def _mla_prep_t_kernel(cq_ref, ckv_ref, cos_ref, sin_ref, qg_ref, wuqt_ref, kvg_ref, wuk_ref,
                       q_ref, rows_ref, rowst_ref):
    tm = cq_ref.shape[1]
    cos2 = cos_ref[...]
    sin2 = sin_ref[...]
    cq = cq_ref[...]
    qn = (cq * lax.rsqrt(jnp.mean(cq * cq, axis=0, keepdims=True) + EPS) * qg_ref[...]).astype(BF16)
    q = _dot(wuqt_ref[...], qn)
    pad = jnp.zeros((2 * LANES - MLA_W, tm), q_ref.dtype)
    for h in range(H_B):
        qh = q[h * LANES:(h + 1) * LANES]
        q_lat = _dot(wuk_ref[h], qh[0:NOPE_B].astype(BF16))
        q_rope = qh[NOPE_B:NOPE_B + ROPE_B] * cos2 + qh[NOPE_B + ROPE_B:] * sin2
        cols = slice(h * tm, (h + 1) * tm)
        q_ref[0:KV_RANK, cols] = (q_lat * MLA_SCALE).astype(q_ref.dtype)
        q_ref[KV_RANK:MLA_W, cols] = (q_rope * MLA_SCALE).astype(q_ref.dtype)
        q_ref[MLA_W:, cols] = pad
    ckv = ckv_ref[...]
    c = ckv[0:KV_RANK]
    cn = c * lax.rsqrt(jnp.mean(c * c, axis=0, keepdims=True) + EPS) * kvg_ref[...]
    kr = ckv[KV_RANK:KV_RANK + ROPE_B] * cos2 + ckv[KV_RANK + ROPE_B:KV_RANK + 2 * ROPE_B] * sin2
    rowst_ref[0:KV_RANK, :] = cn
    rowst_ref[KV_RANK:MLA_W, :] = kr
    rows_t = jnp.concatenate([cn, kr, jnp.zeros((2 * LANES - MLA_W, tm), F32)], axis=0)
    rows_ref[...] = rows_t.T


def _mla_prep_t(cq_t, ckv_t, cos2_t, sin2_t, q_norm_g, wuq_ext_t, kv_norm_g, wuk, *, tm):
    b, _, t = cq_t.shape
    nt = t // tm
    blk = lambda w: pl.BlockSpec((None, w, tm), lambda bi, i: (bi, 0, i))
    pos = pl.BlockSpec((ROPE_B, tm), lambda bi, i: (0, i))
    return pl.pallas_call(
        _mla_prep_t_kernel,
        grid=(b, nt),
        in_specs=[blk(Q_RANK), blk(2 * LANES), pos, pos, _full((Q_RANK, 1)), _full(wuq_ext_t.shape),
                  _full((KV_RANK, 1)), _full(wuk.shape)],
        out_specs=[pl.BlockSpec((None, 2 * LANES, H_B * tm), lambda bi, i: (bi * nt + i, 0, 0)),
                   pl.BlockSpec((None, tm, 2 * LANES), lambda bi, i: (bi, i, 0)),
                   blk(MLA_W)],
        out_shape=[jax.ShapeDtypeStruct((b * nt, 2 * LANES, H_B * tm), BF16),
                   jax.ShapeDtypeStruct((b, t, 2 * LANES), F32),
                   jax.ShapeDtypeStruct((b, MLA_W, t), F32)],
        compiler_params=_cparams(("parallel", "parallel")),
        name="mla_prep_t",
    )(cq_t, ckv_t, cos2_t, sin2_t, q_norm_g.reshape(Q_RANK, 1), wuq_ext_t, kv_norm_g.reshape(KV_RANK, 1), wuk)


def _softmax_update_t(s, m_ref, l_ref, cols):
    m_old = m_ref[:, cols]
    m_new = jnp.maximum(m_old, jnp.max(s, axis=0, keepdims=True))
    alpha = jnp.exp(m_old - m_new)
    p = jnp.exp(s - m_new)
    l_ref[:, cols] = alpha * l_ref[:, cols] + jnp.sum(p, axis=0, keepdims=True)
    m_ref[:, cols] = m_new
    return alpha, p


def _bias_kernel(rb_ref, dist_ref, o_ref):
    dist = dist_ref[...]
    n = jnp.maximum(dist, 0)
    max_exact = NUM_BUCKETS // 2
    n_safe = jnp.maximum(n, max_exact).astype(F32)
    large = max_exact + (jnp.log(n_safe / max_exact) / math.log(MAX_DISTANCE / max_exact)
                         * (NUM_BUCKETS - max_exact)).astype(jnp.int32)
    large = jnp.minimum(large, NUM_BUCKETS - 1)
    bucket = jnp.where(n < max_exact, n, large)
    for h in range(H_C):
        bias = jnp.zeros(dist.shape, F32)
        for kb in range(NUM_BUCKETS):
            bias = jnp.where(bucket == kb, rb_ref[kb * H_C + h], bias)
        o_ref[h] = jnp.where(dist >= 0, bias, NEG_INF)


def _bias_tiles(rel_bias, dist):
    g, r, c = dist.shape
    grid_spec = pltpu.PrefetchScalarGridSpec(
        num_scalar_prefetch=1,
        grid=(g,),
        in_specs=[pl.BlockSpec((None, r, c), lambda i, rb: (i, 0, 0))],
        out_specs=pl.BlockSpec((None, H_C, r, c), lambda i, rb: (i, 0, 0, 0)),
    )
    return pl.pallas_call(
        _bias_kernel,
        grid_spec=grid_spec,
        out_shape=jax.ShapeDtypeStruct((g, H_C, r, c), F32),
        compiler_params=_cparams(("arbitrary",)),
        name="rel_bias_tiles",
    )(rel_bias.reshape(-1), dist)


def _diff_lambda(lam_ref, lam_init):
    lam = lam_ref[...]
    e1 = jnp.exp(jnp.sum(lam[0:1] * lam[1:2], axis=-1, keepdims=True))
    e2 = jnp.exp(jnp.sum(lam[2:3] * lam[3:4], axis=-1, keepdims=True))
    return e1 - e2 + lam_init


def _diff_queries(qc):
    lane = lax.broadcasted_iota(jnp.int32, qc.shape, 1)
    qs = qc * DIFF_SCALE
    groups = []
    for h in range(H_C):
        for c in range(2):
            lo = h * DV_C + c * DC
            groups.append(jnp.where((lane >= lo) & (lane < lo + DC), qs, 0.0))
    return jnp.concatenate(groups, axis=0)


def _diff_out(acc, l, lam, lam_init, sg, gc, o_ref, rows):
    for h in range(H_C):
        hs = slice(h * DV_C, (h + 1) * DV_C)
        r1 = slice((2 * h) * rows, (2 * h + 1) * rows)
        r2 = slice((2 * h + 1) * rows, (2 * h + 2) * rows)
        o = acc[r1, hs] / l[r1] - lam * (acc[r2, hs] / l[r2])
        o = _rms(o, sg, SUBLN_EPS) * (1.0 - lam_init)
        o_ref[:, hs] = (o * _silu(gc[:, hs])).astype(o_ref.dtype)


def _prompt_attn_kernel(pi_ref, pj_ref,
                        q_ref, k_ref, ct_ref, mask_ref, gb_ref, wuvt_ref,
                        qc_ref, kc_ref, vt_ref, gc_ref, bias_ref, lam_ref, sg_ref,
                        ob_ref, oc_ref,
                        m_ref, l_ref, acc_ref, qbd_ref, md_ref, ld_ref, accd_ref, *, lam_init):
    step = pl.program_id(1)
    i = pi_ref[step]
    j = pj_ref[step]
    tq = ob_ref.shape[0]
    nblk = 2 * H_C

    @pl.when(j == 0)
    def _():
        m_ref[...] = jnp.full_like(m_ref, NEG_INF)
        l_ref[...] = jnp.zeros_like(l_ref)
        acc_ref[...] = jnp.zeros_like(acc_ref)
        q = qc_ref[...] * DIFF_SCALE
        feat = lax.broadcasted_iota(jnp.int32, q.shape, 0)
        for blk in range(nblk):
            lo = (blk // 2) * DV_C + (blk % 2) * DC
            qbd_ref[:, blk * tq:(blk + 1) * tq] = jnp.where((feat >= lo) & (feat < lo + DC), q, 0.0).astype(BF16)
        md_ref[...] = jnp.full_like(md_ref, NEG_INF)
        ld_ref[...] = jnp.zeros_like(ld_ref)
        accd_ref[...] = jnp.zeros_like(accd_ref)

    k = k_ref[...].astype(BF16)
    ct = ct_ref[...].astype(BF16)
    mask = mask_ref[jnp.minimum(i - j, 1)]
    heads = range(H_B)
    cols = [slice(h * tq, (h + 1) * tq) for h in heads]
    kc = kc_ref[...].astype(BF16)
    vt = vt_ref[...].astype(BF16)
    tile = jnp.minimum(i - j, 2)
    blocks = range(nblk)
    dcols = [slice(blk * tq, (blk + 1) * tq) for blk in blocks]
    s_b = [_dot(k, q_ref[:, cols[h]]) + mask for h in heads]
    s_d = [_dot(kc, qbd_ref[:, dcols[blk]]) + bias_ref[tile, blk // 2] for blk in blocks]
    ap_b = [_softmax_update_t(s_b[h], m_ref, l_ref, cols[h]) for h in heads]
    ap_d = [_softmax_update_t(s_d[blk], md_ref, ld_ref, dcols[blk]) for blk in blocks]
    pv_b = [_dot(ct, ap_b[h][1].astype(BF16)) for h in heads]
    pv_d = [_dot(vt[(blk // 2) * DV_C:(blk // 2 + 1) * DV_C], ap_d[blk][1].astype(BF16)) for blk in blocks]
    for h in heads:
        acc_ref[:, cols[h]] = ap_b[h][0] * acc_ref[:, cols[h]] + pv_b[h]
    for blk in blocks:
        accd_ref[blk] = ap_d[blk][0] * accd_ref[blk] + pv_d[blk]

    @pl.when(j == i)
    def _():
        outs = []
        for h in heads:
            o_lat = (acc_ref[:, cols[h]] / l_ref[:, cols[h]]).astype(BF16)
            outs.append(_dot(wuvt_ref[h], o_lat))
        o = jnp.concatenate(outs, axis=0).T
        ob_ref[...] = (o * _silu(gb_ref[...])).astype(ob_ref.dtype)
        lam = _diff_lambda(lam_ref, lam_init)
        outs = []
        for h in range(H_C):
            o = (accd_ref[2 * h] / ld_ref[:, dcols[2 * h]]
                 - lam * (accd_ref[2 * h + 1] / ld_ref[:, dcols[2 * h + 1]]))
            o = o * lax.rsqrt(jnp.mean(o * o, axis=0, keepdims=True) + SUBLN_EPS) * sg_ref[...]
            outs.append(o * (1.0 - lam_init))
        o = jnp.concatenate(outs, axis=0).T
        oc_ref[...] = (o * _silu(gc_ref[...])).astype(oc_ref.dtype)


def _prompt_attn(q_t, rows_pad, rows_t, mask, gb, wuv_t, qc_t, kc, vc_t, gc, bias_tiles, lam_vecs, subln_g, lam_init):
    b, t, _ = kc.shape
    tq = ATTN_TILE
    nq = t // tq
    pairs = [(i, j) for i in range(nq) for j in range(i + 1)]
    pair_i = jnp.asarray([p[0] for p in pairs], jnp.int32)
    pair_j = jnp.asarray([p[1] for p in pairs], jnp.int32)
    nblk = 2 * H_C
    q_tile = lambda w: pl.BlockSpec((None, tq, w), lambda bi, s, pi, pj: (bi, pi[s], 0))
    k_tile = lambda w: pl.BlockSpec((None, tq, w), lambda bi, s, pi, pj: (bi, pj[s], 0))
    qt_tile = lambda w: pl.BlockSpec((None, w, tq), lambda bi, s, pi, pj: (bi, 0, pi[s]))
    kt_tile = lambda w: pl.BlockSpec((None, w, tq), lambda bi, s, pi, pj: (bi, 0, pj[s]))
    const = lambda shape: pl.BlockSpec(shape, lambda bi, s, pi, pj: (0,) * len(shape))
    grid_spec = pltpu.PrefetchScalarGridSpec(
        num_scalar_prefetch=2,
        grid=(b, len(pairs)),
        in_specs=[pl.BlockSpec((None, 2 * LANES, H_B * tq), lambda bi, s, pi, pj: (bi * nq + pi[s], 0, 0)),
                  k_tile(2 * LANES), kt_tile(KV_RANK), const(mask.shape), q_tile(D_B), const(wuv_t.shape),
                  qt_tile(D_C), k_tile(D_C), kt_tile(D_C), q_tile(D_C), const(bias_tiles.shape),
                  const(lam_vecs.shape), const((DV_C, 1))],
        out_specs=[q_tile(D_B), q_tile(D_C)],
        scratch_shapes=[pltpu.VMEM((1, H_B * tq), F32), pltpu.VMEM((1, H_B * tq), F32),
                        pltpu.VMEM((KV_RANK, H_B * tq), F32),
                        pltpu.VMEM((D_C, nblk * tq), BF16), pltpu.VMEM((1, nblk * tq), F32),
                        pltpu.VMEM((1, nblk * tq), F32), pltpu.VMEM((nblk, DV_C, tq), F32)],
    )
    return pl.pallas_call(
        functools.partial(_prompt_attn_kernel, lam_init=lam_init),
        grid_spec=grid_spec,
        out_shape=[jax.ShapeDtypeStruct((b, t, D_B), BF16), jax.ShapeDtypeStruct((b, t, D_C), BF16)],
        compiler_params=_cparams(("parallel", "arbitrary")),
        name="prompt_attn",
    )(pair_i, pair_j, q_t, rows_pad, rows_t, mask, gb, wuv_t, qc_t, kc, vc_t, gc, bias_tiles, lam_vecs,
      subln_g.reshape(DV_C, 1))


def _softmax_step(s, m, l):
    m_new = jnp.maximum(m, jnp.max(s, axis=-1, keepdims=True))
    alpha = jnp.exp(m - m_new)
    p = jnp.exp(s - m_new)
    return m_new, alpha * l + jnp.sum(p, axis=-1, keepdims=True), alpha, p


def _decode_attn_kernel(pt_ref, q_ref, row_ref, gb_ref, wuv_ref, qc_ref, kn_ref, vn_ref, gc_ref, bias_ref, lam_ref,
                        sg_ref, cm_hbm, ck_hbm, cv_hbm, ob_ref, oc_ref, mbuf, kbuf, vbuf, sems,
                        *, layer, pages_per_seq, lam_init):
    n_pages = DECODE_PAGES_PER_CHUNK
    seq = pl.program_id(0)
    chunks = pages_per_seq // n_pages
    total = pl.num_programs(0) * chunks

    def chunk_copies(g, slot):
        out = []
        for p in range(n_pages):
            page = pt_ref[g * n_pages + p]
            out.append(pltpu.make_async_copy(cm_hbm.at[layer, page], mbuf.at[slot, p], sems.at[slot, 0]))
            out.append(pltpu.make_async_copy(ck_hbm.at[layer, page], kbuf.at[slot, p], sems.at[slot, 1]))
            out.append(pltpu.make_async_copy(cv_hbm.at[layer, page], vbuf.at[slot, p], sems.at[slot, 2]))
        return out

    @pl.when(seq == 0)
    def _():
        for cp in chunk_copies(0, 0):
            cp.start()

    q = q_ref[...]
    qbd = _diff_queries(qc_ref[...])
    qb = qbd.astype(BF16)

    def chunk_body(c, carry):
        mb, lb, accb, md, ld, accd = carry
        g = seq * chunks + c
        slot = lax.rem(g, 2)

        @pl.when(g + 1 < total)
        def _():
            for cp in chunk_copies(g + 1, 1 - slot):
                cp.start()

        for cp in chunk_copies(g, slot):
            cp.wait()

        pages = range(n_pages)
        lanes = [slice(p * PAGE_SIZE, (p + 1) * PAGE_SIZE) for p in pages]
        ks = [mbuf[slot, p].astype(BF16) for p in pages]
        s_b = jnp.concatenate([_dot(q, ks[p]) for p in pages], axis=1)
        s_d = jnp.concatenate([_dot(qb, kbuf[slot, p].astype(BF16)) for p in pages], axis=1)
        s_d = s_d + bias_ref[jnp.where(c == chunks - 1, 1, 0)]
        mb, lb, alpha_b, p_b = _softmax_step(s_b, mb, lb)
        md, ld, alpha_d, p_d = _softmax_step(s_d, md, ld)
        p_b = p_b.astype(BF16)
        p_d = p_d.astype(BF16)
        pv_b = [_dot_nt(p_b[:, lanes[p]], ks[p][0:KV_RANK, :]) for p in pages]
        pv_d = [_dot_nt(p_d[:, lanes[p]], vbuf[slot, p].astype(BF16)) for p in pages]
        accb = alpha_b * accb + functools.reduce(lambda x, y: x + y, pv_b)
        accd = alpha_d * accd + functools.reduce(lambda x, y: x + y, pv_d)
        return mb, lb, accb, md, ld, accd

    init = (jnp.full((H_B, 1), NEG_INF, F32), jnp.zeros((H_B, 1), F32), jnp.zeros((H_B, KV_RANK), F32),
            jnp.full((2 * H_C, 1), NEG_INF, F32), jnp.zeros((2 * H_C, 1), F32), jnp.zeros((2 * H_C, D_C), F32))
    mb, lb, accb, md, ld, accd = lax.fori_loop(0, chunks, chunk_body, init)

    row = row_ref[...]
    s_new = jnp.sum(q.astype(F32) * row, axis=-1, keepdims=True)
    mb, lb, alpha, p_new = _softmax_step(s_new, mb, lb)
    accb = alpha * accb + p_new * row[:, 0:KV_RANK]
    _mla_out(accb / lb, wuv_ref, gb_ref[...], ob_ref, 1)

    s_new = jnp.sum(qbd * kn_ref[...], axis=-1, keepdims=True) + bias_ref[2][:, 0:1]
    md, ld, alpha, p_new = _softmax_step(s_new, md, ld)
    accd = alpha * accd + p_new * vn_ref[...]
    _diff_out(accd, ld, _diff_lambda(lam_ref, lam_init), lam_init, sg_ref[...], gc_ref[...], oc_ref, 1)


def _decode_attn(page_table_flat, q, rows_new, gb, wuv, qc, kc, vc, gc, bias_rows, lam_vecs, subln_g,
                 cache_mla, cache_k, cache_v, layer, pages_per_seq, lam_init):
    b = q.shape[0]
    n_pages = DECODE_PAGES_PER_CHUNK
    assert pages_per_seq % n_pages == 0
    per_b = lambda shape: pl.BlockSpec((None,) + shape, lambda bi, pt: (bi,) + (0,) * len(shape))
    const = lambda shape: pl.BlockSpec(shape, lambda bi, pt: (0,) * len(shape))
    hbm = pl.BlockSpec(memory_space=pl.ANY)
    grid_spec = pltpu.PrefetchScalarGridSpec(
        num_scalar_prefetch=1,
        grid=(b,),
        in_specs=[per_b((H_B, MLA_W)), per_b((1, MLA_W)), per_b((1, D_B)), const(wuv.shape),
                  per_b((1, D_C)), per_b((1, D_C)), per_b((1, D_C)), per_b((1, D_C)), const(bias_rows.shape),
                  const(lam_vecs.shape), const((1, DV_C)), hbm, hbm, hbm],
        out_specs=[per_b((1, D_B)), per_b((1, D_C))],
        scratch_shapes=[pltpu.VMEM((2, n_pages, MLA_W, PAGE_SIZE), F32),
                        pltpu.VMEM((2, n_pages, D_C, PAGE_SIZE), F32),
                        pltpu.VMEM((2, n_pages, D_C, PAGE_SIZE), F32),
                        pltpu.SemaphoreType.DMA((2, 3))],
    )
    return pl.pallas_call(
        functools.partial(_decode_attn_kernel, layer=layer, pages_per_seq=pages_per_seq, lam_init=lam_init),
        grid_spec=grid_spec,
        out_shape=[jax.ShapeDtypeStruct((b, 1, D_B), BF16), jax.ShapeDtypeStruct((b, 1, D_C), BF16)],
        compiler_params=_cparams(("arbitrary",)),
        name="decode_attn",
    )(page_table_flat, q, rows_new, gb, wuv, qc, kc, vc, gc, bias_rows, lam_vecs, subln_g.reshape(1, DV_C),
      cache_mla, cache_k, cache_v)


def _permute_w_in(w):
    o_ckv = A_IN + D_A + Q_RANK
    o_kr = o_ckv + KV_RANK
    o_gb = o_kr + ROPE_B
    half = ROPE_B // 2
    pad = jnp.zeros((w.shape[0], 2 * LANES - KV_RANK - 2 * ROPE_B), w.dtype)
    out = jnp.concatenate([w[:, :o_gb], w[:, o_kr + half:o_gb], w[:, o_kr:o_kr + half], pad, w[:, o_gb:]], axis=1)
    assert out.shape[1] == IN_COLS_PERM
    return out.astype(BF16)


def _extend_w_uq(w):
    w = w.reshape(Q_RANK, H_B, NOPE_B + ROPE_B)
    half = ROPE_B // 2
    rope = w[:, :, NOPE_B:]
    swapped = jnp.concatenate([rope[:, :, half:], rope[:, :, :half]], axis=-1)
    return jnp.concatenate([w, swapped], axis=-1).reshape(Q_RANK, H_B * LANES).astype(BF16)


def _rope_tables(pos):
    inv = ROPE_THETA ** (-jnp.arange(0, ROPE_B, 2, dtype=F32) / ROPE_B)
    ang = pos.astype(F32)[:, None] * inv[None, :]
    cos, sin = jnp.cos(ang), jnp.sin(ang)
    return jnp.concatenate([cos, cos], axis=-1), jnp.concatenate([-sin, sin], axis=-1)


def _layer_weights(l, W):
    row = lambda a: a.reshape(1, -1)
    rwkv = (row(W["mu_shift"][l]), row(W["rw_w0"][l]), W["rw_w2"][l], row(W["rw_a0"][l]), W["rw_a2"][l],
            row(W["rw_k_k"][l]), row(W["rw_k_a"][l]), row(W["rw_r_k"][l]), row(W["rw_gn_g"][l]),
            row(W["rw_gn_b"][l]))
    col = lambda a: a.reshape(-1, 1)
    rwkv_t = (col(W["mu_shift"][l]), col(W["rw_w0"][l]), W["rw_w2"][l].T, col(W["rw_a0"][l]), W["rw_a2"][l].T,
              col(W["rw_k_k"][l]), col(W["rw_k_a"][l]), col(W["rw_r_k"][l]), col(W["rw_gn_g"][l]),
              col(W["rw_gn_b"][l]))
    w_in = _permute_w_in(W["w_in"][l])
    wuq = _extend_w_uq(W["mla_w_uq"][l])
    return dict(
        w_in_n=jnp.concatenate([w_in[:, a:b] for a, b in PROMPT_SEGS_N], axis=1),
        w_in_t=jnp.concatenate([w_in[:, a:b] for a, b in PROMPT_SEGS_T], axis=1).T,
        w_in_sn=jnp.concatenate([w_in[:, a:b] for a, b in SAMPLE_SEGS_N], axis=1),
        w_in_st=jnp.concatenate([w_in[:, a:b] for a, b in SAMPLE_SEGS_T], axis=1).T,
        rwkv=rwkv,
        rwkv_t=rwkv_t,
        wuq=wuq,
        wuq_t=wuq.T,
        wuk=jnp.transpose(W["mla_w_uk"][l], (1, 0, 2)).astype(BF16),
        wuv_t=jnp.transpose(W["mla_w_uv"][l], (1, 2, 0)).astype(BF16),
        wuk_t=jnp.transpose(W["mla_w_uk"][l], (1, 2, 0)).astype(BF16),
        wuv=jnp.transpose(W["mla_w_uv"][l], (1, 0, 2)).astype(BF16),
        lam_vecs=jnp.stack([W["diff_lam_q1"][l], W["diff_lam_k1"][l], W["diff_lam_q2"][l], W["diff_lam_k2"][l]]),
        wo=W["w_out"][l].astype(BF16),
        wple=W["w_ple"][l].astype(BF16),
        wg=W["w_ple_gate"][l].astype(BF16),
    )


def _run_prompt(x, p, W, LW, depth):
    b, t, d = x.shape
    m = b * t
    tm = ATTN_TILE
    cos2, sin2 = _rope_tables(jnp.arange(t, dtype=jnp.int32))
    cos2_t, sin2_t = cos2.T, sin2.T
    tile = jnp.arange(ATTN_TILE, dtype=jnp.int32)
    dist = (jnp.arange(3, dtype=jnp.int32)[:, None, None] * ATTN_TILE + tile[None, None, :] - tile[None, :, None])
    bias_tiles = _bias_tiles(W["rel_bias"], dist)
    causal = jnp.stack([jnp.where(dist[0] >= 0, 0.0, NEG_INF).astype(F32), jnp.zeros(dist.shape[1:], F32)])
    uprev0 = jnp.zeros((b, 1, A_IN), F32)
    s0 = jnp.zeros((b, H_A, HEAD_A, HEAD_A), F32)
    segs_n = _pack_segments(PROMPT_SEGS_N)
    segs_t = _pack_segments(PROMPT_SEGS_T)
    h = x.reshape(m, d)
    mla_rows, k_rows, v_rows, wkv_out, shift_out = [], [], [], [], []
    for l in range(depth):
        lw = LW[l]
        u, ga, gb, gc, kc, cq_t, ckv_t, qc_t, kc_t, vc_t, xn_last = _inproj(
            h, W["norm_g"][l], lw["w_in_n"], segs_n, normalize=True, rows_per_seq=t, tm=tm,
            wt_bf16=lw["w_in_t"], segs_t=segs_t)
        y_a, s_new = _rwkv_chunked(u.reshape(b, t, A_IN), uprev0, ga.reshape(b, t, D_A), s0, lw["rwkv"])
        q_t, rows_pad, rows_t = _mla_prep_t(cq_t, ckv_t, cos2_t, sin2_t, W["mla_q_norm_g"][l], lw["wuq_t"],
                                            W["mla_kv_norm_g"][l], lw["wuk"], tm=tm)
        lam_init = 0.8 - 0.6 * math.exp(-0.3 * l)
        y_b, y_c = _prompt_attn(q_t, rows_pad, rows_t, causal, gb.reshape(b, t, D_B), lw["wuv_t"],
                                qc_t, kc.reshape(b, t, D_C), vc_t, gc.reshape(b, t, D_C), bias_tiles,
                                lw["lam_vecs"], W["diff_subln_g"][l], lam_init)
        h = _outproj(h, y_a.reshape(m, D_A), y_b.reshape(m, D_B), y_c.reshape(m, D_C), p[l].reshape(m, PLE_DIM),
                     lw["wo"], lw["wple"], lw["wg"], W["final_norm_g"], final=(l == depth - 1), tm=tm)
        mla_rows.append(jnp.transpose(rows_t, (0, 2, 1)))
        k_rows.append(jnp.transpose(kc_t.reshape(b, H_C, 2 * DC, t), (0, 3, 1, 2)))
        v_rows.append(jnp.transpose(vc_t.reshape(b, H_C, DV_C, t), (0, 3, 1, 2)))
        wkv_out.append(s_new)
        shift_out.append(xn_last.reshape(b, d))
    return (h.reshape(b, t, d), jnp.stack(mla_rows), jnp.stack(k_rows), jnp.stack(v_rows), jnp.stack(wkv_out),
            jnp.stack(shift_out))


def _run_sample(x, p, state_shift, state_wkv, cache_mla, cache_k, cache_v, page_table, W, LW, depth):
    b, t, d = x.shape
    assert t == 1
    pages_per_seq = page_table.shape[1]
    past_len = pages_per_seq * PAGE_SIZE
    tm = b
    cos2, sin2 = _rope_tables(jnp.full((b,), past_len, dtype=jnp.int32))
    pt_flat = page_table.reshape(-1).astype(jnp.int32)
    step_keys = DECODE_PAGES_PER_CHUNK * PAGE_SIZE
    key_in_step = jnp.arange(step_keys, dtype=jnp.int32)
    dist = jnp.stack([past_len - key_in_step,
                      past_len - (past_len - step_keys + key_in_step),
                      jnp.zeros((step_keys,), jnp.int32)])
    dist = jnp.broadcast_to(dist[:, None, :], (3, 2, step_keys))
    bias = _bias_tiles(W["rel_bias"], dist)
    bias_rows = bias.reshape(3, 2 * H_C, step_keys)
    to_feature_major = lambda c: jnp.transpose(c, (0, 1, 3, 4, 2)).reshape(c.shape[:2] + (D_C, PAGE_SIZE))
    cache_k2 = to_feature_major(cache_k)
    cache_v2 = to_feature_major(cache_v)
    cache_mla_t = jnp.transpose(cache_mla, (0, 1, 3, 2))
    segs_n = _pack_segments(SAMPLE_SEGS_N)
    segs_t = _pack_segments(SAMPLE_SEGS_T)
    h = x.reshape(b, d)
    mla_rows, k_rows, v_rows, wkv_out, shift_out = [], [], [], [], []
    for l in range(depth):
        lw = LW[l]
        cq, ckv, gb, qc, kc, vc, gc, u_t, ga_t, xn = _inproj(
            h, W["norm_g"][l], lw["w_in_sn"], segs_n, normalize=True, rows_per_seq=1, tm=tm,
            wt_bf16=lw["w_in_st"], segs_t=segs_t)
        (uprev_t,) = _inproj(state_shift[l], W["norm_g"][l], None, (), normalize=False, rows_per_seq=1, tm=tm,
                             wt_bf16=lw["w_in_st"][:A_IN], segs_t=(SEG_U,))
        y_a_t, s_new_t = _rwkv_step(u_t[0], uprev_t[0], ga_t[0], jnp.transpose(state_wkv[l], (1, 2, 3, 0)),
                                    lw["rwkv_t"])
        y_a = y_a_t.T
        s_new = jnp.transpose(s_new_t, (3, 0, 1, 2))
        q, rows = _mla_prep(cq, ckv, cos2, sin2, W["mla_q_norm_g"][l], lw["wuq"], W["mla_kv_norm_g"][l],
                            lw["wuk_t"], tm=tm, pos_tiles=1)
        lam_init = 0.8 - 0.6 * math.exp(-0.3 * l)
        y_b, y_c = _decode_attn(pt_flat, jnp.transpose(q, (1, 0, 2)), rows.reshape(b, 1, MLA_W),
                                gb.reshape(b, 1, D_B), lw["wuv"], qc.reshape(b, 1, D_C), kc.reshape(b, 1, D_C),
                                vc.reshape(b, 1, D_C), gc.reshape(b, 1, D_C), bias_rows, lw["lam_vecs"],
                                W["diff_subln_g"][l], cache_mla_t, cache_k2, cache_v2, l, pages_per_seq, lam_init)
        h = _outproj(h, y_a, y_b.reshape(b, D_B), y_c.reshape(b, D_C), p[l].reshape(b, PLE_DIM),
                     lw["wo"], lw["wple"], lw["wg"], W["final_norm_g"], final=(l == depth - 1), tm=tm)
        mla_rows.append(rows.reshape(b, 1, MLA_W))
        k_rows.append(kc.reshape(b, 1, H_C, 2 * DC))
        v_rows.append(vc.reshape(b, 1, H_C, DV_C))
        wkv_out.append(s_new)
        shift_out.append(xn)
    return (h.reshape(b, 1, d), jnp.stack(mla_rows), jnp.stack(k_rows), jnp.stack(v_rows), jnp.stack(wkv_out),
            jnp.stack(shift_out))


def kernel(x_prompt, x_sample, cache_mla, cache_diff_k, cache_diff_v, state_wkv, state_shift, page_table,
           p_prompt, p_sample, norm_g, w_in, mu_shift, rw_w0, rw_w2, rw_a0, rw_a2, rw_k_k, rw_k_a, rw_r_k,
           rw_gn_g, rw_gn_b, mla_q_norm_g, mla_w_uq, mla_kv_norm_g, mla_w_uk, mla_w_uv, diff_lam_q1,
           diff_lam_k1, diff_lam_q2, diff_lam_k2, diff_subln_g, rel_bias, w_out, w_ple, w_ple_gate,
           final_norm_g):
    W = {"norm_g": norm_g, "w_in": w_in, "mu_shift": mu_shift, "rw_w0": rw_w0, "rw_w2": rw_w2, "rw_a0": rw_a0,
         "rw_a2": rw_a2, "rw_k_k": rw_k_k, "rw_k_a": rw_k_a, "rw_r_k": rw_r_k, "rw_gn_g": rw_gn_g,
         "rw_gn_b": rw_gn_b, "mla_q_norm_g": mla_q_norm_g, "mla_w_uq": mla_w_uq, "mla_kv_norm_g": mla_kv_norm_g,
         "mla_w_uk": mla_w_uk, "mla_w_uv": mla_w_uv, "diff_lam_q1": diff_lam_q1, "diff_lam_k1": diff_lam_k1,
         "diff_lam_q2": diff_lam_q2, "diff_lam_k2": diff_lam_k2, "diff_subln_g": diff_subln_g,
         "rel_bias": rel_bias, "w_out": w_out, "w_ple": w_ple, "w_ple_gate": w_ple_gate,
         "final_norm_g": final_norm_g}
    depth = w_in.shape[0]
    LW = [_layer_weights(l, W) for l in range(depth)]
    y_p, mla_p, dk_p, dv_p, wkv_p, sh_p = _run_prompt(x_prompt, p_prompt, W, LW, depth)
    y_s, mla_s, dk_s, dv_s, wkv_s, sh_s = _run_sample(x_sample, p_sample, state_shift, state_wkv, cache_mla,
                                                      cache_diff_k, cache_diff_v, page_table, W, LW, depth)
    return (y_p, y_s, mla_p, mla_s, dk_p, dk_s, dv_p, dv_s, wkv_p, wkv_s, sh_p, sh_s)
```

```python
import functools
import math

import jax
import jax.numpy as jnp
from jax import lax
from jax.experimental import pallas as pl
from jax.experimental.pallas import tpu as pltpu

F32 = jnp.float32
BF16 = jnp.bfloat16

LANES = 128
SUBLANES = 8
VMEM_LIMIT_BYTES = 56 * 1024 * 1024

D_MODEL = 1024
HEAD_A = 64
D_A = 512
H_A = D_A // HEAD_A
W_LORA = 64
A_LORA = 64
A_IN = 3 * D_A + W_LORA + A_LORA
D_B = 256
DV_B = 64
H_B = D_B // DV_B
NOPE_B = 64
ROPE_B = 32
Q_RANK = 256
KV_RANK = 128
MLA_W = KV_RANK + ROPE_B
ROPE_THETA = 10000.0
D_C = 256
DV_C = 64
H_C = D_C // DV_C
DC = DV_C // 2
NUM_BUCKETS = 32
MAX_DISTANCE = 128
PLE_DIM = 256
PAGE_SIZE = 128
NEG_INF = -1e30
EPS = 1e-6
GN_EPS = 64e-5
SUBLN_EPS = 1e-5
MLA_SCALE = (NOPE_B + ROPE_B) ** -0.5
DIFF_SCALE = DC ** -0.5

SEG_U = (0, A_IN)
SEG_GA = (A_IN, A_IN + D_A)
SEG_CQ = (SEG_GA[1], SEG_GA[1] + Q_RANK)
SEG_CKV = (SEG_CQ[1], SEG_CQ[1] + 2 * LANES)
SEG_GB = (SEG_CKV[1], SEG_CKV[1] + D_B)
SEG_QC = (SEG_GB[1], SEG_GB[1] + D_C)
SEG_KC = (SEG_QC[1], SEG_QC[1] + D_C)
SEG_VC = (SEG_KC[1], SEG_KC[1] + D_C)
SEG_GC = (SEG_VC[1], SEG_VC[1] + D_C)
IN_COLS_PERM = SEG_GC[1]


def _pack_segments(segs):
    out, pos = [], 0
    for a, b in segs:
        out.append((pos, pos + b - a))
        pos += b - a
    return tuple(out)


PROMPT_SEGS_N = (SEG_U, SEG_GA, SEG_GB, SEG_GC, SEG_KC)
PROMPT_SEGS_T = (SEG_CQ, SEG_CKV, SEG_QC, SEG_KC, SEG_VC)
PROMPT_DTYPES_N = (F32, F32, F32, F32, BF16)
SAMPLE_SEGS_N = (SEG_CQ, SEG_CKV, SEG_GB, SEG_QC, SEG_KC, SEG_VC, SEG_GC)
SAMPLE_SEGS_T = (SEG_U, SEG_GA)

RWKV_CHUNK = 64
RWKV_BLOCK = 256
ATTN_TILE = 256
DECODE_PAGES_PER_CHUNK = 32


def _cparams(semantics):
    return pltpu.CompilerParams(dimension_semantics=semantics, vmem_limit_bytes=VMEM_LIMIT_BYTES)


def _full(shape):
    n = len(shape)
    return pl.BlockSpec(shape, lambda *_: (0,) * n)


def _sigmoid(x):
    return 1.0 / (1.0 + jnp.exp(-x))


def _silu(x):
    return x * _sigmoid(x)


def _rms(x, g, eps):
    return x * lax.rsqrt(jnp.mean(x * x, axis=-1, keepdims=True) + eps) * g


def _dot(a, b, **kw):
    return jnp.dot(a, b, preferred_element_type=F32, **kw)


def _dot_nt(a, b, **kw):
    return lax.dot_general(a, b, (((1,), (1,)), ((), ())), preferred_element_type=F32, **kw)


def _dot_tn(a, b, **kw):
    return lax.dot_general(a, b, (((0,), (0,)), ((), ())), preferred_element_type=F32, **kw)


def _split3(x):
    hi = x.astype(BF16)
    rest = x - hi.astype(F32)
    mid = rest.astype(BF16)
    lo = (rest - mid.astype(F32)).astype(BF16)
    return hi, mid, lo


def _dot_split_rhs(a_exact, b):
    hi, mid, lo = _split3(b)
    return _dot(a_exact, hi) + (_dot(a_exact, mid) + _dot(a_exact, lo))


def _dot_split_lhs(a, b_exact):
    hi, mid, lo = _split3(a)
    return _dot(hi, b_exact) + (_dot(mid, b_exact) + _dot(lo, b_exact))


def _mm(a, b):
    return _dot(a.astype(BF16), b.astype(BF16))


def _mm_nt(a, b):
    return _dot_nt(a.astype(BF16), b.astype(BF16))


def _mm_tn(a, b):
    return _dot_tn(a.astype(BF16), b.astype(BF16))


def _inproj_kernel(h_ref, g_ref, w_ref, wt_ref, *out_refs, normalize, segs, segs_t, emit_xn):
    x = h_ref[...]
    xn = _rms(x, g_ref[...], EPS) if normalize else x
    xb = xn.astype(BF16)
    for o_ref, (a, b) in zip(out_refs, segs):
        o_ref[...] = _dot(xb, w_ref[:, a:b]).astype(o_ref.dtype)
    for o_ref, (a, b) in zip(out_refs[len(segs):], segs_t):
        o_ref[...] = _dot_nt(wt_ref[a:b, :], xb)
    n_proj = len(segs) + len(segs_t)
    if emit_xn == "last_row":
        rows = x.shape[0]
        out_refs[n_proj][...] = xn[rows - 1:rows, :]
    elif emit_xn == "all":
        out_refs[n_proj][...] = xn


def _inproj(h2d, norm_g, w_bf16, segs, *, normalize, rows_per_seq, tm, wt_bf16=None, segs_t=(), seg_dtypes=None):
    m, d = h2d.shape
    assert m % tm == 0 and rows_per_seq % tm == 0 or rows_per_seq == 1
    tiles_per_seq = max(rows_per_seq // tm, 1)
    t_cols = rows_per_seq if rows_per_seq > 1 else m
    t_tiles = t_cols // tm
    if w_bf16 is None:
        w_bf16 = jnp.zeros((d, LANES), BF16)
    if wt_bf16 is None:
        wt_bf16 = jnp.zeros((SUBLANES, d), BF16)
    seg_dtypes = seg_dtypes or (F32,) * len(segs)
    out_shapes = [jax.ShapeDtypeStruct((m, b - a), dt) for (a, b), dt in zip(segs, seg_dtypes)]
    out_specs = [pl.BlockSpec((tm, b - a), lambda i: (i, 0)) for a, b in segs]
    for a, b in segs_t:
        out_shapes.append(jax.ShapeDtypeStruct((m // t_cols, b - a, t_cols), F32))
        out_specs.append(pl.BlockSpec((None, b - a, tm), lambda i: (i // t_tiles, 0, i % t_tiles)))
    emit_xn = None
    if normalize:
        if rows_per_seq == 1:
            emit_xn = "all"
            out_shapes.append(jax.ShapeDtypeStruct((m, d), F32))
            out_specs.append(pl.BlockSpec((tm, d), lambda i: (i, 0)))
        else:
            emit_xn = "last_row"
            out_shapes.append(jax.ShapeDtypeStruct((m // rows_per_seq, 1, d), F32))
            out_specs.append(pl.BlockSpec((None, 1, d), lambda i: (i // tiles_per_seq, 0, 0)))
    kern = functools.partial(_inproj_kernel, normalize=normalize, segs=segs, segs_t=segs_t, emit_xn=emit_xn)
    return pl.pallas_call(
        kern,
        grid=(m // tm,),
        in_specs=[pl.BlockSpec((tm, d), lambda i: (i, 0)), _full((1, d)), _full(w_bf16.shape), _full(wt_bf16.shape)],
        out_specs=out_specs,
        out_shape=out_shapes,
        compiler_params=_cparams(("arbitrary",)),
        name="inproj",
    )(h2d, norm_g.reshape(1, d), w_bf16, wt_bf16)


def _outproj_kernel(h_ref, ya_ref, yb_ref, yc_ref, p_ref, wo_ref, wple_ref, wg_ref, fng_ref, o_ref, *, final):
    mixed = (_dot(ya_ref[...], wo_ref[0:D_A, :])
             + _dot(yb_ref[...], wo_ref[D_A:D_A + D_B, :])
             + _dot(yc_ref[...], wo_ref[D_A + D_B:, :]))
    h2 = h_ref[...] + mixed
    ple = _dot(p_ref[...].astype(BF16), wple_ref[...])
    gate = _sigmoid(_dot(h2.astype(BF16), wg_ref[...]))
    h3 = h2 + ple * gate
    o_ref[...] = _rms(h3, fng_ref[...], EPS) if final else h3


def _outproj(h2d, ya, yb, yc, p2d, wo, wple, wg, final_g, *, final, tm):
    m, d = h2d.shape
    row = lambda w: pl.BlockSpec((tm, w), lambda i: (i, 0))
    return pl.pallas_call(
        functools.partial(_outproj_kernel, final=final),
        grid=(m // tm,),
        in_specs=[row(d), row(D_A), row(D_B), row(D_C), row(PLE_DIM),
                  _full(wo.shape), _full(wple.shape), _full(wg.shape), _full((1, d))],
        out_specs=row(d),
        out_shape=jax.ShapeDtypeStruct((m, d), F32),
        compiler_params=_cparams(("arbitrary",)),
        name="outproj",
    )(h2d, ya, yb, yc, p2d, wo, wple, wg, final_g.reshape(1, d))


def _rwkv_prep(um, w0, w2, a0, a2, k_k, k_a):
    r = um[:, 0:D_A]
    k = um[:, D_A:2 * D_A]
    v = um[:, 2 * D_A:3 * D_A]
    w_lo = um[:, 3 * D_A:3 * D_A + W_LORA]
    a_lo = um[:, 3 * D_A + W_LORA:A_IN]
    wl = w0 + _mm(jnp.tanh(w_lo), w2)
    neg = -wl
    softplus = jnp.maximum(neg, 0.0) + jnp.log(1.0 + jnp.exp(-jnp.abs(neg)))
    w = -softplus - 0.5
    log_decay = -jnp.exp(w)
    a = _sigmoid(a0 + _mm(a_lo, a2))
    kk = k * k_k
    k = k * (1.0 + (a - 1.0) * k_a)
    return r, k, v, kk, a, log_decay


def _rwkv_head_out(y, r_h, k_h, v_h, rk_h, gng_h, gnb_h, gate_h):
    mu = jnp.mean(y, axis=-1, keepdims=True)
    var = jnp.mean(jnp.square(y - mu), axis=-1, keepdims=True)
    yn = (y - mu) * lax.rsqrt(var + GN_EPS) * gng_h + gnb_h
    bonus = jnp.sum(r_h * k_h * rk_h, axis=-1, keepdims=True) * v_h
    return (yn + bonus) * _silu(gate_h)


def _rwkv_chunk_kernel(u_ref, uprev0_ref, ga_ref, s0_ref, ones_ref, mu_ref, w0_ref, w2_ref, a0_ref, a2_ref,
                       kk_ref, ka_ref, rk_ref, gng_ref, gnb_ref, y_ref, s_ref, prev_ref):
    step = pl.program_id(1)
    R = u_ref.shape[0]
    C = RWKV_CHUNK
    subs = range(R // C)

    @pl.when(step == 0)
    def _():
        prev_ref[...] = uprev0_ref[...]
        s_ref[...] = s0_ref[...]

    u = u_ref[...]
    row = lax.broadcasted_iota(jnp.int32, (R, 1), 0)
    u_prev = jnp.where(row == 0, prev_ref[...], pltpu.roll(u, 1, axis=0))
    prev_ref[...] = u[R - 1:R, :]
    um = u + mu_ref[...] * (u_prev - u)
    r, k, v, kk, a, log_decay = _rwkv_prep(um, w0_ref[...], w2_ref[...], a0_ref[...], a2_ref[...],
                                           kk_ref[...], ka_ref[...])

    ri = lax.broadcasted_iota(jnp.int32, (R, R), 0)
    rj = lax.broadcasted_iota(jnp.int32, (R, R), 1)
    same_chunk = (ri // C) == (rj // C)
    cs = _dot_split_rhs((same_chunk & (rj <= ri)).astype(BF16), log_decay)
    cs_last = [cs[(sb + 1) * C - 1:(sb + 1) * C, :] for sb in subs]
    cs_end = jnp.concatenate([jnp.broadcast_to(x, (C, x.shape[1])) for x in cs_last], axis=0)
    p_end = [jnp.exp(x) for x in cs_last]
    e_inv = jnp.exp(-cs)
    e_rem = jnp.exp(cs_end - cs)
    kk_n = kk / jnp.maximum(jnp.sqrt(_dot_split_lhs(kk * kk, ones_ref[...])), 1e-12)
    b_f = kk_n * a
    a_t = (-kk_n * jnp.exp(cs - log_decay)).astype(BF16)
    r_t = (r * jnp.exp(cs)).astype(BF16)
    b_t = (b_f * e_inv).astype(BF16)
    k_t = (k * e_inv).astype(BF16)
    b_end = (b_f * e_rem).astype(BF16)
    k_end = (k * e_rem).astype(BF16)
    v_b = v.astype(BF16)
    ga = ga_ref[...]

    t2 = lax.broadcasted_iota(jnp.int32, (C, 2 * C), 0)
    j2 = lax.broadcasted_iota(jnp.int32, (C, 2 * C), 1)
    j2 = jnp.where(j2 >= C, j2 - C, j2)
    strict2 = j2 < t2
    incl2 = j2 <= t2
    ti = lax.broadcasted_iota(jnp.int32, (C, C), 0)
    tj = lax.broadcasted_iota(jnp.int32, (C, C), 1)
    eye = (ti == tj).astype(F32)
    zeros = jnp.zeros((C, HEAD_A), BF16)

    units = [(sb, h) for sb in subs for h in range(H_A)]
    blk = lambda x, sb, h: x[sb * C:(sb + 1) * C, h * HEAD_A:(h + 1) * HEAD_A]
    a_h = {un: blk(a_t, *un) for un in units}
    r_h = {un: blk(r_t, *un) for un in units}
    v_h = {un: blk(v_b, *un) for un in units}
    gram = {un: _dot_nt(jnp.concatenate([a_h[un], r_h[un]], axis=0),
                        jnp.concatenate([blk(b_t, *un), blk(k_t, *un)], axis=0)) for un in units}
    l_top = {un: jnp.where(strict2, gram[un][0:C], 0.0) for un in units}
    m_bot = {un: jnp.where(incl2, gram[un][C:], 0.0).astype(BF16) for un in units}
    lv = {un: _dot(l_top[un].astype(BF16), jnp.concatenate([zeros, v_h[un]], axis=0)) for un in units}

    l_ab = {un: l_top[un][:, 0:C] for un in units}
    inv = {un: eye + l_ab[un] for un in units}
    pw = {un: l_ab[un].astype(BF16) for un in units}
    for _ in range(int(math.log2(C)) - 1):
        pw = {un: _mm(pw[un], pw[un]).astype(BF16) for un in units}
        inv = {un: inv[un] + _mm(inv[un], pw[un]) for un in units}
    inv = {un: inv[un].astype(BF16) for un in units}
    w_mat = {un: _mm(inv[un], a_h[un]).astype(BF16) for un in units}
    u_v = {un: _mm(inv[un], lv[un]) for un in units}

    state = [s_ref[h] for h in range(H_A)]
    for sb in subs:
        heads = [(sb, h) for h in range(H_A)]
        state_b = [x.astype(BF16) for x in state]
        uv = [jnp.concatenate([(_dot_nt(w_mat[un], state_b[un[1]]) + u_v[un]).astype(BF16), v_h[un]], axis=0)
              for un in heads]
        y = [_dot_nt(r_h[un], state_b[un[1]]) + _dot(m_bot[un], uv[un[1]]) for un in heads]
        state = [state[h] * p_end[sb][:, h * HEAD_A:(h + 1) * HEAD_A]
                 + _dot_tn(uv[h], jnp.concatenate([blk(b_end, sb, h), blk(k_end, sb, h)], axis=0))
                 for h in range(H_A)]
        rows = slice(sb * C, (sb + 1) * C)
        for h in range(H_A):
            hs = slice(h * HEAD_A, (h + 1) * HEAD_A)
            out = _rwkv_head_out(y[h], r[rows, hs], k[rows, hs], v[rows, hs], rk_ref[:, hs], gng_ref[:, hs],
                                 gnb_ref[:, hs], ga[rows, hs])
            y_ref[rows, hs] = out.astype(y_ref.dtype)
    for h in range(H_A):
        s_ref[h] = state[h]


def _rwkv_chunked(u, uprev0, ga, s0, params):
    b, t, _ = u.shape
    c = RWKV_BLOCK
    assert t % c == 0
    head_of_lane = jnp.arange(D_A, dtype=jnp.int32) // HEAD_A
    head_ones = (head_of_lane[:, None] == head_of_lane[None, :]).astype(BF16)
    tok = lambda w: pl.BlockSpec((None, c, w), lambda i, j: (i, j, 0))
    state = pl.BlockSpec((None, H_A, HEAD_A, HEAD_A), lambda i, j: (i, 0, 0, 0))
    return pl.pallas_call(
        _rwkv_chunk_kernel,
        grid=(b, t // c),
        in_specs=[tok(A_IN), pl.BlockSpec((None, 1, A_IN), lambda i, j: (i, 0, 0)), tok(D_A), state,
                  _full(head_ones.shape)] + [_full(p.shape) for p in params],
        out_specs=[tok(D_A), state],
        out_shape=[jax.ShapeDtypeStruct((b, t, D_A), BF16), jax.ShapeDtypeStruct(s0.shape, F32)],
        scratch_shapes=[pltpu.VMEM((1, A_IN), F32)],
        compiler_params=_cparams(("parallel", "arbitrary")),
        name="rwkv_chunked",
    )(u, uprev0, ga, s0, head_ones, *params)


def _rwkv_step_kernel(r_ref, k_ref, v_ref, lo_ref, rp_ref, kp_ref, vp_ref, lop_ref, ga_ref, s0_ref,
                      mur_ref, muk_ref, muv_ref, mulo_ref, w0_ref, w2t_ref, a0_ref, a2t_ref, kk_ref, ka_ref,
                      rk_ref, gng_ref, gnb_ref, y_ref, s_ref, y_scr):
    mix = lambda x_ref, p_ref, mu_ref: x_ref[...] + mu_ref[...] * (p_ref[...] - x_ref[...])
    r = mix(r_ref, rp_ref, mur_ref)
    k = mix(k_ref, kp_ref, muk_ref)
    v = mix(v_ref, vp_ref, muv_ref)
    lo = mix(lo_ref, lop_ref, mulo_ref)
    neg = -(w0_ref[...] + _mm(w2t_ref[...], jnp.tanh(lo[0:W_LORA])))
    w = -(jnp.maximum(neg, 0.0) + jnp.log(1.0 + jnp.exp(-jnp.abs(neg)))) - 0.5
    decay = jnp.exp(-jnp.exp(w))
    a = _sigmoid(a0_ref[...] + _mm(a2t_ref[...], lo[W_LORA:]))
    kk = k * kk_ref[...]
    kk = kk / jnp.maximum(jnp.sqrt(jnp.sum(kk * kk, axis=0, keepdims=True)), 1e-12)
    k = k * (1.0 + (a - 1.0) * ka_ref[...])
    b = kk * a
    nkk = -kk
    for i in range(HEAD_A):
        s = s0_ref[i]
        sa = jnp.sum(s * nkk, axis=0, keepdims=True)
        s_new = s * decay + sa * b + v[i:i + 1, :] * k
        s_ref[i] = s_new
        y_scr[i:i + 1, :] = jnp.sum(s_new * r, axis=0, keepdims=True)
    y = y_scr[...]
    mu = jnp.mean(y, axis=0, keepdims=True)
    var = jnp.mean(jnp.square(y - mu), axis=0, keepdims=True)
    yn = (y - mu) * lax.rsqrt(var + GN_EPS) * gng_ref[...] + gnb_ref[...]
    bonus = jnp.sum(r * k * rk_ref[...], axis=0, keepdims=True) * v
    y_ref[...] = ((yn + bonus) * _silu(ga_ref[...])).astype(y_ref.dtype)


def _rwkv_step(u_t, uprev_t, ga_t, s0_t, params_t):
    nb = u_t.shape[1]
    mu, w0, w2t, a0, a2t, k_k, k_a, r_k, gn_g, gn_b = params_t
    n_head_blocks = D_A // HEAD_A
    lora_block = 3 * D_A // (W_LORA + A_LORA)
    feat = lambda off: pl.BlockSpec((HEAD_A, nb), lambda h: (off * n_head_blocks + h, 0))
    lora = pl.BlockSpec((W_LORA + A_LORA, nb), lambda h: (lora_block, 0))
    col = lambda off: pl.BlockSpec((HEAD_A, 1), lambda h: (off * n_head_blocks + h, 0))
    lora_col = pl.BlockSpec((W_LORA + A_LORA, 1), lambda h: (lora_block, 0))
    head_rows = lambda w: pl.BlockSpec((HEAD_A, w), lambda h: (h, 0))
    state = pl.BlockSpec((None, HEAD_A, HEAD_A, nb), lambda h: (h, 0, 0, 0))
    return pl.pallas_call(
        _rwkv_step_kernel,
        grid=(H_A,),
        in_specs=[feat(0), feat(1), feat(2), lora, feat(0), feat(1), feat(2), lora, head_rows(nb), state,
                  col(0), col(1), col(2), lora_col, col(0), head_rows(W_LORA), col(0), head_rows(A_LORA),
                  col(0), col(0), col(0), col(0), col(0)],
        out_specs=[head_rows(nb), state],
        out_shape=[jax.ShapeDtypeStruct((D_A, nb), BF16), jax.ShapeDtypeStruct(s0_t.shape, F32)],
        scratch_shapes=[pltpu.VMEM((HEAD_A, nb), F32)],
        compiler_params=_cparams(("parallel",)),
        name="rwkv_step",
    )(u_t, u_t, u_t, u_t, uprev_t, uprev_t, uprev_t, uprev_t, ga_t, s0_t,
      mu, mu, mu, mu, w0, w2t, a0, a2t, k_k, k_a, r_k, gn_g, gn_b)


def _mla_prep_kernel(cq_ref, ckv_ref, cos_ref, sin_ref, qg_ref, wuq_ref, kvg_ref, wukt_ref, q_ref, rows_ref):
    cos2 = cos_ref[...]
    sin2 = sin_ref[...]
    qn = _rms(cq_ref[...], qg_ref[...], EPS).astype(BF16)
    q = _dot(qn, wuq_ref[...])
    for h in range(H_B):
        qh = q[:, h * LANES:(h + 1) * LANES]
        q_lat = _dot(qh[:, :NOPE_B].astype(BF16), wukt_ref[h])
        q_rope = qh[:, NOPE_B:NOPE_B + ROPE_B] * cos2 + qh[:, NOPE_B + ROPE_B:] * sin2
        q_ref[h, :, 0:KV_RANK] = (q_lat * MLA_SCALE).astype(q_ref.dtype)
        q_ref[h, :, KV_RANK:MLA_W] = (q_rope * MLA_SCALE).astype(q_ref.dtype)
    ckv = ckv_ref[...]
    rows_ref[:, 0:KV_RANK] = _rms(ckv[:, 0:KV_RANK], kvg_ref[...], EPS)
    rows_ref[:, KV_RANK:MLA_W] = (ckv[:, KV_RANK:KV_RANK + ROPE_B] * cos2
                                  + ckv[:, KV_RANK + ROPE_B:KV_RANK + 2 * ROPE_B] * sin2)


def _mla_prep(cq, ckv, cos2, sin2, q_norm_g, wuq_ext, kv_norm_g, wuk_t, *, tm, pos_tiles):
    m = cq.shape[0]
    row = lambda w: pl.BlockSpec((tm, w), lambda i: (i, 0))
    pos = pl.BlockSpec((tm, ROPE_B), lambda i: (i % pos_tiles, 0))
    return pl.pallas_call(
        _mla_prep_kernel,
        grid=(m // tm,),
        in_specs=[row(Q_RANK), row(2 * LANES), pos, pos, _full((1, Q_RANK)), _full(wuq_ext.shape),
                  _full((1, KV_RANK)), _full(wuk_t.shape)],
        out_specs=[pl.BlockSpec((H_B, tm, MLA_W), lambda i: (0, i, 0)), row(MLA_W)],
        out_shape=[jax.ShapeDtypeStruct((H_B, m, MLA_W), BF16), jax.ShapeDtypeStruct((m, MLA_W), F32)],
        compiler_params=_cparams(("parallel",)),
        name="mla_prep",
    )(cq, ckv, cos2, sin2, q_norm_g.reshape(1, Q_RANK), wuq_ext, kv_norm_g.reshape(1, KV_RANK), wuk_t)


def _mla_out(o_lat, wuv_ref, gb, o_ref, rows_per_head):
    for h in range(H_B):
        o_h = _dot(o_lat[h * rows_per_head:(h + 1) * rows_per_head].astype(BF16), wuv_ref[h])
        hs = slice(h * DV_B, (h + 1) * DV_B)
        o_ref[:, hs] = (o_h * _silu(gb[:, hs])).astype(o_ref.dtype)


def _mla_prep_t_kernel(cq_ref, ckv_ref, cos_ref, sin_ref, qg_ref, wuqt_ref, kvg_ref, wuk_ref,
                       q_ref, rows_ref, rowst_ref):
    tm = cq_ref.shape[1]
    cos2 = cos_ref[...]
    sin2 = sin_ref[...]
    cq = cq_ref[...]
    qn = (cq * lax.rsqrt(jnp.mean(cq * cq, axis=0, keepdims=True) + EPS) * qg_ref[...]).astype(BF16)
    q = _dot(wuqt_ref[...], qn)
    pad = jnp.zeros((2 * LANES - MLA_W, tm), q_ref.dtype)
    for h in range(H_B):
        qh = q[h * LANES:(h + 1) * LANES]
        q_lat = _dot(wuk_ref[h], qh[0:NOPE_B].astype(BF16))
        q_rope = qh[NOPE_B:NOPE_B + ROPE_B] * cos2 + qh[NOPE_B + ROPE_B:] * sin2
        cols = slice(h * tm, (h + 1) * tm)
        q_ref[0:KV_RANK, cols] = (q_lat * MLA_SCALE).astype(q_ref.dtype)
        q_ref[KV_RANK:MLA_W, cols] = (q_rope * MLA_SCALE).astype(q_ref.dtype)
        q_ref[MLA_W:, cols] = pad
    ckv = ckv_ref[...]
    c = ckv[0:KV_RANK]
    cn = c * lax.rsqrt(jnp.mean(c * c, axis=0, keepdims=True) + EPS) * kvg_ref[...]
    kr = ckv[KV_RANK:KV_RANK + ROPE_B] * cos2 + ckv[KV_RANK + ROPE_B:KV_RANK + 2 * ROPE_B] * sin2
    rowst_ref[0:KV_RANK, :] = cn
    rowst_ref[KV_RANK:MLA_W, :] = kr
    rows_t = jnp.concatenate([cn, kr, jnp.zeros((2 * LANES - MLA_W, tm), F32)], axis=0)
    rows_ref[...] = rows_t.T.astype(rows_ref.dtype)


def _mla_prep_t(cq_t, ckv_t, cos2_t, sin2_t, q_norm_g, wuq_ext_t, kv_norm_g, wuk, *, tm):
    b, _, t = cq_t.shape
    nt = t // tm
    blk = lambda w: pl.BlockSpec((None, w, tm), lambda bi, i: (bi, 0, i))
    pos = pl.BlockSpec((ROPE_B, tm), lambda bi, i: (0, i))
    return pl.pallas_call(
        _mla_prep_t_kernel,
        grid=(b, nt),
        in_specs=[blk(Q_RANK), blk(2 * LANES), pos, pos, _full((Q_RANK, 1)), _full(wuq_ext_t.shape),
                  _full((KV_RANK, 1)), _full(wuk.shape)],
        out_specs=[pl.BlockSpec((None, 2 * LANES, H_B * tm), lambda bi, i: (bi * nt + i, 0, 0)),
                   pl.BlockSpec((None, tm, 2 * LANES), lambda bi, i: (bi, i, 0)),
                   blk(MLA_W)],
        out_shape=[jax.ShapeDtypeStruct((b * nt, 2 * LANES, H_B * tm), BF16),
                   jax.ShapeDtypeStruct((b, t, 2 * LANES), BF16),
                   jax.ShapeDtypeStruct((b, MLA_W, t), F32)],
        compiler_params=_cparams(("parallel", "parallel")),
        name="mla_prep_t",
    )(cq_t, ckv_t, cos2_t, sin2_t, q_norm_g.reshape(Q_RANK, 1), wuq_ext_t, kv_norm_g.reshape(KV_RANK, 1), wuk)


def _softmax_update_t(s, m_ref, l_ref, cols):
    m_old = m_ref[:, cols]
    m_new = jnp.maximum(m_old, jnp.max(s, axis=0, keepdims=True))
    alpha = jnp.exp(m_old - m_new)
    p = jnp.exp(s - m_new)
    l_ref[:, cols] = alpha * l_ref[:, cols] + jnp.sum(p, axis=0, keepdims=True)
    m_ref[:, cols] = m_new
    return alpha, p


def _bias_kernel(rb_ref, dist_ref, o_ref):
    dist = dist_ref[...]
    n = jnp.maximum(dist, 0)
    max_exact = NUM_BUCKETS // 2
    n_safe = jnp.maximum(n, max_exact).astype(F32)
    large = max_exact + (jnp.log(n_safe / max_exact) / math.log(MAX_DISTANCE / max_exact)
                         * (NUM_BUCKETS - max_exact)).astype(jnp.int32)
    large = jnp.minimum(large, NUM_BUCKETS - 1)
    bucket = jnp.where(n < max_exact, n, large)
    for h in range(H_C):
        bias = jnp.zeros(dist.shape, F32)
        for kb in range(NUM_BUCKETS):
            bias = jnp.where(bucket == kb, rb_ref[kb * H_C + h], bias)
        o_ref[h] = jnp.where(dist >= 0, bias, NEG_INF)


def _bias_tiles(rel_bias, dist):
    g, r, c = dist.shape
    grid_spec = pltpu.PrefetchScalarGridSpec(
        num_scalar_prefetch=1,
        grid=(g,),
        in_specs=[pl.BlockSpec((None, r, c), lambda i, rb: (i, 0, 0))],
        out_specs=pl.BlockSpec((None, H_C, r, c), lambda i, rb: (i, 0, 0, 0)),
    )
    return pl.pallas_call(
        _bias_kernel,
        grid_spec=grid_spec,
        out_shape=jax.ShapeDtypeStruct((g, H_C, r, c), F32),
        compiler_params=_cparams(("arbitrary",)),
        name="rel_bias_tiles",
    )(rel_bias.reshape(-1), dist)


def _diff_lambda(lam_ref, lam_init):
    lam = lam_ref[...]
    e1 = jnp.exp(jnp.sum(lam[0:1] * lam[1:2], axis=-1, keepdims=True))
    e2 = jnp.exp(jnp.sum(lam[2:3] * lam[3:4], axis=-1, keepdims=True))
    return e1 - e2 + lam_init


def _diff_queries(qc):
    lane = lax.broadcasted_iota(jnp.int32, qc.shape, 1)
    qs = qc * DIFF_SCALE
    groups = []
    for h in range(H_C):
        for c in range(2):
            lo = h * DV_C + c * DC
            groups.append(jnp.where((lane >= lo) & (lane < lo + DC), qs, 0.0))
    return jnp.concatenate(groups, axis=0)


def _diff_out(acc, l, lam, lam_init, sg, gc, o_ref, rows):
    for h in range(H_C):
        hs = slice(h * DV_C, (h + 1) * DV_C)
        r1 = slice((2 * h) * rows, (2 * h + 1) * rows)
        r2 = slice((2 * h + 1) * rows, (2 * h + 2) * rows)
        o = acc[r1, hs] / l[r1] - lam * (acc[r2, hs] / l[r2])
        o = _rms(o, sg, SUBLN_EPS) * (1.0 - lam_init)
        o_ref[:, hs] = (o * _silu(gc[:, hs])).astype(o_ref.dtype)


def _prompt_attn_kernel(pi_ref, pj_ref,
                        q_ref, k_ref, ct_ref, mask_ref, gb_ref, wuvt_ref,
                        qc_ref, kc_ref, vt_ref, gc_ref, bias_ref, lam_ref, sg_ref,
                        ob_ref, oc_ref,
                        m_ref, l_ref, acc_ref, qbd_ref, md_ref, ld_ref, accd_ref, *, lam_init):
    step = pl.program_id(1)
    i = pi_ref[step]
    j = pj_ref[step]
    tq = ob_ref.shape[0]
    nblk = 2 * H_C

    @pl.when(j == 0)
    def _():
        m_ref[...] = jnp.full_like(m_ref, NEG_INF)
        l_ref[...] = jnp.zeros_like(l_ref)
        acc_ref[...] = jnp.zeros_like(acc_ref)
        q = qc_ref[...] * DIFF_SCALE
        feat = lax.broadcasted_iota(jnp.int32, q.shape, 0)
        for blk in range(nblk):
            lo = (blk // 2) * DV_C + (blk % 2) * DC
            qbd_ref[:, blk * tq:(blk + 1) * tq] = jnp.where((feat >= lo) & (feat < lo + DC), q, 0.0).astype(BF16)
        md_ref[...] = jnp.full_like(md_ref, NEG_INF)
        ld_ref[...] = jnp.zeros_like(ld_ref)
        accd_ref[...] = jnp.zeros_like(accd_ref)

    k = k_ref[...].astype(BF16)
    ct = ct_ref[...].astype(BF16)
    mask = mask_ref[jnp.minimum(i - j, 1)]
    heads = range(H_B)
    cols = [slice(h * tq, (h + 1) * tq) for h in heads]
    kc = kc_ref[...].astype(BF16)
    vt = vt_ref[...].astype(BF16)
    tile = jnp.minimum(i - j, 2)
    blocks = range(nblk)
    dcols = [slice(blk * tq, (blk + 1) * tq) for blk in blocks]
    s_b = [_dot(k, q_ref[:, cols[h]]) + mask for h in heads]
    s_d = [_dot(kc, qbd_ref[:, dcols[blk]]) + bias_ref[tile, blk // 2] for blk in blocks]
    ap_b = [_softmax_update_t(s_b[h], m_ref, l_ref, cols[h]) for h in heads]
    ap_d = [_softmax_update_t(s_d[blk], md_ref, ld_ref, dcols[blk]) for blk in blocks]
    pv_b = [_dot(ct, ap_b[h][1].astype(BF16)) for h in heads]
    pv_d = [_dot(vt[(blk // 2) * DV_C:(blk // 2 + 1) * DV_C], ap_d[blk][1].astype(BF16)) for blk in blocks]
    for h in heads:
        acc_ref[:, cols[h]] = ap_b[h][0] * acc_ref[:, cols[h]] + pv_b[h]
    for blk in blocks:
        accd_ref[blk] = ap_d[blk][0] * accd_ref[blk] + pv_d[blk]

    @pl.when(j == i)
    def _():
        outs = []
        for h in heads:
            o_lat = (acc_ref[:, cols[h]] / l_ref[:, cols[h]]).astype(BF16)
            outs.append(_dot(wuvt_ref[h], o_lat))
        o = jnp.concatenate(outs, axis=0).T
        ob_ref[...] = (o * _silu(gb_ref[...])).astype(ob_ref.dtype)
        lam = _diff_lambda(lam_ref, lam_init)
        outs = []
        for h in range(H_C):
            o = (accd_ref[2 * h] / ld_ref[:, dcols[2 * h]]
                 - lam * (accd_ref[2 * h + 1] / ld_ref[:, dcols[2 * h + 1]]))
            o = o * lax.rsqrt(jnp.mean(o * o, axis=0, keepdims=True) + SUBLN_EPS) * sg_ref[...]
            outs.append(o * (1.0 - lam_init))
        o = jnp.concatenate(outs, axis=0).T
        oc_ref[...] = (o * _silu(gc_ref[...])).astype(oc_ref.dtype)


def _prompt_attn(q_t, rows_pad, rows_t, mask, gb, wuv_t, qc_t, kc, vc_t, gc, bias_tiles, lam_vecs, subln_g, lam_init):
    b, t, _ = kc.shape
    tq = ATTN_TILE
    nq = t // tq
    pairs = [(i, j) for i in range(nq) for j in range(i + 1)]
    pair_i = jnp.asarray([p[0] for p in pairs], jnp.int32)
    pair_j = jnp.asarray([p[1] for p in pairs], jnp.int32)
    nblk = 2 * H_C
    q_tile = lambda w: pl.BlockSpec((None, tq, w), lambda bi, s, pi, pj: (bi, pi[s], 0))
    k_tile = lambda w: pl.BlockSpec((None, tq, w), lambda bi, s, pi, pj: (bi, pj[s], 0))
    qt_tile = lambda w: pl.BlockSpec((None, w, tq), lambda bi, s, pi, pj: (bi, 0, pi[s]))
    kt_tile = lambda w: pl.BlockSpec((None, w, tq), lambda bi, s, pi, pj: (bi, 0, pj[s]))
    const = lambda shape: pl.BlockSpec(shape, lambda bi, s, pi, pj: (0,) * len(shape))
    grid_spec = pltpu.PrefetchScalarGridSpec(
        num_scalar_prefetch=2,
        grid=(b, len(pairs)),
        in_specs=[pl.BlockSpec((None, 2 * LANES, H_B * tq), lambda bi, s, pi, pj: (bi * nq + pi[s], 0, 0)),
                  k_tile(2 * LANES), kt_tile(KV_RANK), const(mask.shape), q_tile(D_B), const(wuv_t.shape),
                  qt_tile(D_C), k_tile(D_C), kt_tile(D_C), q_tile(D_C), const(bias_tiles.shape),
                  const(lam_vecs.shape), const((DV_C, 1))],
        out_specs=[q_tile(D_B), q_tile(D_C)],
        scratch_shapes=[pltpu.VMEM((1, H_B * tq), F32), pltpu.VMEM((1, H_B * tq), F32),
                        pltpu.VMEM((KV_RANK, H_B * tq), F32),
                        pltpu.VMEM((D_C, nblk * tq), BF16), pltpu.VMEM((1, nblk * tq), F32),
                        pltpu.VMEM((1, nblk * tq), F32), pltpu.VMEM((nblk, DV_C, tq), F32)],
    )
    return pl.pallas_call(
        functools.partial(_prompt_attn_kernel, lam_init=lam_init),
        grid_spec=grid_spec,
        out_shape=[jax.ShapeDtypeStruct((b, t, D_B), BF16), jax.ShapeDtypeStruct((b, t, D_C), BF16)],
        compiler_params=_cparams(("parallel", "arbitrary")),
        name="prompt_attn",
    )(pair_i, pair_j, q_t, rows_pad, rows_t, mask, gb, wuv_t, qc_t, kc, vc_t, gc, bias_tiles, lam_vecs,
      subln_g.reshape(DV_C, 1))


def _softmax_step(s, m, l):
    m_new = jnp.maximum(m, jnp.max(s, axis=-1, keepdims=True))
    alpha = jnp.exp(m - m_new)
    p = jnp.exp(s - m_new)
    return m_new, alpha * l + jnp.sum(p, axis=-1, keepdims=True), alpha, p


def _decode_attn_kernel(pt_ref, q_ref, row_ref, gb_ref, wuv_ref, qc_ref, kn_ref, vn_ref, gc_ref, bias_ref, lam_ref,
                        sg_ref, cm_hbm, ck_hbm, cv_hbm, ob_ref, oc_ref, mbuf, kbuf, vbuf, sems,
                        *, layer, pages_per_seq, lam_init):
    n_pages = DECODE_PAGES_PER_CHUNK
    seq = pl.program_id(0)
    chunks = pages_per_seq // n_pages
    total = pl.num_programs(0) * chunks

    def chunk_copies(g, slot):
        out = []
        for p in range(n_pages):
            page = pt_ref[g * n_pages + p]
            out.append(pltpu.make_async_copy(cm_hbm.at[layer, page], mbuf.at[slot, p], sems.at[slot, 0]))
            out.append(pltpu.make_async_copy(ck_hbm.at[layer, page], kbuf.at[slot, p], sems.at[slot, 1]))
            out.append(pltpu.make_async_copy(cv_hbm.at[layer, page], vbuf.at[slot, p], sems.at[slot, 2]))
        return out

    @pl.when(seq == 0)
    def _():
        for cp in chunk_copies(0, 0):
            cp.start()

    q = q_ref[...]
    qbd = _diff_queries(qc_ref[...])
    qb = qbd.astype(BF16)

    def chunk_body(c, carry):
        mb, lb, accb, md, ld, accd = carry
        g = seq * chunks + c
        slot = lax.rem(g, 2)

        @pl.when(g + 1 < total)
        def _():
            for cp in chunk_copies(g + 1, 1 - slot):
                cp.start()

        for cp in chunk_copies(g, slot):
            cp.wait()

        pages = range(n_pages)
        lanes = [slice(p * PAGE_SIZE, (p + 1) * PAGE_SIZE) for p in pages]
        ks = [mbuf[slot, p].astype(BF16) for p in pages]
        s_b = jnp.concatenate([_dot(q, ks[p]) for p in pages], axis=1)
        s_d = jnp.concatenate([_dot(qb, kbuf[slot, p].astype(BF16)) for p in pages], axis=1)
        s_d = s_d + bias_ref[jnp.where(c == chunks - 1, 1, 0)]
        mb, lb, alpha_b, p_b = _softmax_step(s_b, mb, lb)
        md, ld, alpha_d, p_d = _softmax_step(s_d, md, ld)
        p_b = p_b.astype(BF16)
        p_d = p_d.astype(BF16)
        pv_b = [_dot_nt(p_b[:, lanes[p]], ks[p][0:KV_RANK, :]) for p in pages]
        pv_d = [_dot_nt(p_d[:, lanes[p]], vbuf[slot, p].astype(BF16)) for p in pages]
        accb = alpha_b * accb + functools.reduce(lambda x, y: x + y, pv_b)
        accd = alpha_d * accd + functools.reduce(lambda x, y: x + y, pv_d)
        return mb, lb, accb, md, ld, accd

    init = (jnp.full((H_B, 1), NEG_INF, F32), jnp.zeros((H_B, 1), F32), jnp.zeros((H_B, KV_RANK), F32),
            jnp.full((2 * H_C, 1), NEG_INF, F32), jnp.zeros((2 * H_C, 1), F32), jnp.zeros((2 * H_C, D_C), F32))
    mb, lb, accb, md, ld, accd = lax.fori_loop(0, chunks, chunk_body, init)

    row = row_ref[...]
    s_new = jnp.sum(q.astype(F32) * row, axis=-1, keepdims=True)
    mb, lb, alpha, p_new = _softmax_step(s_new, mb, lb)
    accb = alpha * accb + p_new * row[:, 0:KV_RANK]
    _mla_out(accb / lb, wuv_ref, gb_ref[...], ob_ref, 1)

    s_new = jnp.sum(qbd * kn_ref[...], axis=-1, keepdims=True) + bias_ref[2][:, 0:1]
    md, ld, alpha, p_new = _softmax_step(s_new, md, ld)
    accd = alpha * accd + p_new * vn_ref[...]
    _diff_out(accd, ld, _diff_lambda(lam_ref, lam_init), lam_init, sg_ref[...], gc_ref[...], oc_ref, 1)


def _decode_attn(page_table_flat, q, rows_new, gb, wuv, qc, kc, vc, gc, bias_rows, lam_vecs, subln_g,
                 cache_mla, cache_k, cache_v, layer, pages_per_seq, lam_init):
    b = q.shape[0]
    n_pages = DECODE_PAGES_PER_CHUNK
    assert pages_per_seq % n_pages == 0
    per_b = lambda shape: pl.BlockSpec((None,) + shape, lambda bi, pt: (bi,) + (0,) * len(shape))
    const = lambda shape: pl.BlockSpec(shape, lambda bi, pt: (0,) * len(shape))
    hbm = pl.BlockSpec(memory_space=pl.ANY)
    grid_spec = pltpu.PrefetchScalarGridSpec(
        num_scalar_prefetch=1,
        grid=(b,),
        in_specs=[per_b((H_B, MLA_W)), per_b((1, MLA_W)), per_b((1, D_B)), const(wuv.shape),
                  per_b((1, D_C)), per_b((1, D_C)), per_b((1, D_C)), per_b((1, D_C)), const(bias_rows.shape),
                  const(lam_vecs.shape), const((1, DV_C)), hbm, hbm, hbm],
        out_specs=[per_b((1, D_B)), per_b((1, D_C))],
        scratch_shapes=[pltpu.VMEM((2, n_pages, MLA_W, PAGE_SIZE), F32),
                        pltpu.VMEM((2, n_pages, D_C, PAGE_SIZE), F32),
                        pltpu.VMEM((2, n_pages, D_C, PAGE_SIZE), F32),
                        pltpu.SemaphoreType.DMA((2, 3))],
    )
    return pl.pallas_call(
        functools.partial(_decode_attn_kernel, layer=layer, pages_per_seq=pages_per_seq, lam_init=lam_init),
        grid_spec=grid_spec,
        out_shape=[jax.ShapeDtypeStruct((b, 1, D_B), BF16), jax.ShapeDtypeStruct((b, 1, D_C), BF16)],
        compiler_params=_cparams(("arbitrary",)),
        name="decode_attn",
    )(page_table_flat, q, rows_new, gb, wuv, qc, kc, vc, gc, bias_rows, lam_vecs, subln_g.reshape(1, DV_C),
      cache_mla, cache_k, cache_v)


def _permute_w_in(w):
    o_ckv = A_IN + D_A + Q_RANK
    o_kr = o_ckv + KV_RANK
    o_gb = o_kr + ROPE_B
    half = ROPE_B // 2
    pad = jnp.zeros((w.shape[0], 2 * LANES - KV_RANK - 2 * ROPE_B), w.dtype)
    out = jnp.concatenate([w[:, :o_gb], w[:, o_kr + half:o_gb], w[:, o_kr:o_kr + half], pad, w[:, o_gb:]], axis=1)
    assert out.shape[1] == IN_COLS_PERM
    return out.astype(BF16)


def _extend_w_uq(w):
    w = w.reshape(Q_RANK, H_B, NOPE_B + ROPE_B)
    half = ROPE_B // 2
    rope = w[:, :, NOPE_B:]
    swapped = jnp.concatenate([rope[:, :, half:], rope[:, :, :half]], axis=-1)
    return jnp.concatenate([w, swapped], axis=-1).reshape(Q_RANK, H_B * LANES).astype(BF16)


def _rope_tables(pos):
    inv = ROPE_THETA ** (-jnp.arange(0, ROPE_B, 2, dtype=F32) / ROPE_B)
    ang = pos.astype(F32)[:, None] * inv[None, :]
    cos, sin = jnp.cos(ang), jnp.sin(ang)
    return jnp.concatenate([cos, cos], axis=-1), jnp.concatenate([-sin, sin], axis=-1)


def _layer_weights(l, W):
    row = lambda a: a.reshape(1, -1)
    rwkv = (row(W["mu_shift"][l]), row(W["rw_w0"][l]), W["rw_w2"][l], row(W["rw_a0"][l]), W["rw_a2"][l],
            row(W["rw_k_k"][l]), row(W["rw_k_a"][l]), row(W["rw_r_k"][l]), row(W["rw_gn_g"][l]),
            row(W["rw_gn_b"][l]))
    col = lambda a: a.reshape(-1, 1)
    rwkv_t = (col(W["mu_shift"][l]), col(W["rw_w0"][l]), W["rw_w2"][l].T, col(W["rw_a0"][l]), W["rw_a2"][l].T,
              col(W["rw_k_k"][l]), col(W["rw_k_a"][l]), col(W["rw_r_k"][l]), col(W["rw_gn_g"][l]),
              col(W["rw_gn_b"][l]))
    w_in = _permute_w_in(W["w_in"][l])
    wuq = _extend_w_uq(W["mla_w_uq"][l])
    return dict(
        w_in_n=jnp.concatenate([w_in[:, a:b] for a, b in PROMPT_SEGS_N], axis=1),
        w_in_t=jnp.concatenate([w_in[:, a:b] for a, b in PROMPT_SEGS_T], axis=1).T,
        w_in_sn=jnp.concatenate([w_in[:, a:b] for a, b in SAMPLE_SEGS_N], axis=1),
        w_in_st=jnp.concatenate([w_in[:, a:b] for a, b in SAMPLE_SEGS_T], axis=1).T,
        rwkv=rwkv,
        rwkv_t=rwkv_t,
        wuq=wuq,
        wuq_t=wuq.T,
        wuk=jnp.transpose(W["mla_w_uk"][l], (1, 0, 2)).astype(BF16),
        wuv_t=jnp.transpose(W["mla_w_uv"][l], (1, 2, 0)).astype(BF16),
        wuk_t=jnp.transpose(W["mla_w_uk"][l], (1, 2, 0)).astype(BF16),
        wuv=jnp.transpose(W["mla_w_uv"][l], (1, 0, 2)).astype(BF16),
        lam_vecs=jnp.stack([W["diff_lam_q1"][l], W["diff_lam_k1"][l], W["diff_lam_q2"][l], W["diff_lam_k2"][l]]),
        wo=W["w_out"][l].astype(BF16),
        wple=W["w_ple"][l].astype(BF16),
        wg=W["w_ple_gate"][l].astype(BF16),
    )


def _run_prompt(x, p, W, LW, depth):
    b, t, d = x.shape
    m = b * t
    tm = ATTN_TILE
    cos2, sin2 = _rope_tables(jnp.arange(t, dtype=jnp.int32))
    cos2_t, sin2_t = cos2.T, sin2.T
    tile = jnp.arange(ATTN_TILE, dtype=jnp.int32)
    dist = (jnp.arange(3, dtype=jnp.int32)[:, None, None] * ATTN_TILE + tile[None, None, :] - tile[None, :, None])
    bias_tiles = _bias_tiles(W["rel_bias"], dist)
    causal = jnp.stack([jnp.where(dist[0] >= 0, 0.0, NEG_INF).astype(F32), jnp.zeros(dist.shape[1:], F32)])
    uprev0 = jnp.zeros((b, 1, A_IN), F32)
    s0 = jnp.zeros((b, H_A, HEAD_A, HEAD_A), F32)
    segs_n = _pack_segments(PROMPT_SEGS_N)
    segs_t = _pack_segments(PROMPT_SEGS_T)
    h = x.reshape(m, d)
    mla_rows, k_rows, v_rows, wkv_out, shift_out = [], [], [], [], []
    for l in range(depth):
        lw = LW[l]
        u, ga, gb, gc, kc, cq_t, ckv_t, qc_t, kc_t, vc_t, xn_last = _inproj(
            h, W["norm_g"][l], lw["w_in_n"], segs_n, normalize=True, rows_per_seq=t, tm=tm,
            wt_bf16=lw["w_in_t"], segs_t=segs_t, seg_dtypes=PROMPT_DTYPES_N)
        y_a, s_new = _rwkv_chunked(u.reshape(b, t, A_IN), uprev0, ga.reshape(b, t, D_A), s0, lw["rwkv"])
        q_t, rows_pad, rows_t = _mla_prep_t(cq_t, ckv_t, cos2_t, sin2_t, W["mla_q_norm_g"][l], lw["wuq_t"],
                                            W["mla_kv_norm_g"][l], lw["wuk"], tm=tm)
        lam_init = 0.8 - 0.6 * math.exp(-0.3 * l)
        y_b, y_c = _prompt_attn(q_t, rows_pad, rows_t, causal, gb.reshape(b, t, D_B), lw["wuv_t"],
                                qc_t, kc.reshape(b, t, D_C), vc_t, gc.reshape(b, t, D_C), bias_tiles,
                                lw["lam_vecs"], W["diff_subln_g"][l], lam_init)
        h = _outproj(h, y_a.reshape(m, D_A), y_b.reshape(m, D_B), y_c.reshape(m, D_C), p[l].reshape(m, PLE_DIM),
                     lw["wo"], lw["wple"], lw["wg"], W["final_norm_g"], final=(l == depth - 1), tm=tm)
        mla_rows.append(jnp.transpose(rows_t, (0, 2, 1)))
        k_rows.append(jnp.transpose(kc_t.reshape(b, H_C, 2 * DC, t), (0, 3, 1, 2)))
        v_rows.append(jnp.transpose(vc_t.reshape(b, H_C, DV_C, t), (0, 3, 1, 2)))
        wkv_out.append(s_new)
        shift_out.append(xn_last.reshape(b, d))
    return (h.reshape(b, t, d), jnp.stack(mla_rows), jnp.stack(k_rows), jnp.stack(v_rows), jnp.stack(wkv_out),
            jnp.stack(shift_out))


def _run_sample(x, p, state_shift, state_wkv, cache_mla, cache_k, cache_v, page_table, W, LW, depth):
    b, t, d = x.shape
    assert t == 1
    pages_per_seq = page_table.shape[1]
    past_len = pages_per_seq * PAGE_SIZE
    tm = b
    cos2, sin2 = _rope_tables(jnp.full((b,), past_len, dtype=jnp.int32))
    pt_flat = page_table.reshape(-1).astype(jnp.int32)
    step_keys = DECODE_PAGES_PER_CHUNK * PAGE_SIZE
    key_in_step = jnp.arange(step_keys, dtype=jnp.int32)
    dist = jnp.stack([past_len - key_in_step,
                      past_len - (past_len - step_keys + key_in_step),
                      jnp.zeros((step_keys,), jnp.int32)])
    dist = jnp.broadcast_to(dist[:, None, :], (3, 2, step_keys))
    bias = _bias_tiles(W["rel_bias"], dist)
    bias_rows = bias.reshape(3, 2 * H_C, step_keys)
    to_feature_major = lambda c: jnp.transpose(c, (0, 1, 3, 4, 2)).reshape(c.shape[:2] + (D_C, PAGE_SIZE))
    cache_k2 = to_feature_major(cache_k)
    cache_v2 = to_feature_major(cache_v)
    cache_mla_t = jnp.transpose(cache_mla, (0, 1, 3, 2))
    segs_n = _pack_segments(SAMPLE_SEGS_N)
    segs_t = _pack_segments(SAMPLE_SEGS_T)
    h = x.reshape(b, d)
    mla_rows, k_rows, v_rows, wkv_out, shift_out = [], [], [], [], []
    for l in range(depth):
        lw = LW[l]
        cq, ckv, gb, qc, kc, vc, gc, u_t, ga_t, xn = _inproj(
            h, W["norm_g"][l], lw["w_in_sn"], segs_n, normalize=True, rows_per_seq=1, tm=tm,
            wt_bf16=lw["w_in_st"], segs_t=segs_t)
        (uprev_t,) = _inproj(state_shift[l], W["norm_g"][l], None, (), normalize=False, rows_per_seq=1, tm=tm,
                             wt_bf16=lw["w_in_st"][:A_IN], segs_t=(SEG_U,))
        y_a_t, s_new_t = _rwkv_step(u_t[0], uprev_t[0], ga_t[0], jnp.transpose(state_wkv[l], (1, 2, 3, 0)),
                                    lw["rwkv_t"])
        y_a = y_a_t.T
        s_new = jnp.transpose(s_new_t, (3, 0, 1, 2))
        q, rows = _mla_prep(cq, ckv, cos2, sin2, W["mla_q_norm_g"][l], lw["wuq"], W["mla_kv_norm_g"][l],
                            lw["wuk_t"], tm=tm, pos_tiles=1)
        lam_init = 0.8 - 0.6 * math.exp(-0.3 * l)
        y_b, y_c = _decode_attn(pt_flat, jnp.transpose(q, (1, 0, 2)), rows.reshape(b, 1, MLA_W),
                                gb.reshape(b, 1, D_B), lw["wuv"], qc.reshape(b, 1, D_C), kc.reshape(b, 1, D_C),
                                vc.reshape(b, 1, D_C), gc.reshape(b, 1, D_C), bias_rows, lw["lam_vecs"],
                                W["diff_subln_g"][l], cache_mla_t, cache_k2, cache_v2, l, pages_per_seq, lam_init)
        h = _outproj(h, y_a, y_b.reshape(b, D_B), y_c.reshape(b, D_C), p[l].reshape(b, PLE_DIM),
                     lw["wo"], lw["wple"], lw["wg"], W["final_norm_g"], final=(l == depth - 1), tm=tm)
        mla_rows.append(rows.reshape(b, 1, MLA_W))
        k_rows.append(kc.reshape(b, 1, H_C, 2 * DC))
        v_rows.append(vc.reshape(b, 1, H_C, DV_C))
        wkv_out.append(s_new)
        shift_out.append(xn)
    return (h.reshape(b, 1, d), jnp.stack(mla_rows), jnp.stack(k_rows), jnp.stack(v_rows), jnp.stack(wkv_out),
            jnp.stack(shift_out))


def kernel(x_prompt, x_sample, cache_mla, cache_diff_k, cache_diff_v, state_wkv, state_shift, page_table,
           p_prompt, p_sample, norm_g, w_in, mu_shift, rw_w0, rw_w2, rw_a0, rw_a2, rw_k_k, rw_k_a, rw_r_k,
           rw_gn_g, rw_gn_b, mla_q_norm_g, mla_w_uq, mla_kv_norm_g, mla_w_uk, mla_w_uv, diff_lam_q1,
           diff_lam_k1, diff_lam_q2, diff_lam_k2, diff_subln_g, rel_bias, w_out, w_ple, w_ple_gate,
           final_norm_g):
    W = {"norm_g": norm_g, "w_in": w_in, "mu_shift": mu_shift, "rw_w0": rw_w0, "rw_w2": rw_w2, "rw_a0": rw_a0,
         "rw_a2": rw_a2, "rw_k_k": rw_k_k, "rw_k_a": rw_k_a, "rw_r_k": rw_r_k, "rw_gn_g": rw_gn_g,
         "rw_gn_b": rw_gn_b, "mla_q_norm_g": mla_q_norm_g, "mla_w_uq": mla_w_uq, "mla_kv_norm_g": mla_kv_norm_g,
         "mla_w_uk": mla_w_uk, "mla_w_uv": mla_w_uv, "diff_lam_q1": diff_lam_q1, "diff_lam_k1": diff_lam_k1,
         "diff_lam_q2": diff_lam_q2, "diff_lam_k2": diff_lam_k2, "diff_subln_g": diff_subln_g,
         "rel_bias": rel_bias, "w_out": w_out, "w_ple": w_ple, "w_ple_gate": w_ple_gate,
         "final_norm_g": final_norm_g}
    depth = w_in.shape[0]
    LW = [_layer_weights(l, W) for l in range(depth)]
    y_p, mla_p, dk_p, dv_p, wkv_p, sh_p = _run_prompt(x_prompt, p_prompt, W, LW, depth)
    y_s, mla_s, dk_s, dv_s, wkv_s, sh_s = _run_sample(x_sample, p_sample, state_shift, state_wkv, cache_mla,
                                                      cache_diff_k, cache_diff_v, page_table, W, LW, depth)
    return (y_p, y_s, mla_p, mla_s, dk_p, dk_s, dv_p, dv_s, wkv_p, wkv_s, sh_p, sh_s)
```

```python
import functools
import math

import jax
import jax.numpy as jnp
from jax import lax
from jax.experimental import pallas as pl
from jax.experimental.pallas import tpu as pltpu

F32 = jnp.float32
BF16 = jnp.bfloat16

LANES = 128
SUBLANES = 8
VMEM_LIMIT_BYTES = 56 * 1024 * 1024

D_MODEL = 1024
HEAD_A = 64
D_A = 512
H_A = D_A // HEAD_A
W_LORA = 64
A_LORA = 64
A_IN = 3 * D_A + W_LORA + A_LORA
D_B = 256
DV_B = 64
H_B = D_B // DV_B
NOPE_B = 64
ROPE_B = 32
Q_RANK = 256
KV_RANK = 128
MLA_W = KV_RANK + ROPE_B
ROPE_THETA = 10000.0
D_C = 256
DV_C = 64
H_C = D_C // DV_C
DC = DV_C // 2
NUM_BUCKETS = 32
MAX_DISTANCE = 128
PLE_DIM = 256
PAGE_SIZE = 128
NEG_INF = -1e30
EPS = 1e-6
GN_EPS = 64e-5
SUBLN_EPS = 1e-5
MLA_SCALE = (NOPE_B + ROPE_B) ** -0.5
DIFF_SCALE = DC ** -0.5

SEG_U = (0, A_IN)
SEG_GA = (A_IN, A_IN + D_A)
SEG_CQ = (SEG_GA[1], SEG_GA[1] + Q_RANK)
SEG_CKV = (SEG_CQ[1], SEG_CQ[1] + 2 * LANES)
SEG_GB = (SEG_CKV[1], SEG_CKV[1] + D_B)
SEG_QC = (SEG_GB[1], SEG_GB[1] + D_C)
SEG_KC = (SEG_QC[1], SEG_QC[1] + D_C)
SEG_VC = (SEG_KC[1], SEG_KC[1] + D_C)
SEG_GC = (SEG_VC[1], SEG_VC[1] + D_C)
IN_COLS_PERM = SEG_GC[1]


def _pack_segments(segs):
    out, pos = [], 0
    for a, b in segs:
        out.append((pos, pos + b - a))
        pos += b - a
    return tuple(out)


PROMPT_SEGS_N = (SEG_U, SEG_GA, SEG_GB, SEG_GC, SEG_KC)
PROMPT_SEGS_T = (SEG_CQ, SEG_CKV, SEG_QC, SEG_KC, SEG_VC)
PROMPT_DTYPES_N = (F32, F32, F32, F32, BF16)
SAMPLE_SEGS_N = (SEG_CQ, SEG_CKV, SEG_GB, SEG_QC, SEG_KC, SEG_VC, SEG_GC)
SAMPLE_SEGS_T = (SEG_U, SEG_GA)

RWKV_CHUNK = 64
RWKV_BLOCK = 256
ATTN_TILE = 256
OUTPROJ_TILE = 512
DECODE_PAGES_PER_CHUNK = 64


def _cparams(semantics):
    return pltpu.CompilerParams(dimension_semantics=semantics, vmem_limit_bytes=VMEM_LIMIT_BYTES)


def _full(shape):
    n = len(shape)
    return pl.BlockSpec(shape, lambda *_: (0,) * n)


def _sigmoid(x):
    return 1.0 / (1.0 + jnp.exp(-x))


def _silu(x):
    return x * _sigmoid(x)


def _rms(x, g, eps):
    return x * lax.rsqrt(jnp.mean(x * x, axis=-1, keepdims=True) + eps) * g


def _dot(a, b, **kw):
    return jnp.dot(a, b, preferred_element_type=F32, **kw)


def _dot_nt(a, b, **kw):
    return lax.dot_general(a, b, (((1,), (1,)), ((), ())), preferred_element_type=F32, **kw)


def _dot_tn(a, b, **kw):
    return lax.dot_general(a, b, (((0,), (0,)), ((), ())), preferred_element_type=F32, **kw)


def _split3(x):
    hi = x.astype(BF16)
    rest = x - hi.astype(F32)
    mid = rest.astype(BF16)
    lo = (rest - mid.astype(F32)).astype(BF16)
    return hi, mid, lo


def _dot_split_rhs(a_exact, b):
    hi, mid, lo = _split3(b)
    return _dot(a_exact, hi) + (_dot(a_exact, mid) + _dot(a_exact, lo))


def _dot_split_lhs(a, b_exact):
    hi, mid, lo = _split3(a)
    return _dot(hi, b_exact) + (_dot(mid, b_exact) + _dot(lo, b_exact))


def _mm(a, b):
    return _dot(a.astype(BF16), b.astype(BF16))


def _mm_nt(a, b):
    return _dot_nt(a.astype(BF16), b.astype(BF16))


def _mm_tn(a, b):
    return _dot_tn(a.astype(BF16), b.astype(BF16))


def _inproj_kernel(h_ref, g_ref, w_ref, wt_ref, *out_refs, normalize, segs, segs_t, emit_xn):
    x = h_ref[...]
    xn = _rms(x, g_ref[...], EPS) if normalize else x
    xb = xn.astype(BF16)
    for o_ref, (a, b) in zip(out_refs, segs):
        o_ref[...] = _dot(xb, w_ref[:, a:b]).astype(o_ref.dtype)
    for o_ref, (a, b) in zip(out_refs[len(segs):], segs_t):
        o_ref[...] = _dot_nt(wt_ref[a:b, :], xb)
    n_proj = len(segs) + len(segs_t)
    if emit_xn == "last_row":
        rows = x.shape[0]
        out_refs[n_proj][...] = xn[rows - 1:rows, :]
    elif emit_xn == "all":
        out_refs[n_proj][...] = xn


def _inproj(h2d, norm_g, w_bf16, segs, *, normalize, rows_per_seq, tm, wt_bf16=None, segs_t=(), seg_dtypes=None):
    m, d = h2d.shape
    assert m % tm == 0 and rows_per_seq % tm == 0 or rows_per_seq == 1
    tiles_per_seq = max(rows_per_seq // tm, 1)
    t_cols = rows_per_seq if rows_per_seq > 1 else m
    t_tiles = t_cols // tm
    if w_bf16 is None:
        w_bf16 = jnp.zeros((d, LANES), BF16)
    if wt_bf16 is None:
        wt_bf16 = jnp.zeros((SUBLANES, d), BF16)
    seg_dtypes = seg_dtypes or (F32,) * len(segs)
    out_shapes = [jax.ShapeDtypeStruct((m, b - a), dt) for (a, b), dt in zip(segs, seg_dtypes)]
    out_specs = [pl.BlockSpec((tm, b - a), lambda i: (i, 0)) for a, b in segs]
    for a, b in segs_t:
        out_shapes.append(jax.ShapeDtypeStruct((m // t_cols, b - a, t_cols), F32))
        out_specs.append(pl.BlockSpec((None, b - a, tm), lambda i: (i // t_tiles, 0, i % t_tiles)))
    emit_xn = None
    if normalize:
        if rows_per_seq == 1:
            emit_xn = "all"
            out_shapes.append(jax.ShapeDtypeStruct((m, d), F32))
            out_specs.append(pl.BlockSpec((tm, d), lambda i: (i, 0)))
        else:
            emit_xn = "last_row"
            out_shapes.append(jax.ShapeDtypeStruct((m // rows_per_seq, 1, d), F32))
            out_specs.append(pl.BlockSpec((None, 1, d), lambda i: (i // tiles_per_seq, 0, 0)))
    kern = functools.partial(_inproj_kernel, normalize=normalize, segs=segs, segs_t=segs_t, emit_xn=emit_xn)
    return pl.pallas_call(
        kern,
        grid=(m // tm,),
        in_specs=[pl.BlockSpec((tm, d), lambda i: (i, 0)), _full((1, d)), _full(w_bf16.shape), _full(wt_bf16.shape)],
        out_specs=out_specs,
        out_shape=out_shapes,
        compiler_params=_cparams(("arbitrary",)),
        name="inproj",
    )(h2d, norm_g.reshape(1, d), w_bf16, wt_bf16)


def _outproj_kernel(h_ref, ya_ref, yb_ref, yc_ref, p_ref, wo_ref, wple_ref, wg_ref, fng_ref, o_ref, *, final):
    mixed = (_dot(ya_ref[...], wo_ref[0:D_A, :])
             + _dot(yb_ref[...], wo_ref[D_A:D_A + D_B, :])
             + _dot(yc_ref[...], wo_ref[D_A + D_B:, :]))
    h2 = h_ref[...] + mixed
    ple = _dot(p_ref[...].astype(BF16), wple_ref[...])
    gate = _sigmoid(_dot(h2.astype(BF16), wg_ref[...]))
    h3 = h2 + ple * gate
    o_ref[...] = _rms(h3, fng_ref[...], EPS) if final else h3


def _outproj(h2d, ya, yb, yc, p2d, wo, wple, wg, final_g, *, final, tm):
    m, d = h2d.shape
    row = lambda w: pl.BlockSpec((tm, w), lambda i: (i, 0))
    return pl.pallas_call(
        functools.partial(_outproj_kernel, final=final),
        grid=(m // tm,),
        in_specs=[row(d), row(D_A), row(D_B), row(D_C), row(PLE_DIM),
                  _full(wo.shape), _full(wple.shape), _full(wg.shape), _full((1, d))],
        out_specs=row(d),
        out_shape=jax.ShapeDtypeStruct((m, d), F32),
        compiler_params=_cparams(("arbitrary",)),
        name="outproj",
    )(h2d, ya, yb, yc, p2d, wo, wple, wg, final_g.reshape(1, d))


def _rwkv_prep(um, w0, w2, a0, a2, k_k, k_a):
    r = um[:, 0:D_A]
    k = um[:, D_A:2 * D_A]
    v = um[:, 2 * D_A:3 * D_A]
    w_lo = um[:, 3 * D_A:3 * D_A + W_LORA]
    a_lo = um[:, 3 * D_A + W_LORA:A_IN]
    wl = w0 + _mm(jnp.tanh(w_lo), w2)
    neg = -wl
    softplus = jnp.maximum(neg, 0.0) + jnp.log(1.0 + jnp.exp(-jnp.abs(neg)))
    w = -softplus - 0.5
    log_decay = -jnp.exp(w)
    a = _sigmoid(a0 + _mm(a_lo, a2))
    kk = k * k_k
    k = k * (1.0 + (a - 1.0) * k_a)
    return r, k, v, kk, a, log_decay


def _rwkv_head_out(y, r_h, k_h, v_h, rk_h, gng_h, gnb_h, gate_h):
    mu = jnp.mean(y, axis=-1, keepdims=True)
    var = jnp.mean(jnp.square(y - mu), axis=-1, keepdims=True)
    yn = (y - mu) * lax.rsqrt(var + GN_EPS) * gng_h + gnb_h
    bonus = jnp.sum(r_h * k_h * rk_h, axis=-1, keepdims=True) * v_h
    return (yn + bonus) * _silu(gate_h)


def _rwkv_chunk_kernel(u_ref, uprev0_ref, ga_ref, s0_ref, ones_ref, mu_ref, w0_ref, w2_ref, a0_ref, a2_ref,
                       kk_ref, ka_ref, rk_ref, gng_ref, gnb_ref, y_ref, s_ref, prev_ref):
    step = pl.program_id(1)
    R = u_ref.shape[0]
    C = RWKV_CHUNK
    subs = range(R // C)

    @pl.when(step == 0)
    def _():
        prev_ref[...] = uprev0_ref[...]
        s_ref[...] = s0_ref[...]

    u = u_ref[...]
    row = lax.broadcasted_iota(jnp.int32, (R, 1), 0)
    u_prev = jnp.where(row == 0, prev_ref[...], pltpu.roll(u, 1, axis=0))
    prev_ref[...] = u[R - 1:R, :]
    um = u + mu_ref[...] * (u_prev - u)
    r, k, v, kk, a, log_decay = _rwkv_prep(um, w0_ref[...], w2_ref[...], a0_ref[...], a2_ref[...],
                                           kk_ref[...], ka_ref[...])

    ri = lax.broadcasted_iota(jnp.int32, (R, R), 0)
    rj = lax.broadcasted_iota(jnp.int32, (R, R), 1)
    same_chunk = (ri // C) == (rj // C)
    cs = _dot_split_rhs((same_chunk & (rj <= ri)).astype(BF16), log_decay)
    cs_last = [cs[(sb + 1) * C - 1:(sb + 1) * C, :] for sb in subs]
    cs_end = jnp.concatenate([jnp.broadcast_to(x, (C, x.shape[1])) for x in cs_last], axis=0)
    p_end = [jnp.exp(x) for x in cs_last]
    e_inv = jnp.exp(-cs)
    e_rem = jnp.exp(cs_end - cs)
    kk_n = kk / jnp.maximum(jnp.sqrt(_dot_split_lhs(kk * kk, ones_ref[...])), 1e-12)
    b_f = kk_n * a
    a_t = (-kk_n * jnp.exp(cs - log_decay)).astype(BF16)
    r_t = (r * jnp.exp(cs)).astype(BF16)
    b_t = (b_f * e_inv).astype(BF16)
    k_t = (k * e_inv).astype(BF16)
    b_end = (b_f * e_rem).astype(BF16)
    k_end = (k * e_rem).astype(BF16)
    v_b = v.astype(BF16)
    ga = ga_ref[...]

    t2 = lax.broadcasted_iota(jnp.int32, (C, 2 * C), 0)
    j2 = lax.broadcasted_iota(jnp.int32, (C, 2 * C), 1)
    j2 = jnp.where(j2 >= C, j2 - C, j2)
    strict2 = j2 < t2
    incl2 = j2 <= t2
    ti = lax.broadcasted_iota(jnp.int32, (C, C), 0)
    tj = lax.broadcasted_iota(jnp.int32, (C, C), 1)
    eye = (ti == tj).astype(F32)
    zeros = jnp.zeros((C, HEAD_A), BF16)

    units = [(sb, h) for sb in subs for h in range(H_A)]
    blk = lambda x, sb, h: x[sb * C:(sb + 1) * C, h * HEAD_A:(h + 1) * HEAD_A]
    a_h = {un: blk(a_t, *un) for un in units}
    r_h = {un: blk(r_t, *un) for un in units}
    v_h = {un: blk(v_b, *un) for un in units}
    gram = {un: _dot_nt(jnp.concatenate([a_h[un], r_h[un]], axis=0),
                        jnp.concatenate([blk(b_t, *un), blk(k_t, *un)], axis=0)) for un in units}
    l_top = {un: jnp.where(strict2, gram[un][0:C], 0.0) for un in units}
    m_bot = {un: jnp.where(incl2, gram[un][C:], 0.0).astype(BF16) for un in units}
    lv = {un: _dot(l_top[un].astype(BF16), jnp.concatenate([zeros, v_h[un]], axis=0)) for un in units}

    l_ab = {un: l_top[un][:, 0:C] for un in units}
    inv = {un: eye + l_ab[un] for un in units}
    pw = {un: l_ab[un].astype(BF16) for un in units}
    for _ in range(int(math.log2(C)) - 1):
        pw = {un: _mm(pw[un], pw[un]).astype(BF16) for un in units}
        inv = {un: inv[un] + _mm(inv[un], pw[un]) for un in units}
    inv = {un: inv[un].astype(BF16) for un in units}
    w_mat = {un: _mm(inv[un], a_h[un]).astype(BF16) for un in units}
    u_v = {un: _mm(inv[un], lv[un]) for un in units}

    state = [s_ref[h] for h in range(H_A)]
    for sb in subs:
        heads = [(sb, h) for h in range(H_A)]
        state_b = [x.astype(BF16) for x in state]
        uv = [jnp.concatenate([(_dot_nt(w_mat[un], state_b[un[1]]) + u_v[un]).astype(BF16), v_h[un]], axis=0)
              for un in heads]
        y = [_dot_nt(r_h[un], state_b[un[1]]) + _dot(m_bot[un], uv[un[1]]) for un in heads]
        state = [state[h] * p_end[sb][:, h * HEAD_A:(h + 1) * HEAD_A]
                 + _dot_tn(uv[h], jnp.concatenate([blk(b_end, sb, h), blk(k_end, sb, h)], axis=0))
                 for h in range(H_A)]
        rows = slice(sb * C, (sb + 1) * C)
        for h in range(H_A):
            hs = slice(h * HEAD_A, (h + 1) * HEAD_A)
            out = _rwkv_head_out(y[h], r[rows, hs], k[rows, hs], v[rows, hs], rk_ref[:, hs], gng_ref[:, hs],
                                 gnb_ref[:, hs], ga[rows, hs])
            y_ref[rows, hs] = out.astype(y_ref.dtype)
    for h in range(H_A):
        s_ref[h] = state[h]


def _rwkv_chunked(u, uprev0, ga, s0, params):
    b, t, _ = u.shape
    c = RWKV_BLOCK
    assert t % c == 0
    head_of_lane = jnp.arange(D_A, dtype=jnp.int32) // HEAD_A
    head_ones = (head_of_lane[:, None] == head_of_lane[None, :]).astype(BF16)
    tok = lambda w: pl.BlockSpec((None, c, w), lambda i, j: (i, j, 0))
    state = pl.BlockSpec((None, H_A, HEAD_A, HEAD_A), lambda i, j: (i, 0, 0, 0))
    return pl.pallas_call(
        _rwkv_chunk_kernel,
        grid=(b, t // c),
        in_specs=[tok(A_IN), pl.BlockSpec((None, 1, A_IN), lambda i, j: (i, 0, 0)), tok(D_A), state,
                  _full(head_ones.shape)] + [_full(p.shape) for p in params],
        out_specs=[tok(D_A), state],
        out_shape=[jax.ShapeDtypeStruct((b, t, D_A), BF16), jax.ShapeDtypeStruct(s0.shape, F32)],
        scratch_shapes=[pltpu.VMEM((1, A_IN), F32)],
        compiler_params=_cparams(("parallel", "arbitrary")),
        name="rwkv_chunked",
    )(u, uprev0, ga, s0, head_ones, *params)


def _rwkv_step_kernel(r_ref, k_ref, v_ref, lo_ref, rp_ref, kp_ref, vp_ref, lop_ref, ga_ref, s0_ref,
                      mur_ref, muk_ref, muv_ref, mulo_ref, w0_ref, w2t_ref, a0_ref, a2t_ref, kk_ref, ka_ref,
                      rk_ref, gng_ref, gnb_ref, y_ref, s_ref, y_scr):
    mix = lambda x_ref, p_ref, mu_ref: x_ref[...] + mu_ref[...] * (p_ref[...] - x_ref[...])
    r = mix(r_ref, rp_ref, mur_ref)
    k = mix(k_ref, kp_ref, muk_ref)
    v = mix(v_ref, vp_ref, muv_ref)
    lo = mix(lo_ref, lop_ref, mulo_ref)
    neg = -(w0_ref[...] + _mm(w2t_ref[...], jnp.tanh(lo[0:W_LORA])))
    w = -(jnp.maximum(neg, 0.0) + jnp.log(1.0 + jnp.exp(-jnp.abs(neg)))) - 0.5
    decay = jnp.exp(-jnp.exp(w))
    a = _sigmoid(a0_ref[...] + _mm(a2t_ref[...], lo[W_LORA:]))
    kk = k * kk_ref[...]
    kk = kk / jnp.maximum(jnp.sqrt(jnp.sum(kk * kk, axis=0, keepdims=True)), 1e-12)
    k = k * (1.0 + (a - 1.0) * ka_ref[...])
    b = kk * a
    nkk = -kk
    for i in range(HEAD_A):
        s = s0_ref[i]
        sa = jnp.sum(s * nkk, axis=0, keepdims=True)
        s_new = s * decay + sa * b + v[i:i + 1, :] * k
        s_ref[i] = s_new
        y_scr[i:i + 1, :] = jnp.sum(s_new * r, axis=0, keepdims=True)
    y = y_scr[...]
    mu = jnp.mean(y, axis=0, keepdims=True)
    var = jnp.mean(jnp.square(y - mu), axis=0, keepdims=True)
    yn = (y - mu) * lax.rsqrt(var + GN_EPS) * gng_ref[...] + gnb_ref[...]
    bonus = jnp.sum(r * k * rk_ref[...], axis=0, keepdims=True) * v
    y_ref[...] = ((yn + bonus) * _silu(ga_ref[...])).astype(y_ref.dtype)


def _rwkv_step(u_t, uprev_t, ga_t, s0_t, params_t):
    nb = u_t.shape[1]
    mu, w0, w2t, a0, a2t, k_k, k_a, r_k, gn_g, gn_b = params_t
    n_head_blocks = D_A // HEAD_A
    lora_block = 3 * D_A // (W_LORA + A_LORA)
    feat = lambda off: pl.BlockSpec((HEAD_A, nb), lambda h: (off * n_head_blocks + h, 0))
    lora = pl.BlockSpec((W_LORA + A_LORA, nb), lambda h: (lora_block, 0))
    col = lambda off: pl.BlockSpec((HEAD_A, 1), lambda h: (off * n_head_blocks + h, 0))
    lora_col = pl.BlockSpec((W_LORA + A_LORA, 1), lambda h: (lora_block, 0))
    head_rows = lambda w: pl.BlockSpec((HEAD_A, w), lambda h: (h, 0))
    state = pl.BlockSpec((None, HEAD_A, HEAD_A, nb), lambda h: (h, 0, 0, 0))
    return pl.pallas_call(
        _rwkv_step_kernel,
        grid=(H_A,),
        in_specs=[feat(0), feat(1), feat(2), lora, feat(0), feat(1), feat(2), lora, head_rows(nb), state,
                  col(0), col(1), col(2), lora_col, col(0), head_rows(W_LORA), col(0), head_rows(A_LORA),
                  col(0), col(0), col(0), col(0), col(0)],
        out_specs=[head_rows(nb), state],
        out_shape=[jax.ShapeDtypeStruct((D_A, nb), BF16), jax.ShapeDtypeStruct(s0_t.shape, F32)],
        scratch_shapes=[pltpu.VMEM((HEAD_A, nb), F32)],
        compiler_params=_cparams(("parallel",)),
        name="rwkv_step",
    )(u_t, u_t, u_t, u_t, uprev_t, uprev_t, uprev_t, uprev_t, ga_t, s0_t,
      mu, mu, mu, mu, w0, w2t, a0, a2t, k_k, k_a, r_k, gn_g, gn_b)


def _mla_prep_kernel(cq_ref, ckv_ref, cos_ref, sin_ref, qg_ref, wuq_ref, kvg_ref, wukt_ref, q_ref, rows_ref):
    cos2 = cos_ref[...]
    sin2 = sin_ref[...]
    qn = _rms(cq_ref[...], qg_ref[...], EPS).astype(BF16)
    q = _dot(qn, wuq_ref[...])
    for h in range(H_B):
        qh = q[:, h * LANES:(h + 1) * LANES]
        q_lat = _dot(qh[:, :NOPE_B].astype(BF16), wukt_ref[h])
        q_rope = qh[:, NOPE_B:NOPE_B + ROPE_B] * cos2 + qh[:, NOPE_B + ROPE_B:] * sin2
        q_ref[h, :, 0:KV_RANK] = (q_lat * MLA_SCALE).astype(q_ref.dtype)
        q_ref[h, :, KV_RANK:MLA_W] = (q_rope * MLA_SCALE).astype(q_ref.dtype)
    ckv = ckv_ref[...]
    rows_ref[:, 0:KV_RANK] = _rms(ckv[:, 0:KV_RANK], kvg_ref[...], EPS)
    rows_ref[:, KV_RANK:MLA_W] = (ckv[:, KV_RANK:KV_RANK + ROPE_B] * cos2
                                  + ckv[:, KV_RANK + ROPE_B:KV_RANK + 2 * ROPE_B] * sin2)


def _mla_prep(cq, ckv, cos2, sin2, q_norm_g, wuq_ext, kv_norm_g, wuk_t, *, tm, pos_tiles):
    m = cq.shape[0]
    row = lambda w: pl.BlockSpec((tm, w), lambda i: (i, 0))
    pos = pl.BlockSpec((tm, ROPE_B), lambda i: (i % pos_tiles, 0))
    return pl.pallas_call(
        _mla_prep_kernel,
        grid=(m // tm,),
        in_specs=[row(Q_RANK), row(2 * LANES), pos, pos, _full((1, Q_RANK)), _full(wuq_ext.shape),
                  _full((1, KV_RANK)), _full(wuk_t.shape)],
        out_specs=[pl.BlockSpec((H_B, tm, MLA_W), lambda i: (0, i, 0)), row(MLA_W)],
        out_shape=[jax.ShapeDtypeStruct((H_B, m, MLA_W), BF16), jax.ShapeDtypeStruct((m, MLA_W), F32)],
        compiler_params=_cparams(("parallel",)),
        name="mla_prep",
    )(cq, ckv, cos2, sin2, q_norm_g.reshape(1, Q_RANK), wuq_ext, kv_norm_g.reshape(1, KV_RANK), wuk_t)


def _mla_out(o_lat, wuv_ref, gb, o_ref, rows_per_head):
    for h in range(H_B):
        o_h = _dot(o_lat[h * rows_per_head:(h + 1) * rows_per_head].astype(BF16), wuv_ref[h])
        hs = slice(h * DV_B, (h + 1) * DV_B)
        o_ref[:, hs] = (o_h * _silu(gb[:, hs])).astype(o_ref.dtype)


def _mla_prep_t_kernel(cq_ref, ckv_ref, cos_ref, sin_ref, qg_ref, wuqt_ref, kvg_ref, wuk_ref,
                       q_ref, rows_ref, rowst_ref):
    tm = cq_ref.shape[1]
    cos2 = cos_ref[...]
    sin2 = sin_ref[...]
    cq = cq_ref[...]
    qn = (cq * lax.rsqrt(jnp.mean(cq * cq, axis=0, keepdims=True) + EPS) * qg_ref[...]).astype(BF16)
    q = _dot(wuqt_ref[...], qn)
    pad = jnp.zeros((2 * LANES - MLA_W, tm), q_ref.dtype)
    for h in range(H_B):
        qh = q[h * LANES:(h + 1) * LANES]
        q_lat = _dot(wuk_ref[h], qh[0:NOPE_B].astype(BF16))
        q_rope = qh[NOPE_B:NOPE_B + ROPE_B] * cos2 + qh[NOPE_B + ROPE_B:] * sin2
        cols = slice(h * tm, (h + 1) * tm)
        q_ref[0:KV_RANK, cols] = (q_lat * MLA_SCALE).astype(q_ref.dtype)
        q_ref[KV_RANK:MLA_W, cols] = (q_rope * MLA_SCALE).astype(q_ref.dtype)
        q_ref[MLA_W:, cols] = pad
    ckv = ckv_ref[...]
    c = ckv[0:KV_RANK]
    cn = c * lax.rsqrt(jnp.mean(c * c, axis=0, keepdims=True) + EPS) * kvg_ref[...]
    kr = ckv[KV_RANK:KV_RANK + ROPE_B] * cos2 + ckv[KV_RANK + ROPE_B:KV_RANK + 2 * ROPE_B] * sin2
    rowst_ref[0:KV_RANK, :] = cn
    rowst_ref[KV_RANK:MLA_W, :] = kr
    rows_t = jnp.concatenate([cn, kr, jnp.zeros((2 * LANES - MLA_W, tm), F32)], axis=0)
    rows_ref[...] = rows_t.T.astype(rows_ref.dtype)


def _mla_prep_t(cq_t, ckv_t, cos2_t, sin2_t, q_norm_g, wuq_ext_t, kv_norm_g, wuk, *, tm):
    b, _, t = cq_t.shape
    nt = t // tm
    blk = lambda w: pl.BlockSpec((None, w, tm), lambda bi, i: (bi, 0, i))
    pos = pl.BlockSpec((ROPE_B, tm), lambda bi, i: (0, i))
    return pl.pallas_call(
        _mla_prep_t_kernel,
        grid=(b, nt),
        in_specs=[blk(Q_RANK), blk(2 * LANES), pos, pos, _full((Q_RANK, 1)), _full(wuq_ext_t.shape),
                  _full((KV_RANK, 1)), _full(wuk.shape)],
        out_specs=[pl.BlockSpec((None, 2 * LANES, H_B * tm), lambda bi, i: (bi * nt + i, 0, 0)),
                   pl.BlockSpec((None, tm, 2 * LANES), lambda bi, i: (bi, i, 0)),
                   blk(MLA_W)],
        out_shape=[jax.ShapeDtypeStruct((b * nt, 2 * LANES, H_B * tm), BF16),
                   jax.ShapeDtypeStruct((b, t, 2 * LANES), BF16),
                   jax.ShapeDtypeStruct((b, MLA_W, t), F32)],
        compiler_params=_cparams(("parallel", "parallel")),
        name="mla_prep_t",
    )(cq_t, ckv_t, cos2_t, sin2_t, q_norm_g.reshape(Q_RANK, 1), wuq_ext_t, kv_norm_g.reshape(KV_RANK, 1), wuk)


def _softmax_update_t(s, m_ref, l_ref, cols):
    m_old = m_ref[:, cols]
    m_new = jnp.maximum(m_old, jnp.max(s, axis=0, keepdims=True))
    alpha = jnp.exp(m_old - m_new)
    p = jnp.exp(s - m_new)
    l_ref[:, cols] = alpha * l_ref[:, cols] + jnp.sum(p, axis=0, keepdims=True)
    m_ref[:, cols] = m_new
    return alpha, p


def _bias_kernel(rb_ref, dist_ref, o_ref):
    dist = dist_ref[...]
    n = jnp.maximum(dist, 0)
    max_exact = NUM_BUCKETS // 2
    n_safe = jnp.maximum(n, max_exact).astype(F32)
    large = max_exact + (jnp.log(n_safe / max_exact) / math.log(MAX_DISTANCE / max_exact)
                         * (NUM_BUCKETS - max_exact)).astype(jnp.int32)
    large = jnp.minimum(large, NUM_BUCKETS - 1)
    bucket = jnp.where(n < max_exact, n, large)
    for h in range(H_C):
        bias = jnp.zeros(dist.shape, F32)
        for kb in range(NUM_BUCKETS):
            bias = jnp.where(bucket == kb, rb_ref[kb * H_C + h], bias)
        o_ref[h] = jnp.where(dist >= 0, bias, NEG_INF)


def _bias_tiles(rel_bias, dist):
    g, r, c = dist.shape
    grid_spec = pltpu.PrefetchScalarGridSpec(
        num_scalar_prefetch=1,
        grid=(g,),
        in_specs=[pl.BlockSpec((None, r, c), lambda i, rb: (i, 0, 0))],
        out_specs=pl.BlockSpec((None, H_C, r, c), lambda i, rb: (i, 0, 0, 0)),
    )
    return pl.pallas_call(
        _bias_kernel,
        grid_spec=grid_spec,
        out_shape=jax.ShapeDtypeStruct((g, H_C, r, c), F32),
        compiler_params=_cparams(("arbitrary",)),
        name="rel_bias_tiles",
    )(rel_bias.reshape(-1), dist)


def _diff_lambda(lam_ref, lam_init):
    lam = lam_ref[...]
    e1 = jnp.exp(jnp.sum(lam[0:1] * lam[1:2], axis=-1, keepdims=True))
    e2 = jnp.exp(jnp.sum(lam[2:3] * lam[3:4], axis=-1, keepdims=True))
    return e1 - e2 + lam_init


def _diff_queries(qc):
    lane = lax.broadcasted_iota(jnp.int32, qc.shape, 1)
    qs = qc * DIFF_SCALE
    groups = []
    for h in range(H_C):
        for c in range(2):
            lo = h * DV_C + c * DC
            groups.append(jnp.where((lane >= lo) & (lane < lo + DC), qs, 0.0))
    return jnp.concatenate(groups, axis=0)


def _diff_out(acc, l, lam, lam_init, sg, gc, o_ref, rows):
    for h in range(H_C):
        hs = slice(h * DV_C, (h + 1) * DV_C)
        r1 = slice((2 * h) * rows, (2 * h + 1) * rows)
        r2 = slice((2 * h + 1) * rows, (2 * h + 2) * rows)
        o = acc[r1, hs] / l[r1] - lam * (acc[r2, hs] / l[r2])
        o = _rms(o, sg, SUBLN_EPS) * (1.0 - lam_init)
        o_ref[:, hs] = (o * _silu(gc[:, hs])).astype(o_ref.dtype)


def _prompt_attn_kernel(pi_ref, pj_ref,
                        q_ref, k_ref, ct_ref, mask_ref, gb_ref, wuvt_ref,
                        qc_ref, kc_ref, vt_ref, gc_ref, bias_ref, lam_ref, sg_ref,
                        ob_ref, oc_ref,
                        m_ref, l_ref, acc_ref, qbd_ref, md_ref, ld_ref, accd_ref, *, lam_init):
    step = pl.program_id(1)
    i = pi_ref[step]
    j = pj_ref[step]
    tq = ob_ref.shape[0]
    nblk = 2 * H_C

    @pl.when(j == 0)
    def _():
        m_ref[...] = jnp.full_like(m_ref, NEG_INF)
        l_ref[...] = jnp.zeros_like(l_ref)
        acc_ref[...] = jnp.zeros_like(acc_ref)
        q = qc_ref[...] * DIFF_SCALE
        feat = lax.broadcasted_iota(jnp.int32, q.shape, 0)
        for blk in range(nblk):
            lo = (blk // 2) * DV_C + (blk % 2) * DC
            qbd_ref[:, blk * tq:(blk + 1) * tq] = jnp.where((feat >= lo) & (feat < lo + DC), q, 0.0).astype(BF16)
        md_ref[...] = jnp.full_like(md_ref, NEG_INF)
        ld_ref[...] = jnp.zeros_like(ld_ref)
        accd_ref[...] = jnp.zeros_like(accd_ref)

    k = k_ref[...].astype(BF16)
    ct = ct_ref[...].astype(BF16)
    mask = mask_ref[jnp.minimum(i - j, 1)]
    heads = range(H_B)
    cols = [slice(h * tq, (h + 1) * tq) for h in heads]
    kc = kc_ref[...].astype(BF16)
    vt = vt_ref[...].astype(BF16)
    tile = jnp.minimum(i - j, 2)
    blocks = range(nblk)
    dcols = [slice(blk * tq, (blk + 1) * tq) for blk in blocks]
    s_b = [_dot(k, q_ref[:, cols[h]]) + mask for h in heads]
    s_d = [_dot(kc, qbd_ref[:, dcols[blk]]) + bias_ref[tile, blk // 2] for blk in blocks]
    ap_b = [_softmax_update_t(s_b[h], m_ref, l_ref, cols[h]) for h in heads]
    ap_d = [_softmax_update_t(s_d[blk], md_ref, ld_ref, dcols[blk]) for blk in blocks]
    pv_b = [_dot(ct, ap_b[h][1].astype(BF16)) for h in heads]
    pv_d = [_dot(vt[(blk // 2) * DV_C:(blk // 2 + 1) * DV_C], ap_d[blk][1].astype(BF16)) for blk in blocks]
    for h in heads:
        acc_ref[:, cols[h]] = ap_b[h][0] * acc_ref[:, cols[h]] + pv_b[h]
    for blk in blocks:
        accd_ref[blk] = ap_d[blk][0] * accd_ref[blk] + pv_d[blk]

    @pl.when(j == i)
    def _():
        outs = []
        for h in heads:
            o_lat = (acc_ref[:, cols[h]] / l_ref[:, cols[h]]).astype(BF16)
            outs.append(_dot(wuvt_ref[h], o_lat))
        o = jnp.concatenate(outs, axis=0).T
        ob_ref[...] = (o * _silu(gb_ref[...])).astype(ob_ref.dtype)
        lam = _diff_lambda(lam_ref, lam_init)
        outs = []
        for h in range(H_C):
            o = (accd_ref[2 * h] / ld_ref[:, dcols[2 * h]]
                 - lam * (accd_ref[2 * h + 1] / ld_ref[:, dcols[2 * h + 1]]))
            o = o * lax.rsqrt(jnp.mean(o * o, axis=0, keepdims=True) + SUBLN_EPS) * sg_ref[...]
            outs.append(o * (1.0 - lam_init))
        o = jnp.concatenate(outs, axis=0).T
        oc_ref[...] = (o * _silu(gc_ref[...])).astype(oc_ref.dtype)


def _prompt_attn(q_t, rows_pad, rows_t, mask, gb, wuv_t, qc_t, kc, vc_t, gc, bias_tiles, lam_vecs, subln_g, lam_init):
    b, t, _ = kc.shape
    tq = ATTN_TILE
    nq = t // tq
    pairs = [(i, j) for i in range(nq) for j in range(i + 1)]
    pair_i = jnp.asarray([p[0] for p in pairs], jnp.int32)
    pair_j = jnp.asarray([p[1] for p in pairs], jnp.int32)
    nblk = 2 * H_C
    q_tile = lambda w: pl.BlockSpec((None, tq, w), lambda bi, s, pi, pj: (bi, pi[s], 0))
    k_tile = lambda w: pl.BlockSpec((None, tq, w), lambda bi, s, pi, pj: (bi, pj[s], 0))
    qt_tile = lambda w: pl.BlockSpec((None, w, tq), lambda bi, s, pi, pj: (bi, 0, pi[s]))
    kt_tile = lambda w: pl.BlockSpec((None, w, tq), lambda bi, s, pi, pj: (bi, 0, pj[s]))
    const = lambda shape: pl.BlockSpec(shape, lambda bi, s, pi, pj: (0,) * len(shape))
    grid_spec = pltpu.PrefetchScalarGridSpec(
        num_scalar_prefetch=2,
        grid=(b, len(pairs)),
        in_specs=[pl.BlockSpec((None, 2 * LANES, H_B * tq), lambda bi, s, pi, pj: (bi * nq + pi[s], 0, 0)),
                  k_tile(2 * LANES), kt_tile(KV_RANK), const(mask.shape), q_tile(D_B), const(wuv_t.shape),
                  qt_tile(D_C), k_tile(D_C), kt_tile(D_C), q_tile(D_C), const(bias_tiles.shape),
                  const(lam_vecs.shape), const((DV_C, 1))],
        out_specs=[q_tile(D_B), q_tile(D_C)],
        scratch_shapes=[pltpu.VMEM((1, H_B * tq), F32), pltpu.VMEM((1, H_B * tq), F32),
                        pltpu.VMEM((KV_RANK, H_B * tq), F32),
                        pltpu.VMEM((D_C, nblk * tq), BF16), pltpu.VMEM((1, nblk * tq), F32),
                        pltpu.VMEM((1, nblk * tq), F32), pltpu.VMEM((nblk, DV_C, tq), F32)],
    )
    return pl.pallas_call(
        functools.partial(_prompt_attn_kernel, lam_init=lam_init),
        grid_spec=grid_spec,
        out_shape=[jax.ShapeDtypeStruct((b, t, D_B), BF16), jax.ShapeDtypeStruct((b, t, D_C), BF16)],
        compiler_params=_cparams(("parallel", "arbitrary")),
        name="prompt_attn",
    )(pair_i, pair_j, q_t, rows_pad, rows_t, mask, gb, wuv_t, qc_t, kc, vc_t, gc, bias_tiles, lam_vecs,
      subln_g.reshape(DV_C, 1))


def _softmax_step(s, m, l):
    m_new = jnp.maximum(m, jnp.max(s, axis=-1, keepdims=True))
    alpha = jnp.exp(m - m_new)
    p = jnp.exp(s - m_new)
    return m_new, alpha * l + jnp.sum(p, axis=-1, keepdims=True), alpha, p


def _decode_attn_kernel(pt_ref, q_ref, row_ref, gb_ref, wuv_ref, qc_ref, kn_ref, vn_ref, gc_ref, bias_ref, lam_ref,
                        sg_ref, cm_hbm, ck_hbm, cv_hbm, ob_ref, oc_ref, mbuf, kbuf, vbuf, sems,
                        *, layer, pages_per_seq, lam_init):
    n_pages = DECODE_PAGES_PER_CHUNK
    seq = pl.program_id(0)
    chunks = pages_per_seq // n_pages
    total = pl.num_programs(0) * chunks

    def chunk_copies(g, slot):
        out = []
        for p in range(n_pages):
            page = pt_ref[g * n_pages + p]
            out.append(pltpu.make_async_copy(cm_hbm.at[layer, page], mbuf.at[slot, p], sems.at[slot, 0]))
            out.append(pltpu.make_async_copy(ck_hbm.at[layer, page], kbuf.at[slot, p], sems.at[slot, 1]))
            out.append(pltpu.make_async_copy(cv_hbm.at[layer, page], vbuf.at[slot, p], sems.at[slot, 2]))
        return out

    @pl.when(seq == 0)
    def _():
        for cp in chunk_copies(0, 0):
            cp.start()

    q = q_ref[...]
    qbd = _diff_queries(qc_ref[...])
    qb = qbd.astype(BF16)

    def chunk_body(c, carry):
        mb, lb, accb, md, ld, accd = carry
        g = seq * chunks + c
        slot = lax.rem(g, 2)

        @pl.when(g + 1 < total)
        def _():
            for cp in chunk_copies(g + 1, 1 - slot):
                cp.start()

        for cp in chunk_copies(g, slot):
            cp.wait()

        pages = range(n_pages)
        lanes = [slice(p * PAGE_SIZE, (p + 1) * PAGE_SIZE) for p in pages]
        ks = [mbuf[slot, p].astype(BF16) for p in pages]
        s_b = jnp.concatenate([_dot(q, ks[p]) for p in pages], axis=1)
        s_d = jnp.concatenate([_dot(qb, kbuf[slot, p].astype(BF16)) for p in pages], axis=1)
        s_d = s_d + bias_ref[jnp.where(c == chunks - 1, 1, 0)]
        mb, lb, alpha_b, p_b = _softmax_step(s_b, mb, lb)
        md, ld, alpha_d, p_d = _softmax_step(s_d, md, ld)
        p_b = p_b.astype(BF16)
        p_d = p_d.astype(BF16)
        pv_b = [_dot_nt(p_b[:, lanes[p]], ks[p][0:KV_RANK, :]) for p in pages]
        pv_d = [_dot_nt(p_d[:, lanes[p]], vbuf[slot, p].astype(BF16)) for p in pages]
        accb = alpha_b * accb + functools.reduce(lambda x, y: x + y, pv_b)
        accd = alpha_d * accd + functools.reduce(lambda x, y: x + y, pv_d)
        return mb, lb, accb, md, ld, accd

    init = (jnp.full((H_B, 1), NEG_INF, F32), jnp.zeros((H_B, 1), F32), jnp.zeros((H_B, KV_RANK), F32),
            jnp.full((2 * H_C, 1), NEG_INF, F32), jnp.zeros((2 * H_C, 1), F32), jnp.zeros((2 * H_C, D_C), F32))
    mb, lb, accb, md, ld, accd = lax.fori_loop(0, chunks, chunk_body, init)

    row = row_ref[...]
    s_new = jnp.sum(q.astype(F32) * row, axis=-1, keepdims=True)
    mb, lb, alpha, p_new = _softmax_step(s_new, mb, lb)
    accb = alpha * accb + p_new * row[:, 0:KV_RANK]
    _mla_out(accb / lb, wuv_ref, gb_ref[...], ob_ref, 1)

    s_new = jnp.sum(qbd * kn_ref[...], axis=-1, keepdims=True) + bias_ref[2][:, 0:1]
    md, ld, alpha, p_new = _softmax_step(s_new, md, ld)
    accd = alpha * accd + p_new * vn_ref[...]
    _diff_out(accd, ld, _diff_lambda(lam_ref, lam_init), lam_init, sg_ref[...], gc_ref[...], oc_ref, 1)


def _decode_attn(page_table_flat, q, rows_new, gb, wuv, qc, kc, vc, gc, bias_rows, lam_vecs, subln_g,
                 cache_mla, cache_k, cache_v, layer, pages_per_seq, lam_init):
    b = q.shape[0]
    n_pages = DECODE_PAGES_PER_CHUNK
    assert pages_per_seq % n_pages == 0
    per_b = lambda shape: pl.BlockSpec((None,) + shape, lambda bi, pt: (bi,) + (0,) * len(shape))
    const = lambda shape: pl.BlockSpec(shape, lambda bi, pt: (0,) * len(shape))
    hbm = pl.BlockSpec(memory_space=pl.ANY)
    grid_spec = pltpu.PrefetchScalarGridSpec(
        num_scalar_prefetch=1,
        grid=(b,),
        in_specs=[per_b((H_B, MLA_W)), per_b((1, MLA_W)), per_b((1, D_B)), const(wuv.shape),
                  per_b((1, D_C)), per_b((1, D_C)), per_b((1, D_C)), per_b((1, D_C)), const(bias_rows.shape),
                  const(lam_vecs.shape), const((1, DV_C)), hbm, hbm, hbm],
        out_specs=[per_b((1, D_B)), per_b((1, D_C))],
        scratch_shapes=[pltpu.VMEM((2, n_pages, MLA_W, PAGE_SIZE), F32),
                        pltpu.VMEM((2, n_pages, D_C, PAGE_SIZE), F32),
                        pltpu.VMEM((2, n_pages, D_C, PAGE_SIZE), F32),
                        pltpu.SemaphoreType.DMA((2, 3))],
    )
    return pl.pallas_call(
        functools.partial(_decode_attn_kernel, layer=layer, pages_per_seq=pages_per_seq, lam_init=lam_init),
        grid_spec=grid_spec,
        out_shape=[jax.ShapeDtypeStruct((b, 1, D_B), BF16), jax.ShapeDtypeStruct((b, 1, D_C), BF16)],
        compiler_params=_cparams(("arbitrary",)),
        name="decode_attn",
    )(page_table_flat, q, rows_new, gb, wuv, qc, kc, vc, gc, bias_rows, lam_vecs, subln_g.reshape(1, DV_C),
      cache_mla, cache_k, cache_v)


def _permute_w_in(w):
    o_ckv = A_IN + D_A + Q_RANK
    o_kr = o_ckv + KV_RANK
    o_gb = o_kr + ROPE_B
    half = ROPE_B // 2
    pad = jnp.zeros((w.shape[0], 2 * LANES - KV_RANK - 2 * ROPE_B), w.dtype)
    out = jnp.concatenate([w[:, :o_gb], w[:, o_kr + half:o_gb], w[:, o_kr:o_kr + half], pad, w[:, o_gb:]], axis=1)
    assert out.shape[1] == IN_COLS_PERM
    return out.astype(BF16)


def _extend_w_uq(w):
    w = w.reshape(Q_RANK, H_B, NOPE_B + ROPE_B)
    half = ROPE_B // 2
    rope = w[:, :, NOPE_B:]
    swapped = jnp.concatenate([rope[:, :, half:], rope[:, :, :half]], axis=-1)
    return jnp.concatenate([w, swapped], axis=-1).reshape(Q_RANK, H_B * LANES).astype(BF16)


def _rope_tables(pos):
    inv = ROPE_THETA ** (-jnp.arange(0, ROPE_B, 2, dtype=F32) / ROPE_B)
    ang = pos.astype(F32)[:, None] * inv[None, :]
    cos, sin = jnp.cos(ang), jnp.sin(ang)
    return jnp.concatenate([cos, cos], axis=-1), jnp.concatenate([-sin, sin], axis=-1)


def _layer_weights(l, W):
    row = lambda a: a.reshape(1, -1)
    rwkv = (row(W["mu_shift"][l]), row(W["rw_w0"][l]), W["rw_w2"][l], row(W["rw_a0"][l]), W["rw_a2"][l],
            row(W["rw_k_k"][l]), row(W["rw_k_a"][l]), row(W["rw_r_k"][l]), row(W["rw_gn_g"][l]),
            row(W["rw_gn_b"][l]))
    col = lambda a: a.reshape(-1, 1)
    rwkv_t = (col(W["mu_shift"][l]), col(W["rw_w0"][l]), W["rw_w2"][l].T, col(W["rw_a0"][l]), W["rw_a2"][l].T,
              col(W["rw_k_k"][l]), col(W["rw_k_a"][l]), col(W["rw_r_k"][l]), col(W["rw_gn_g"][l]),
              col(W["rw_gn_b"][l]))
    w_in = _permute_w_in(W["w_in"][l])
    wuq = _extend_w_uq(W["mla_w_uq"][l])
    return dict(
        w_in_n=jnp.concatenate([w_in[:, a:b] for a, b in PROMPT_SEGS_N], axis=1),
        w_in_t=jnp.concatenate([w_in[:, a:b] for a, b in PROMPT_SEGS_T], axis=1).T,
        w_in_sn=jnp.concatenate([w_in[:, a:b] for a, b in SAMPLE_SEGS_N], axis=1),
        w_in_st=jnp.concatenate([w_in[:, a:b] for a, b in SAMPLE_SEGS_T], axis=1).T,
        rwkv=rwkv,
        rwkv_t=rwkv_t,
        wuq=wuq,
        wuq_t=wuq.T,
        wuk=jnp.transpose(W["mla_w_uk"][l], (1, 0, 2)).astype(BF16),
        wuv_t=jnp.transpose(W["mla_w_uv"][l], (1, 2, 0)).astype(BF16),
        wuk_t=jnp.transpose(W["mla_w_uk"][l], (1, 2, 0)).astype(BF16),
        wuv=jnp.transpose(W["mla_w_uv"][l], (1, 0, 2)).astype(BF16),
        lam_vecs=jnp.stack([W["diff_lam_q1"][l], W["diff_lam_k1"][l], W["diff_lam_q2"][l], W["diff_lam_k2"][l]]),
        wo=W["w_out"][l].astype(BF16),
        wple=W["w_ple"][l].astype(BF16),
        wg=W["w_ple_gate"][l].astype(BF16),
    )


def _run_prompt(x, p, W, LW, depth):
    b, t, d = x.shape
    m = b * t
    tm = ATTN_TILE
    cos2, sin2 = _rope_tables(jnp.arange(t, dtype=jnp.int32))
    cos2_t, sin2_t = cos2.T, sin2.T
    tile = jnp.arange(ATTN_TILE, dtype=jnp.int32)
    dist = (jnp.arange(3, dtype=jnp.int32)[:, None, None] * ATTN_TILE + tile[None, None, :] - tile[None, :, None])
    bias_tiles = _bias_tiles(W["rel_bias"], dist)
    causal = jnp.stack([jnp.where(dist[0] >= 0, 0.0, NEG_INF).astype(F32), jnp.zeros(dist.shape[1:], F32)])
    uprev0 = jnp.zeros((b, 1, A_IN), F32)
    s0 = jnp.zeros((b, H_A, HEAD_A, HEAD_A), F32)
    segs_n = _pack_segments(PROMPT_SEGS_N)
    segs_t = _pack_segments(PROMPT_SEGS_T)
    h = x.reshape(m, d)
    mla_rows, k_rows, v_rows, wkv_out, shift_out = [], [], [], [], []
    for l in range(depth):
        lw = LW[l]
        u, ga, gb, gc, kc, cq_t, ckv_t, qc_t, kc_t, vc_t, xn_last = _inproj(
            h, W["norm_g"][l], lw["w_in_n"], segs_n, normalize=True, rows_per_seq=t, tm=tm,
            wt_bf16=lw["w_in_t"], segs_t=segs_t, seg_dtypes=PROMPT_DTYPES_N)
        y_a, s_new = _rwkv_chunked(u.reshape(b, t, A_IN), uprev0, ga.reshape(b, t, D_A), s0, lw["rwkv"])
        q_t, rows_pad, rows_t = _mla_prep_t(cq_t, ckv_t, cos2_t, sin2_t, W["mla_q_norm_g"][l], lw["wuq_t"],
                                            W["mla_kv_norm_g"][l], lw["wuk"], tm=tm)
        lam_init = 0.8 - 0.6 * math.exp(-0.3 * l)
        y_b, y_c = _prompt_attn(q_t, rows_pad, rows_t, causal, gb.reshape(b, t, D_B), lw["wuv_t"],
                                qc_t, kc.reshape(b, t, D_C), vc_t, gc.reshape(b, t, D_C), bias_tiles,
                                lw["lam_vecs"], W["diff_subln_g"][l], lam_init)
        h = _outproj(h, y_a.reshape(m, D_A), y_b.reshape(m, D_B), y_c.reshape(m, D_C), p[l].reshape(m, PLE_DIM),
                     lw["wo"], lw["wple"], lw["wg"], W["final_norm_g"], final=(l == depth - 1), tm=OUTPROJ_TILE)
        mla_rows.append(jnp.transpose(rows_t, (0, 2, 1)))
        k_rows.append(jnp.transpose(kc_t.reshape(b, H_C, 2 * DC, t), (0, 3, 1, 2)))
        v_rows.append(jnp.transpose(vc_t.reshape(b, H_C, DV_C, t), (0, 3, 1, 2)))
        wkv_out.append(s_new)
        shift_out.append(xn_last.reshape(b, d))
    return (h.reshape(b, t, d), jnp.stack(mla_rows), jnp.stack(k_rows), jnp.stack(v_rows), jnp.stack(wkv_out),
            jnp.stack(shift_out))


def _run_sample(x, p, state_shift, state_wkv, cache_mla, cache_k, cache_v, page_table, W, LW, depth):
    b, t, d = x.shape
    assert t == 1
    pages_per_seq = page_table.shape[1]
    past_len = pages_per_seq * PAGE_SIZE
    tm = b
    cos2, sin2 = _rope_tables(jnp.full((b,), past_len, dtype=jnp.int32))
    pt_flat = page_table.reshape(-1).astype(jnp.int32)
    step_keys = DECODE_PAGES_PER_CHUNK * PAGE_SIZE
    key_in_step = jnp.arange(step_keys, dtype=jnp.int32)
    dist = jnp.stack([past_len - key_in_step,
                      past_len - (past_len - step_keys + key_in_step),
                      jnp.zeros((step_keys,), jnp.int32)])
    dist = jnp.broadcast_to(dist[:, None, :], (3, 2, step_keys))
    bias = _bias_tiles(W["rel_bias"], dist)
    bias_rows = bias.reshape(3, 2 * H_C, step_keys)
    to_feature_major = lambda c: jnp.transpose(c, (0, 1, 3, 4, 2)).reshape(c.shape[:2] + (D_C, PAGE_SIZE))
    cache_k2 = to_feature_major(cache_k)
    cache_v2 = to_feature_major(cache_v)
    cache_mla_t = jnp.transpose(cache_mla, (0, 1, 3, 2))
    segs_n = _pack_segments(SAMPLE_SEGS_N)
    segs_t = _pack_segments(SAMPLE_SEGS_T)
    h = x.reshape(b, d)
    mla_rows, k_rows, v_rows, wkv_out, shift_out = [], [], [], [], []
    for l in range(depth):
        lw = LW[l]
        cq, ckv, gb, qc, kc, vc, gc, u_t, ga_t, xn = _inproj(
            h, W["norm_g"][l], lw["w_in_sn"], segs_n, normalize=True, rows_per_seq=1, tm=tm,
            wt_bf16=lw["w_in_st"], segs_t=segs_t)
        (uprev_t,) = _inproj(state_shift[l], W["norm_g"][l], None, (), normalize=False, rows_per_seq=1, tm=tm,
                             wt_bf16=lw["w_in_st"][:A_IN], segs_t=(SEG_U,))
        y_a_t, s_new_t = _rwkv_step(u_t[0], uprev_t[0], ga_t[0], jnp.transpose(state_wkv[l], (1, 2, 3, 0)),
                                    lw["rwkv_t"])
        y_a = y_a_t.T
        s_new = jnp.transpose(s_new_t, (3, 0, 1, 2))
        q, rows = _mla_prep(cq, ckv, cos2, sin2, W["mla_q_norm_g"][l], lw["wuq"], W["mla_kv_norm_g"][l],
                            lw["wuk_t"], tm=tm, pos_tiles=1)
        lam_init = 0.8 - 0.6 * math.exp(-0.3 * l)
        y_b, y_c = _decode_attn(pt_flat, jnp.transpose(q, (1, 0, 2)), rows.reshape(b, 1, MLA_W),
                                gb.reshape(b, 1, D_B), lw["wuv"], qc.reshape(b, 1, D_C), kc.reshape(b, 1, D_C),
                                vc.reshape(b, 1, D_C), gc.reshape(b, 1, D_C), bias_rows, lw["lam_vecs"],
                                W["diff_subln_g"][l], cache_mla_t, cache_k2, cache_v2, l, pages_per_seq, lam_init)
        h = _outproj(h, y_a, y_b.reshape(b, D_B), y_c.reshape(b, D_C), p[l].reshape(b, PLE_DIM),
                     lw["wo"], lw["wple"], lw["wg"], W["final_norm_g"], final=(l == depth - 1), tm=tm)
        mla_rows.append(rows.reshape(b, 1, MLA_W))
        k_rows.append(kc.reshape(b, 1, H_C, 2 * DC))
        v_rows.append(vc.reshape(b, 1, H_C, DV_C))
        wkv_out.append(s_new)
        shift_out.append(xn)
    return (h.reshape(b, 1, d), jnp.stack(mla_rows), jnp.stack(k_rows), jnp.stack(v_rows), jnp.stack(wkv_out),
            jnp.stack(shift_out))


def kernel(x_prompt, x_sample, cache_mla, cache_diff_k, cache_diff_v, state_wkv, state_shift, page_table,
           p_prompt, p_sample, norm_g, w_in, mu_shift, rw_w0, rw_w2, rw_a0, rw_a2, rw_k_k, rw_k_a, rw_r_k,
           rw_gn_g, rw_gn_b, mla_q_norm_g, mla_w_uq, mla_kv_norm_g, mla_w_uk, mla_w_uv, diff_lam_q1,
           diff_lam_k1, diff_lam_q2, diff_lam_k2, diff_subln_g, rel_bias, w_out, w_ple, w_ple_gate,
           final_norm_g):
    W = {"norm_g": norm_g, "w_in": w_in, "mu_shift": mu_shift, "rw_w0": rw_w0, "rw_w2": rw_w2, "rw_a0": rw_a0,
         "rw_a2": rw_a2, "rw_k_k": rw_k_k, "rw_k_a": rw_k_a, "rw_r_k": rw_r_k, "rw_gn_g": rw_gn_g,
         "rw_gn_b": rw_gn_b, "mla_q_norm_g": mla_q_norm_g, "mla_w_uq": mla_w_uq, "mla_kv_norm_g": mla_kv_norm_g,
         "mla_w_uk": mla_w_uk, "mla_w_uv": mla_w_uv, "diff_lam_q1": diff_lam_q1, "diff_lam_k1": diff_lam_k1,
         "diff_lam_q2": diff_lam_q2, "diff_lam_k2": diff_lam_k2, "diff_subln_g": diff_subln_g,
         "rel_bias": rel_bias, "w_out": w_out, "w_ple": w_ple, "w_ple_gate": w_ple_gate,
         "final_norm_g": final_norm_g}
    depth = w_in.shape[0]
    LW = [_layer_weights(l, W) for l in range(depth)]
    y_p, mla_p, dk_p, dv_p, wkv_p, sh_p = _run_prompt(x_prompt, p_prompt, W, LW, depth)
    y_s, mla_s, dk_s, dv_s, wkv_s, sh_s = _run_sample(x_sample, p_sample, state_shift, state_wkv, cache_mla,
                                                      cache_diff_k, cache_diff_v, page_table, W, LW, depth)
    return (y_p, y_s, mla_p, mla_s, dk_p, dk_s, dv_p, dv_s, wkv_p, wkv_s, sh_p, sh_s)
```

```python
import functools
import math

import jax
import jax.numpy as jnp
from jax import lax
from jax.experimental import pallas as pl
from jax.experimental.pallas import tpu as pltpu

F32 = jnp.float32
BF16 = jnp.bfloat16

LANES = 128
SUBLANES = 8
VMEM_LIMIT_BYTES = 56 * 1024 * 1024

D_MODEL = 1024
HEAD_A = 64
D_A = 512
H_A = D_A // HEAD_A
W_LORA = 64
A_LORA = 64
A_IN = 3 * D_A + W_LORA + A_LORA
D_B = 256
DV_B = 64
H_B = D_B // DV_B
NOPE_B = 64
ROPE_B = 32
Q_RANK = 256
KV_RANK = 128
MLA_W = KV_RANK + ROPE_B
ROPE_THETA = 10000.0
D_C = 256
DV_C = 64
H_C = D_C // DV_C
DC = DV_C // 2
NUM_BUCKETS = 32
MAX_DISTANCE = 128
PLE_DIM = 256
PAGE_SIZE = 128
NEG_INF = -1e30
EPS = 1e-6
GN_EPS = 64e-5
SUBLN_EPS = 1e-5
MLA_SCALE = (NOPE_B + ROPE_B) ** -0.5
DIFF_SCALE = DC ** -0.5

SEG_U = (0, A_IN)
SEG_GA = (A_IN, A_IN + D_A)
SEG_CQ = (SEG_GA[1], SEG_GA[1] + Q_RANK)
SEG_CKV = (SEG_CQ[1], SEG_CQ[1] + 2 * LANES)
SEG_GB = (SEG_CKV[1], SEG_CKV[1] + D_B)
SEG_QC = (SEG_GB[1], SEG_GB[1] + D_C)
SEG_KC = (SEG_QC[1], SEG_QC[1] + D_C)
SEG_VC = (SEG_KC[1], SEG_KC[1] + D_C)
SEG_GC = (SEG_VC[1], SEG_VC[1] + D_C)
IN_COLS_PERM = SEG_GC[1]


def _pack_segments(segs):
    out, pos = [], 0
    for a, b in segs:
        out.append((pos, pos + b - a))
        pos += b - a
    return tuple(out)


PROMPT_SEGS_N = (SEG_U, SEG_GA, SEG_GB, SEG_GC, SEG_KC)
PROMPT_SEGS_T = (SEG_CQ, SEG_CKV, SEG_QC, SEG_KC, SEG_VC)
PROMPT_DTYPES_N = (F32, F32, F32, F32, BF16)
SAMPLE_SEGS_N = (SEG_CQ, SEG_CKV, SEG_GB, SEG_QC, SEG_KC, SEG_VC, SEG_GC)
SAMPLE_SEGS_T = (SEG_U, SEG_GA)

RWKV_CHUNK = 64
RWKV_BLOCK = 256
ATTN_TILE = 512
OUTPROJ_TILE = 512
DECODE_PAGES_PER_CHUNK = 64


def _cparams(semantics):
    return pltpu.CompilerParams(dimension_semantics=semantics, vmem_limit_bytes=VMEM_LIMIT_BYTES)


def _full(shape):
    n = len(shape)
    return pl.BlockSpec(shape, lambda *_: (0,) * n)


def _sigmoid(x):
    return 1.0 / (1.0 + jnp.exp(-x))


def _silu(x):
    return x * _sigmoid(x)


def _rms(x, g, eps):
    return x * lax.rsqrt(jnp.mean(x * x, axis=-1, keepdims=True) + eps) * g


def _dot(a, b, **kw):
    return jnp.dot(a, b, preferred_element_type=F32, **kw)


def _dot_nt(a, b, **kw):
    return lax.dot_general(a, b, (((1,), (1,)), ((), ())), preferred_element_type=F32, **kw)


def _dot_tn(a, b, **kw):
    return lax.dot_general(a, b, (((0,), (0,)), ((), ())), preferred_element_type=F32, **kw)


def _split3(x):
    hi = x.astype(BF16)
    rest = x - hi.astype(F32)
    mid = rest.astype(BF16)
    lo = (rest - mid.astype(F32)).astype(BF16)
    return hi, mid, lo


def _dot_split_rhs(a_exact, b):
    hi, mid, lo = _split3(b)
    return _dot(a_exact, hi) + (_dot(a_exact, mid) + _dot(a_exact, lo))


def _dot_split_lhs(a, b_exact):
    hi, mid, lo = _split3(a)
    return _dot(hi, b_exact) + (_dot(mid, b_exact) + _dot(lo, b_exact))


def _mm(a, b):
    return _dot(a.astype(BF16), b.astype(BF16))


def _mm_nt(a, b):
    return _dot_nt(a.astype(BF16), b.astype(BF16))


def _mm_tn(a, b):
    return _dot_tn(a.astype(BF16), b.astype(BF16))


def _inproj_kernel(h_ref, g_ref, w_ref, wt_ref, *out_refs, normalize, segs, segs_t, emit_xn):
    x = h_ref[...]
    xn = _rms(x, g_ref[...], EPS) if normalize else x
    xb = xn.astype(BF16)
    for o_ref, (a, b) in zip(out_refs, segs):
        o_ref[...] = _dot(xb, w_ref[:, a:b]).astype(o_ref.dtype)
    for o_ref, (a, b) in zip(out_refs[len(segs):], segs_t):
        o_ref[...] = _dot_nt(wt_ref[a:b, :], xb)
    n_proj = len(segs) + len(segs_t)
    if emit_xn == "last_row":
        rows = x.shape[0]
        out_refs[n_proj][...] = xn[rows - 1:rows, :]
    elif emit_xn == "all":
        out_refs[n_proj][...] = xn


def _inproj(h2d, norm_g, w_bf16, segs, *, normalize, rows_per_seq, tm, wt_bf16=None, segs_t=(), seg_dtypes=None):
    m, d = h2d.shape
    assert m % tm == 0 and rows_per_seq % tm == 0 or rows_per_seq == 1
    tiles_per_seq = max(rows_per_seq // tm, 1)
    t_cols = rows_per_seq if rows_per_seq > 1 else m
    t_tiles = t_cols // tm
    if w_bf16 is None:
        w_bf16 = jnp.zeros((d, LANES), BF16)
    if wt_bf16 is None:
        wt_bf16 = jnp.zeros((SUBLANES, d), BF16)
    seg_dtypes = seg_dtypes or (F32,) * len(segs)
    out_shapes = [jax.ShapeDtypeStruct((m, b - a), dt) for (a, b), dt in zip(segs, seg_dtypes)]
    out_specs = [pl.BlockSpec((tm, b - a), lambda i: (i, 0)) for a, b in segs]
    for a, b in segs_t:
        out_shapes.append(jax.ShapeDtypeStruct((m // t_cols, b - a, t_cols), F32))
        out_specs.append(pl.BlockSpec((None, b - a, tm), lambda i: (i // t_tiles, 0, i % t_tiles)))
    emit_xn = None
    if normalize:
        if rows_per_seq == 1:
            emit_xn = "all"
            out_shapes.append(jax.ShapeDtypeStruct((m, d), F32))
            out_specs.append(pl.BlockSpec((tm, d), lambda i: (i, 0)))
        else:
            emit_xn = "last_row"
            out_shapes.append(jax.ShapeDtypeStruct((m // rows_per_seq, 1, d), F32))
            out_specs.append(pl.BlockSpec((None, 1, d), lambda i: (i // tiles_per_seq, 0, 0)))
    kern = functools.partial(_inproj_kernel, normalize=normalize, segs=segs, segs_t=segs_t, emit_xn=emit_xn)
    return pl.pallas_call(
        kern,
        grid=(m // tm,),
        in_specs=[pl.BlockSpec((tm, d), lambda i: (i, 0)), _full((1, d)), _full(w_bf16.shape), _full(wt_bf16.shape)],
        out_specs=out_specs,
        out_shape=out_shapes,
        compiler_params=_cparams(("arbitrary",)),
        name="inproj",
    )(h2d, norm_g.reshape(1, d), w_bf16, wt_bf16)


def _outproj_kernel(h_ref, ya_ref, yb_ref, yc_ref, p_ref, wo_ref, wple_ref, wg_ref, fng_ref, o_ref, *, final):
    mixed = (_dot(ya_ref[...], wo_ref[0:D_A, :])
             + _dot(yb_ref[...], wo_ref[D_A:D_A + D_B, :])
             + _dot(yc_ref[...], wo_ref[D_A + D_B:, :]))
    h2 = h_ref[...] + mixed
    ple = _dot(p_ref[...].astype(BF16), wple_ref[...])
    gate = _sigmoid(_dot(h2.astype(BF16), wg_ref[...]))
    h3 = h2 + ple * gate
    o_ref[...] = _rms(h3, fng_ref[...], EPS) if final else h3


def _outproj(h2d, ya, yb, yc, p2d, wo, wple, wg, final_g, *, final, tm):
    m, d = h2d.shape
    row = lambda w: pl.BlockSpec((tm, w), lambda i: (i, 0))
    return pl.pallas_call(
        functools.partial(_outproj_kernel, final=final),
        grid=(m // tm,),
        in_specs=[row(d), row(D_A), row(D_B), row(D_C), row(PLE_DIM),
                  _full(wo.shape), _full(wple.shape), _full(wg.shape), _full((1, d))],
        out_specs=row(d),
        out_shape=jax.ShapeDtypeStruct((m, d), F32),
        compiler_params=_cparams(("arbitrary",)),
        name="outproj",
    )(h2d, ya, yb, yc, p2d, wo, wple, wg, final_g.reshape(1, d))


def _rwkv_prep(um, w0, w2, a0, a2, k_k, k_a):
    r = um[:, 0:D_A]
    k = um[:, D_A:2 * D_A]
    v = um[:, 2 * D_A:3 * D_A]
    w_lo = um[:, 3 * D_A:3 * D_A + W_LORA]
    a_lo = um[:, 3 * D_A + W_LORA:A_IN]
    wl = w0 + _mm(jnp.tanh(w_lo), w2)
    neg = -wl
    softplus = jnp.maximum(neg, 0.0) + jnp.log(1.0 + jnp.exp(-jnp.abs(neg)))
    w = -softplus - 0.5
    log_decay = -jnp.exp(w)
    a = _sigmoid(a0 + _mm(a_lo, a2))
    kk = k * k_k
    k = k * (1.0 + (a - 1.0) * k_a)
    return r, k, v, kk, a, log_decay


def _rwkv_head_out(y, r_h, k_h, v_h, rk_h, gng_h, gnb_h, gate_h):
    mu = jnp.mean(y, axis=-1, keepdims=True)
    var = jnp.mean(jnp.square(y - mu), axis=-1, keepdims=True)
    yn = (y - mu) * lax.rsqrt(var + GN_EPS) * gng_h + gnb_h
    bonus = jnp.sum(r_h * k_h * rk_h, axis=-1, keepdims=True) * v_h
    return (yn + bonus) * _silu(gate_h)


def _rwkv_chunk_kernel(u_ref, uprev0_ref, ga_ref, s0_ref, ones_ref, mu_ref, w0_ref, w2_ref, a0_ref, a2_ref,
                       kk_ref, ka_ref, rk_ref, gng_ref, gnb_ref, y_ref, s_ref, prev_ref):
    step = pl.program_id(1)
    R = u_ref.shape[0]
    C = RWKV_CHUNK
    subs = range(R // C)

    @pl.when(step == 0)
    def _():
        prev_ref[...] = uprev0_ref[...]
        s_ref[...] = s0_ref[...]

    u = u_ref[...]
    row = lax.broadcasted_iota(jnp.int32, (R, 1), 0)
    u_prev = jnp.where(row == 0, prev_ref[...], pltpu.roll(u, 1, axis=0))
    prev_ref[...] = u[R - 1:R, :]
    um = u + mu_ref[...] * (u_prev - u)
    r, k, v, kk, a, log_decay = _rwkv_prep(um, w0_ref[...], w2_ref[...], a0_ref[...], a2_ref[...],
                                           kk_ref[...], ka_ref[...])

    ri = lax.broadcasted_iota(jnp.int32, (R, R), 0)
    rj = lax.broadcasted_iota(jnp.int32, (R, R), 1)
    same_chunk = (ri // C) == (rj // C)
    cs = _dot_split_rhs((same_chunk & (rj <= ri)).astype(BF16), log_decay)
    cs_last = [cs[(sb + 1) * C - 1:(sb + 1) * C, :] for sb in subs]
    cs_end = jnp.concatenate([jnp.broadcast_to(x, (C, x.shape[1])) for x in cs_last], axis=0)
    p_end = [jnp.exp(x) for x in cs_last]
    e_inv = jnp.exp(-cs)
    e_rem = jnp.exp(cs_end - cs)
    kk_n = kk / jnp.maximum(jnp.sqrt(_dot_split_lhs(kk * kk, ones_ref[...])), 1e-12)
    b_f = kk_n * a
    a_t = (-kk_n * jnp.exp(cs - log_decay)).astype(BF16)
    r_t = (r * jnp.exp(cs)).astype(BF16)
    b_t = (b_f * e_inv).astype(BF16)
    k_t = (k * e_inv).astype(BF16)
    b_end = (b_f * e_rem).astype(BF16)
    k_end = (k * e_rem).astype(BF16)
    v_b = v.astype(BF16)
    ga = ga_ref[...]

    t2 = lax.broadcasted_iota(jnp.int32, (C, 2 * C), 0)
    j2 = lax.broadcasted_iota(jnp.int32, (C, 2 * C), 1)
    j2 = jnp.where(j2 >= C, j2 - C, j2)
    strict2 = j2 < t2
    incl2 = j2 <= t2
    ti = lax.broadcasted_iota(jnp.int32, (C, C), 0)
    tj = lax.broadcasted_iota(jnp.int32, (C, C), 1)
    eye = (ti == tj).astype(F32)
    zeros = jnp.zeros((C, HEAD_A), BF16)

    units = [(sb, h) for sb in subs for h in range(H_A)]
    blk = lambda x, sb, h: x[sb * C:(sb + 1) * C, h * HEAD_A:(h + 1) * HEAD_A]
    a_h = {un: blk(a_t, *un) for un in units}
    r_h = {un: blk(r_t, *un) for un in units}
    v_h = {un: blk(v_b, *un) for un in units}
    gram = {un: _dot_nt(jnp.concatenate([a_h[un], r_h[un]], axis=0),
                        jnp.concatenate([blk(b_t, *un), blk(k_t, *un)], axis=0)) for un in units}
    l_top = {un: jnp.where(strict2, gram[un][0:C], 0.0) for un in units}
    m_bot = {un: jnp.where(incl2, gram[un][C:], 0.0).astype(BF16) for un in units}
    lv = {un: _dot(l_top[un].astype(BF16), jnp.concatenate([zeros, v_h[un]], axis=0)) for un in units}

    l_ab = {un: l_top[un][:, 0:C] for un in units}
    inv = {un: eye + l_ab[un] for un in units}
    pw = {un: l_ab[un].astype(BF16) for un in units}
    for _ in range(int(math.log2(C)) - 1):
        pw = {un: _mm(pw[un], pw[un]).astype(BF16) for un in units}
        inv = {un: inv[un] + _mm(inv[un], pw[un]) for un in units}
    inv = {un: inv[un].astype(BF16) for un in units}
    w_mat = {un: _mm(inv[un], a_h[un]).astype(BF16) for un in units}
    u_v = {un: _mm(inv[un], lv[un]) for un in units}

    state = [s_ref[h] for h in range(H_A)]
    for sb in subs:
        heads = [(sb, h) for h in range(H_A)]
        state_b = [x.astype(BF16) for x in state]
        uv = [jnp.concatenate([(_dot_nt(w_mat[un], state_b[un[1]]) + u_v[un]).astype(BF16), v_h[un]], axis=0)
              for un in heads]
        y = [_dot_nt(r_h[un], state_b[un[1]]) + _dot(m_bot[un], uv[un[1]]) for un in heads]
        state = [state[h] * p_end[sb][:, h * HEAD_A:(h + 1) * HEAD_A]
                 + _dot_tn(uv[h], jnp.concatenate([blk(b_end, sb, h), blk(k_end, sb, h)], axis=0))
                 for h in range(H_A)]
        rows = slice(sb * C, (sb + 1) * C)
        for h in range(H_A):
            hs = slice(h * HEAD_A, (h + 1) * HEAD_A)
            out = _rwkv_head_out(y[h], r[rows, hs], k[rows, hs], v[rows, hs], rk_ref[:, hs], gng_ref[:, hs],
                                 gnb_ref[:, hs], ga[rows, hs])
            y_ref[rows, hs] = out.astype(y_ref.dtype)
    for h in range(H_A):
        s_ref[h] = state[h]


def _rwkv_chunked(u, uprev0, ga, s0, params):
    b, t, _ = u.shape
    c = RWKV_BLOCK
    assert t % c == 0
    head_of_lane = jnp.arange(D_A, dtype=jnp.int32) // HEAD_A
    head_ones = (head_of_lane[:, None] == head_of_lane[None, :]).astype(BF16)
    tok = lambda w: pl.BlockSpec((None, c, w), lambda i, j: (i, j, 0))
    state = pl.BlockSpec((None, H_A, HEAD_A, HEAD_A), lambda i, j: (i, 0, 0, 0))
    return pl.pallas_call(
        _rwkv_chunk_kernel,
        grid=(b, t // c),
        in_specs=[tok(A_IN), pl.BlockSpec((None, 1, A_IN), lambda i, j: (i, 0, 0)), tok(D_A), state,
                  _full(head_ones.shape)] + [_full(p.shape) for p in params],
        out_specs=[tok(D_A), state],
        out_shape=[jax.ShapeDtypeStruct((b, t, D_A), BF16), jax.ShapeDtypeStruct(s0.shape, F32)],
        scratch_shapes=[pltpu.VMEM((1, A_IN), F32)],
        compiler_params=_cparams(("parallel", "arbitrary")),
        name="rwkv_chunked",
    )(u, uprev0, ga, s0, head_ones, *params)


def _rwkv_step_kernel(r_ref, k_ref, v_ref, lo_ref, rp_ref, kp_ref, vp_ref, lop_ref, ga_ref, s0_ref,
                      mur_ref, muk_ref, muv_ref, mulo_ref, w0_ref, w2t_ref, a0_ref, a2t_ref, kk_ref, ka_ref,
                      rk_ref, gng_ref, gnb_ref, y_ref, s_ref, y_scr):
    mix = lambda x_ref, p_ref, mu_ref: x_ref[...] + mu_ref[...] * (p_ref[...] - x_ref[...])
    r = mix(r_ref, rp_ref, mur_ref)
    k = mix(k_ref, kp_ref, muk_ref)
    v = mix(v_ref, vp_ref, muv_ref)
    lo = mix(lo_ref, lop_ref, mulo_ref)
    neg = -(w0_ref[...] + _mm(w2t_ref[...], jnp.tanh(lo[0:W_LORA])))
    w = -(jnp.maximum(neg, 0.0) + jnp.log(1.0 + jnp.exp(-jnp.abs(neg)))) - 0.5
    decay = jnp.exp(-jnp.exp(w))
    a = _sigmoid(a0_ref[...] + _mm(a2t_ref[...], lo[W_LORA:]))
    kk = k * kk_ref[...]
    kk = kk / jnp.maximum(jnp.sqrt(jnp.sum(kk * kk, axis=0, keepdims=True)), 1e-12)
    k = k * (1.0 + (a - 1.0) * ka_ref[...])
    b = kk * a
    nkk = -kk
    for i in range(HEAD_A):
        s = s0_ref[i]
        sa = jnp.sum(s * nkk, axis=0, keepdims=True)
        s_new = s * decay + sa * b + v[i:i + 1, :] * k
        s_ref[i] = s_new
        y_scr[i:i + 1, :] = jnp.sum(s_new * r, axis=0, keepdims=True)
    y = y_scr[...]
    mu = jnp.mean(y, axis=0, keepdims=True)
    var = jnp.mean(jnp.square(y - mu), axis=0, keepdims=True)
    yn = (y - mu) * lax.rsqrt(var + GN_EPS) * gng_ref[...] + gnb_ref[...]
    bonus = jnp.sum(r * k * rk_ref[...], axis=0, keepdims=True) * v
    y_ref[...] = ((yn + bonus) * _silu(ga_ref[...])).astype(y_ref.dtype)


def _rwkv_step(u_t, uprev_t, ga_t, s0_t, params_t):
    nb = u_t.shape[1]
    mu, w0, w2t, a0, a2t, k_k, k_a, r_k, gn_g, gn_b = params_t
    n_head_blocks = D_A // HEAD_A
    lora_block = 3 * D_A // (W_LORA + A_LORA)
    feat = lambda off: pl.BlockSpec((HEAD_A, nb), lambda h: (off * n_head_blocks + h, 0))
    lora = pl.BlockSpec((W_LORA + A_LORA, nb), lambda h: (lora_block, 0))
    col = lambda off: pl.BlockSpec((HEAD_A, 1), lambda h: (off * n_head_blocks + h, 0))
    lora_col = pl.BlockSpec((W_LORA + A_LORA, 1), lambda h: (lora_block, 0))
    head_rows = lambda w: pl.BlockSpec((HEAD_A, w), lambda h: (h, 0))
    state = pl.BlockSpec((None, HEAD_A, HEAD_A, nb), lambda h: (h, 0, 0, 0))
    return pl.pallas_call(
        _rwkv_step_kernel,
        grid=(H_A,),
        in_specs=[feat(0), feat(1), feat(2), lora, feat(0), feat(1), feat(2), lora, head_rows(nb), state,
                  col(0), col(1), col(2), lora_col, col(0), head_rows(W_LORA), col(0), head_rows(A_LORA),
                  col(0), col(0), col(0), col(0), col(0)],
        out_specs=[head_rows(nb), state],
        out_shape=[jax.ShapeDtypeStruct((D_A, nb), BF16), jax.ShapeDtypeStruct(s0_t.shape, F32)],
        scratch_shapes=[pltpu.VMEM((HEAD_A, nb), F32)],
        compiler_params=_cparams(("parallel",)),
        name="rwkv_step",
    )(u_t, u_t, u_t, u_t, uprev_t, uprev_t, uprev_t, uprev_t, ga_t, s0_t,
      mu, mu, mu, mu, w0, w2t, a0, a2t, k_k, k_a, r_k, gn_g, gn_b)


def _mla_prep_kernel(cq_ref, ckv_ref, cos_ref, sin_ref, qg_ref, wuq_ref, kvg_ref, wukt_ref, q_ref, rows_ref):
    cos2 = cos_ref[...]
    sin2 = sin_ref[...]
    qn = _rms(cq_ref[...], qg_ref[...], EPS).astype(BF16)
    q = _dot(qn, wuq_ref[...])
    for h in range(H_B):
        qh = q[:, h * LANES:(h + 1) * LANES]
        q_lat = _dot(qh[:, :NOPE_B].astype(BF16), wukt_ref[h])
        q_rope = qh[:, NOPE_B:NOPE_B + ROPE_B] * cos2 + qh[:, NOPE_B + ROPE_B:] * sin2
        q_ref[h, :, 0:KV_RANK] = (q_lat * MLA_SCALE).astype(q_ref.dtype)
        q_ref[h, :, KV_RANK:MLA_W] = (q_rope * MLA_SCALE).astype(q_ref.dtype)
    ckv = ckv_ref[...]
    rows_ref[:, 0:KV_RANK] = _rms(ckv[:, 0:KV_RANK], kvg_ref[...], EPS)
    rows_ref[:, KV_RANK:MLA_W] = (ckv[:, KV_RANK:KV_RANK + ROPE_B] * cos2
                                  + ckv[:, KV_RANK + ROPE_B:KV_RANK + 2 * ROPE_B] * sin2)


def _mla_prep(cq, ckv, cos2, sin2, q_norm_g, wuq_ext, kv_norm_g, wuk_t, *, tm, pos_tiles):
    m = cq.shape[0]
    row = lambda w: pl.BlockSpec((tm, w), lambda i: (i, 0))
    pos = pl.BlockSpec((tm, ROPE_B), lambda i: (i % pos_tiles, 0))
    return pl.pallas_call(
        _mla_prep_kernel,
        grid=(m // tm,),
        in_specs=[row(Q_RANK), row(2 * LANES), pos, pos, _full((1, Q_RANK)), _full(wuq_ext.shape),
                  _full((1, KV_RANK)), _full(wuk_t.shape)],
        out_specs=[pl.BlockSpec((H_B, tm, MLA_W), lambda i: (0, i, 0)), row(MLA_W)],
        out_shape=[jax.ShapeDtypeStruct((H_B, m, MLA_W), BF16), jax.ShapeDtypeStruct((m, MLA_W), F32)],
        compiler_params=_cparams(("parallel",)),
        name="mla_prep",
    )(cq, ckv, cos2, sin2, q_norm_g.reshape(1, Q_RANK), wuq_ext, kv_norm_g.reshape(1, KV_RANK), wuk_t)


def _mla_out(o_lat, wuv_ref, gb, o_ref, rows_per_head):
    for h in range(H_B):
        o_h = _dot(o_lat[h * rows_per_head:(h + 1) * rows_per_head].astype(BF16), wuv_ref[h])
        hs = slice(h * DV_B, (h + 1) * DV_B)
        o_ref[:, hs] = (o_h * _silu(gb[:, hs])).astype(o_ref.dtype)


def _mla_prep_t_kernel(cq_ref, ckv_ref, cos_ref, sin_ref, qg_ref, wuqt_ref, kvg_ref, wuk_ref,
                       q_ref, rows_ref, rowst_ref):
    tm = cq_ref.shape[1]
    cos2 = cos_ref[...]
    sin2 = sin_ref[...]
    cq = cq_ref[...]
    qn = (cq * lax.rsqrt(jnp.mean(cq * cq, axis=0, keepdims=True) + EPS) * qg_ref[...]).astype(BF16)
    q = _dot(wuqt_ref[...], qn)
    pad = jnp.zeros((2 * LANES - MLA_W, tm), q_ref.dtype)
    for h in range(H_B):
        qh = q[h * LANES:(h + 1) * LANES]
        q_lat = _dot(wuk_ref[h], qh[0:NOPE_B].astype(BF16))
        q_rope = qh[NOPE_B:NOPE_B + ROPE_B] * cos2 + qh[NOPE_B + ROPE_B:] * sin2
        cols = slice(h * tm, (h + 1) * tm)
        q_ref[0:KV_RANK, cols] = (q_lat * MLA_SCALE).astype(q_ref.dtype)
        q_ref[KV_RANK:MLA_W, cols] = (q_rope * MLA_SCALE).astype(q_ref.dtype)
        q_ref[MLA_W:, cols] = pad
    ckv = ckv_ref[...]
    c = ckv[0:KV_RANK]
    cn = c * lax.rsqrt(jnp.mean(c * c, axis=0, keepdims=True) + EPS) * kvg_ref[...]
    kr = ckv[KV_RANK:KV_RANK + ROPE_B] * cos2 + ckv[KV_RANK + ROPE_B:KV_RANK + 2 * ROPE_B] * sin2
    rowst_ref[0:KV_RANK, :] = cn
    rowst_ref[KV_RANK:MLA_W, :] = kr
    rows_t = jnp.concatenate([cn, kr, jnp.zeros((2 * LANES - MLA_W, tm), F32)], axis=0)
    rows_ref[...] = rows_t.T.astype(rows_ref.dtype)


def _mla_prep_t(cq_t, ckv_t, cos2_t, sin2_t, q_norm_g, wuq_ext_t, kv_norm_g, wuk, *, tm):
    b, _, t = cq_t.shape
    nt = t // tm
    blk = lambda w: pl.BlockSpec((None, w, tm), lambda bi, i: (bi, 0, i))
    pos = pl.BlockSpec((ROPE_B, tm), lambda bi, i: (0, i))
    return pl.pallas_call(
        _mla_prep_t_kernel,
        grid=(b, nt),
        in_specs=[blk(Q_RANK), blk(2 * LANES), pos, pos, _full((Q_RANK, 1)), _full(wuq_ext_t.shape),
                  _full((KV_RANK, 1)), _full(wuk.shape)],
        out_specs=[pl.BlockSpec((None, 2 * LANES, H_B * tm), lambda bi, i: (bi * nt + i, 0, 0)),
                   pl.BlockSpec((None, tm, 2 * LANES), lambda bi, i: (bi, i, 0)),
                   blk(MLA_W)],
        out_shape=[jax.ShapeDtypeStruct((b * nt, 2 * LANES, H_B * tm), BF16),
                   jax.ShapeDtypeStruct((b, t, 2 * LANES), BF16),
                   jax.ShapeDtypeStruct((b, MLA_W, t), F32)],
        compiler_params=_cparams(("parallel", "parallel")),
        name="mla_prep_t",
    )(cq_t, ckv_t, cos2_t, sin2_t, q_norm_g.reshape(Q_RANK, 1), wuq_ext_t, kv_norm_g.reshape(KV_RANK, 1), wuk)


def _softmax_update_t(s, m_ref, l_ref, cols):
    m_old = m_ref[:, cols]
    m_new = jnp.maximum(m_old, jnp.max(s, axis=0, keepdims=True))
    alpha = jnp.exp(m_old - m_new)
    p = jnp.exp(s - m_new)
    l_ref[:, cols] = alpha * l_ref[:, cols] + jnp.sum(p, axis=0, keepdims=True)
    m_ref[:, cols] = m_new
    return alpha, p


def _bias_kernel(rb_ref, dist_ref, o_ref):
    dist = dist_ref[...]
    n = jnp.maximum(dist, 0)
    max_exact = NUM_BUCKETS // 2
    n_safe = jnp.maximum(n, max_exact).astype(F32)
    large = max_exact + (jnp.log(n_safe / max_exact) / math.log(MAX_DISTANCE / max_exact)
                         * (NUM_BUCKETS - max_exact)).astype(jnp.int32)
    large = jnp.minimum(large, NUM_BUCKETS - 1)
    bucket = jnp.where(n < max_exact, n, large)
    for h in range(H_C):
        bias = jnp.zeros(dist.shape, F32)
        for kb in range(NUM_BUCKETS):
            bias = jnp.where(bucket == kb, rb_ref[kb * H_C + h], bias)
        o_ref[h] = jnp.where(dist >= 0, bias, NEG_INF)


def _bias_tiles(rel_bias, dist):
    g, r, c = dist.shape
    grid_spec = pltpu.PrefetchScalarGridSpec(
        num_scalar_prefetch=1,
        grid=(g,),
        in_specs=[pl.BlockSpec((None, r, c), lambda i, rb: (i, 0, 0))],
        out_specs=pl.BlockSpec((None, H_C, r, c), lambda i, rb: (i, 0, 0, 0)),
    )
    return pl.pallas_call(
        _bias_kernel,
        grid_spec=grid_spec,
        out_shape=jax.ShapeDtypeStruct((g, H_C, r, c), F32),
        compiler_params=_cparams(("arbitrary",)),
        name="rel_bias_tiles",
    )(rel_bias.reshape(-1), dist)


def _diff_lambda(lam_ref, lam_init):
    lam = lam_ref[...]
    e1 = jnp.exp(jnp.sum(lam[0:1] * lam[1:2], axis=-1, keepdims=True))
    e2 = jnp.exp(jnp.sum(lam[2:3] * lam[3:4], axis=-1, keepdims=True))
    return e1 - e2 + lam_init


def _diff_queries(qc):
    lane = lax.broadcasted_iota(jnp.int32, qc.shape, 1)
    qs = qc * DIFF_SCALE
    groups = []
    for h in range(H_C):
        for c in range(2):
            lo = h * DV_C + c * DC
            groups.append(jnp.where((lane >= lo) & (lane < lo + DC), qs, 0.0))
    return jnp.concatenate(groups, axis=0)


def _diff_out(acc, l, lam, lam_init, sg, gc, o_ref, rows):
    for h in range(H_C):
        hs = slice(h * DV_C, (h + 1) * DV_C)
        r1 = slice((2 * h) * rows, (2 * h + 1) * rows)
        r2 = slice((2 * h + 1) * rows, (2 * h + 2) * rows)
        o = acc[r1, hs] / l[r1] - lam * (acc[r2, hs] / l[r2])
        o = _rms(o, sg, SUBLN_EPS) * (1.0 - lam_init)
        o_ref[:, hs] = (o * _silu(gc[:, hs])).astype(o_ref.dtype)


def _prompt_attn_kernel(pi_ref, pj_ref,
                        q_ref, k_ref, ct_ref, mask_ref, gb_ref, wuvt_ref,
                        qc_ref, kc_ref, vt_ref, gc_ref, bias_ref, lam_ref, sg_ref,
                        ob_ref, oc_ref,
                        m_ref, l_ref, acc_ref, qbd_ref, md_ref, ld_ref, accd_ref, *, lam_init):
    step = pl.program_id(1)
    i = pi_ref[step]
    j = pj_ref[step]
    tq = ob_ref.shape[0]
    nblk = 2 * H_C

    @pl.when(j == 0)
    def _():
        m_ref[...] = jnp.full_like(m_ref, NEG_INF)
        l_ref[...] = jnp.zeros_like(l_ref)
        acc_ref[...] = jnp.zeros_like(acc_ref)
        q = qc_ref[...] * DIFF_SCALE
        feat = lax.broadcasted_iota(jnp.int32, q.shape, 0)
        for blk in range(nblk):
            lo = (blk // 2) * DV_C + (blk % 2) * DC
            qbd_ref[:, blk * tq:(blk + 1) * tq] = jnp.where((feat >= lo) & (feat < lo + DC), q, 0.0).astype(BF16)
        md_ref[...] = jnp.full_like(md_ref, NEG_INF)
        ld_ref[...] = jnp.zeros_like(ld_ref)
        accd_ref[...] = jnp.zeros_like(accd_ref)

    k = k_ref[...].astype(BF16)
    ct = ct_ref[...].astype(BF16)
    mask = mask_ref[jnp.minimum(i - j, 1)]
    heads = range(H_B)
    cols = [slice(h * tq, (h + 1) * tq) for h in heads]
    kc = kc_ref[...].astype(BF16)
    vt = vt_ref[...].astype(BF16)
    tile = jnp.minimum(i - j, 2)
    blocks = range(nblk)
    dcols = [slice(blk * tq, (blk + 1) * tq) for blk in blocks]
    s_b = [_dot(k, q_ref[:, cols[h]]) + mask for h in heads]
    s_d = [_dot(kc, qbd_ref[:, dcols[blk]]) + bias_ref[tile, blk // 2] for blk in blocks]
    ap_b = [_softmax_update_t(s_b[h], m_ref, l_ref, cols[h]) for h in heads]
    ap_d = [_softmax_update_t(s_d[blk], md_ref, ld_ref, dcols[blk]) for blk in blocks]
    pv_b = [_dot(ct, ap_b[h][1].astype(BF16)) for h in heads]
    pv_d = [_dot(vt[(blk // 2) * DV_C:(blk // 2 + 1) * DV_C], ap_d[blk][1].astype(BF16)) for blk in blocks]
    for h in heads:
        acc_ref[:, cols[h]] = ap_b[h][0] * acc_ref[:, cols[h]] + pv_b[h]
    for blk in blocks:
        accd_ref[blk] = ap_d[blk][0] * accd_ref[blk] + pv_d[blk]

    @pl.when(j == i)
    def _():
        outs = []
        for h in heads:
            o_lat = (acc_ref[:, cols[h]] / l_ref[:, cols[h]]).astype(BF16)
            outs.append(_dot(wuvt_ref[h], o_lat))
        o = jnp.concatenate(outs, axis=0).T
        ob_ref[...] = (o * _silu(gb_ref[...])).astype(ob_ref.dtype)
        lam = _diff_lambda(lam_ref, lam_init)
        outs = []
        for h in range(H_C):
            o = (accd_ref[2 * h] / ld_ref[:, dcols[2 * h]]
                 - lam * (accd_ref[2 * h + 1] / ld_ref[:, dcols[2 * h + 1]]))
            o = o * lax.rsqrt(jnp.mean(o * o, axis=0, keepdims=True) + SUBLN_EPS) * sg_ref[...]
            outs.append(o * (1.0 - lam_init))
        o = jnp.concatenate(outs, axis=0).T
        oc_ref[...] = (o * _silu(gc_ref[...])).astype(oc_ref.dtype)


def _prompt_attn(q_t, rows_pad, rows_t, mask, gb, wuv_t, qc_t, kc, vc_t, gc, bias_tiles, lam_vecs, subln_g, lam_init):
    b, t, _ = kc.shape
    tq = ATTN_TILE
    nq = t // tq
    pairs = [(i, j) for i in range(nq) for j in range(i + 1)]
    pair_i = jnp.asarray([p[0] for p in pairs], jnp.int32)
    pair_j = jnp.asarray([p[1] for p in pairs], jnp.int32)
    nblk = 2 * H_C
    q_tile = lambda w: pl.BlockSpec((None, tq, w), lambda bi, s, pi, pj: (bi, pi[s], 0))
    k_tile = lambda w: pl.BlockSpec((None, tq, w), lambda bi, s, pi, pj: (bi, pj[s], 0))
    qt_tile = lambda w: pl.BlockSpec((None, w, tq), lambda bi, s, pi, pj: (bi, 0, pi[s]))
    kt_tile = lambda w: pl.BlockSpec((None, w, tq), lambda bi, s, pi, pj: (bi, 0, pj[s]))
    const = lambda shape: pl.BlockSpec(shape, lambda bi, s, pi, pj: (0,) * len(shape))
    grid_spec = pltpu.PrefetchScalarGridSpec(
        num_scalar_prefetch=2,
        grid=(b, len(pairs)),
        in_specs=[pl.BlockSpec((None, 2 * LANES, H_B * tq), lambda bi, s, pi, pj: (bi * nq + pi[s], 0, 0)),
                  k_tile(2 * LANES), kt_tile(KV_RANK), const(mask.shape), q_tile(D_B), const(wuv_t.shape),
                  qt_tile(D_C), k_tile(D_C), kt_tile(D_C), q_tile(D_C), const(bias_tiles.shape),
                  const(lam_vecs.shape), const((DV_C, 1))],
        out_specs=[q_tile(D_B), q_tile(D_C)],
        scratch_shapes=[pltpu.VMEM((1, H_B * tq), F32), pltpu.VMEM((1, H_B * tq), F32),
                        pltpu.VMEM((KV_RANK, H_B * tq), F32),
                        pltpu.VMEM((D_C, nblk * tq), BF16), pltpu.VMEM((1, nblk * tq), F32),
                        pltpu.VMEM((1, nblk * tq), F32), pltpu.VMEM((nblk, DV_C, tq), F32)],
    )
    return pl.pallas_call(
        functools.partial(_prompt_attn_kernel, lam_init=lam_init),
        grid_spec=grid_spec,
        out_shape=[jax.ShapeDtypeStruct((b, t, D_B), BF16), jax.ShapeDtypeStruct((b, t, D_C), BF16)],
        compiler_params=_cparams(("parallel", "arbitrary")),
        name="prompt_attn",
    )(pair_i, pair_j, q_t, rows_pad, rows_t, mask, gb, wuv_t, qc_t, kc, vc_t, gc, bias_tiles, lam_vecs,
      subln_g.reshape(DV_C, 1))


def _softmax_step(s, m, l):
    m_new = jnp.maximum(m, jnp.max(s, axis=-1, keepdims=True))
    alpha = jnp.exp(m - m_new)
    p = jnp.exp(s - m_new)
    return m_new, alpha * l + jnp.sum(p, axis=-1, keepdims=True), alpha, p


def _decode_attn_kernel(pt_ref, q_ref, row_ref, gb_ref, wuv_ref, qc_ref, kn_ref, vn_ref, gc_ref, bias_ref, lam_ref,
                        sg_ref, cm_hbm, ck_hbm, cv_hbm, ob_ref, oc_ref, mbuf, kbuf, vbuf, sems,
                        *, layer, pages_per_seq, lam_init):
    n_pages = DECODE_PAGES_PER_CHUNK
    seq = pl.program_id(0)
    chunks = pages_per_seq // n_pages
    total = pl.num_programs(0) * chunks

    def chunk_copies(g, slot):
        out = []
        for p in range(n_pages):
            page = pt_ref[g * n_pages + p]
            out.append(pltpu.make_async_copy(cm_hbm.at[layer, page], mbuf.at[slot, p], sems.at[slot, 0]))
            out.append(pltpu.make_async_copy(ck_hbm.at[layer, page], kbuf.at[slot, p], sems.at[slot, 1]))
            out.append(pltpu.make_async_copy(cv_hbm.at[layer, page], vbuf.at[slot, p], sems.at[slot, 2]))
        return out

    @pl.when(seq == 0)
    def _():
        for cp in chunk_copies(0, 0):
            cp.start()

    q = q_ref[...]
    qbd = _diff_queries(qc_ref[...])
    qb = qbd.astype(BF16)

    def chunk_body(c, carry):
        mb, lb, accb, md, ld, accd = carry
        g = seq * chunks + c
        slot = lax.rem(g, 2)

        @pl.when(g + 1 < total)
        def _():
            for cp in chunk_copies(g + 1, 1 - slot):
                cp.start()

        for cp in chunk_copies(g, slot):
            cp.wait()

        pages = range(n_pages)
        lanes = [slice(p * PAGE_SIZE, (p + 1) * PAGE_SIZE) for p in pages]
        ks = [mbuf[slot, p].astype(BF16) for p in pages]
        s_b = jnp.concatenate([_dot(q, ks[p]) for p in pages], axis=1)
        s_d = jnp.concatenate([_dot(qb, kbuf[slot, p].astype(BF16)) for p in pages], axis=1)
        s_d = s_d + bias_ref[jnp.where(c == chunks - 1, 1, 0)]
        mb, lb, alpha_b, p_b = _softmax_step(s_b, mb, lb)
        md, ld, alpha_d, p_d = _softmax_step(s_d, md, ld)
        p_b = p_b.astype(BF16)
        p_d = p_d.astype(BF16)
        pv_b = [_dot_nt(p_b[:, lanes[p]], ks[p][0:KV_RANK, :]) for p in pages]
        pv_d = [_dot_nt(p_d[:, lanes[p]], vbuf[slot, p].astype(BF16)) for p in pages]
        accb = alpha_b * accb + functools.reduce(lambda x, y: x + y, pv_b)
        accd = alpha_d * accd + functools.reduce(lambda x, y: x + y, pv_d)
        return mb, lb, accb, md, ld, accd

    init = (jnp.full((H_B, 1), NEG_INF, F32), jnp.zeros((H_B, 1), F32), jnp.zeros((H_B, KV_RANK), F32),
            jnp.full((2 * H_C, 1), NEG_INF, F32), jnp.zeros((2 * H_C, 1), F32), jnp.zeros((2 * H_C, D_C), F32))
    mb, lb, accb, md, ld, accd = lax.fori_loop(0, chunks, chunk_body, init)

    row = row_ref[...]
    s_new = jnp.sum(q.astype(F32) * row, axis=-1, keepdims=True)
    mb, lb, alpha, p_new = _softmax_step(s_new, mb, lb)
    accb = alpha * accb + p_new * row[:, 0:KV_RANK]
    _mla_out(accb / lb, wuv_ref, gb_ref[...], ob_ref, 1)

    s_new = jnp.sum(qbd * kn_ref[...], axis=-1, keepdims=True) + bias_ref[2][:, 0:1]
    md, ld, alpha, p_new = _softmax_step(s_new, md, ld)
    accd = alpha * accd + p_new * vn_ref[...]
    _diff_out(accd, ld, _diff_lambda(lam_ref, lam_init), lam_init, sg_ref[...], gc_ref[...], oc_ref, 1)


def _decode_attn(page_table_flat, q, rows_new, gb, wuv, qc, kc, vc, gc, bias_rows, lam_vecs, subln_g,
                 cache_mla, cache_k, cache_v, layer, pages_per_seq, lam_init):
    b = q.shape[0]
    n_pages = DECODE_PAGES_PER_CHUNK
    assert pages_per_seq % n_pages == 0
    per_b = lambda shape: pl.BlockSpec((None,) + shape, lambda bi, pt: (bi,) + (0,) * len(shape))
    const = lambda shape: pl.BlockSpec(shape, lambda bi, pt: (0,) * len(shape))
    hbm = pl.BlockSpec(memory_space=pl.ANY)
    grid_spec = pltpu.PrefetchScalarGridSpec(
        num_scalar_prefetch=1,
        grid=(b,),
        in_specs=[per_b((H_B, MLA_W)), per_b((1, MLA_W)), per_b((1, D_B)), const(wuv.shape),
                  per_b((1, D_C)), per_b((1, D_C)), per_b((1, D_C)), per_b((1, D_C)), const(bias_rows.shape),
                  const(lam_vecs.shape), const((1, DV_C)), hbm, hbm, hbm],
        out_specs=[per_b((1, D_B)), per_b((1, D_C))],
        scratch_shapes=[pltpu.VMEM((2, n_pages, MLA_W, PAGE_SIZE), F32),
                        pltpu.VMEM((2, n_pages, D_C, PAGE_SIZE), F32),
                        pltpu.VMEM((2, n_pages, D_C, PAGE_SIZE), F32),
                        pltpu.SemaphoreType.DMA((2, 3))],
    )
    return pl.pallas_call(
        functools.partial(_decode_attn_kernel, layer=layer, pages_per_seq=pages_per_seq, lam_init=lam_init),
        grid_spec=grid_spec,
        out_shape=[jax.ShapeDtypeStruct((b, 1, D_B), BF16), jax.ShapeDtypeStruct((b, 1, D_C), BF16)],
        compiler_params=_cparams(("arbitrary",)),
        name="decode_attn",
    )(page_table_flat, q, rows_new, gb, wuv, qc, kc, vc, gc, bias_rows, lam_vecs, subln_g.reshape(1, DV_C),
      cache_mla, cache_k, cache_v)


def _permute_w_in(w):
    o_ckv = A_IN + D_A + Q_RANK
    o_kr = o_ckv + KV_RANK
    o_gb = o_kr + ROPE_B
    half = ROPE_B // 2
    pad = jnp.zeros((w.shape[0], 2 * LANES - KV_RANK - 2 * ROPE_B), w.dtype)
    out = jnp.concatenate([w[:, :o_gb], w[:, o_kr + half:o_gb], w[:, o_kr:o_kr + half], pad, w[:, o_gb:]], axis=1)
    assert out.shape[1] == IN_COLS_PERM
    return out.astype(BF16)


def _extend_w_uq(w):
    w = w.reshape(Q_RANK, H_B, NOPE_B + ROPE_B)
    half = ROPE_B // 2
    rope = w[:, :, NOPE_B:]
    swapped = jnp.concatenate([rope[:, :, half:], rope[:, :, :half]], axis=-1)
    return jnp.concatenate([w, swapped], axis=-1).reshape(Q_RANK, H_B * LANES).astype(BF16)


def _rope_tables(pos):
    inv = ROPE_THETA ** (-jnp.arange(0, ROPE_B, 2, dtype=F32) / ROPE_B)
    ang = pos.astype(F32)[:, None] * inv[None, :]
    cos, sin = jnp.cos(ang), jnp.sin(ang)
    return jnp.concatenate([cos, cos], axis=-1), jnp.concatenate([-sin, sin], axis=-1)


def _layer_weights(l, W):
    row = lambda a: a.reshape(1, -1)
    rwkv = (row(W["mu_shift"][l]), row(W["rw_w0"][l]), W["rw_w2"][l], row(W["rw_a0"][l]), W["rw_a2"][l],
            row(W["rw_k_k"][l]), row(W["rw_k_a"][l]), row(W["rw_r_k"][l]), row(W["rw_gn_g"][l]),
            row(W["rw_gn_b"][l]))
    col = lambda a: a.reshape(-1, 1)
    rwkv_t = (col(W["mu_shift"][l]), col(W["rw_w0"][l]), W["rw_w2"][l].T, col(W["rw_a0"][l]), W["rw_a2"][l].T,
              col(W["rw_k_k"][l]), col(W["rw_k_a"][l]), col(W["rw_r_k"][l]), col(W["rw_gn_g"][l]),
              col(W["rw_gn_b"][l]))
    w_in = _permute_w_in(W["w_in"][l])
    wuq = _extend_w_uq(W["mla_w_uq"][l])
    return dict(
        w_in_n=jnp.concatenate([w_in[:, a:b] for a, b in PROMPT_SEGS_N], axis=1),
        w_in_t=jnp.concatenate([w_in[:, a:b] for a, b in PROMPT_SEGS_T], axis=1).T,
        w_in_sn=jnp.concatenate([w_in[:, a:b] for a, b in SAMPLE_SEGS_N], axis=1),
        w_in_st=jnp.concatenate([w_in[:, a:b] for a, b in SAMPLE_SEGS_T], axis=1).T,
        rwkv=rwkv,
        rwkv_t=rwkv_t,
        wuq=wuq,
        wuq_t=wuq.T,
        wuk=jnp.transpose(W["mla_w_uk"][l], (1, 0, 2)).astype(BF16),
        wuv_t=jnp.transpose(W["mla_w_uv"][l], (1, 2, 0)).astype(BF16),
        wuk_t=jnp.transpose(W["mla_w_uk"][l], (1, 2, 0)).astype(BF16),
        wuv=jnp.transpose(W["mla_w_uv"][l], (1, 0, 2)).astype(BF16),
        lam_vecs=jnp.stack([W["diff_lam_q1"][l], W["diff_lam_k1"][l], W["diff_lam_q2"][l], W["diff_lam_k2"][l]]),
        wo=W["w_out"][l].astype(BF16),
        wple=W["w_ple"][l].astype(BF16),
        wg=W["w_ple_gate"][l].astype(BF16),
    )


def _run_prompt(x, p, W, LW, depth):
    b, t, d = x.shape
    m = b * t
    tm = ATTN_TILE
    cos2, sin2 = _rope_tables(jnp.arange(t, dtype=jnp.int32))
    cos2_t, sin2_t = cos2.T, sin2.T
    tile = jnp.arange(ATTN_TILE, dtype=jnp.int32)
    dist = (jnp.arange(3, dtype=jnp.int32)[:, None, None] * ATTN_TILE + tile[None, None, :] - tile[None, :, None])
    bias_tiles = _bias_tiles(W["rel_bias"], dist)
    causal = jnp.stack([jnp.where(dist[0] >= 0, 0.0, NEG_INF).astype(F32), jnp.zeros(dist.shape[1:], F32)])
    uprev0 = jnp.zeros((b, 1, A_IN), F32)
    s0 = jnp.zeros((b, H_A, HEAD_A, HEAD_A), F32)
    segs_n = _pack_segments(PROMPT_SEGS_N)
    segs_t = _pack_segments(PROMPT_SEGS_T)
    h = x.reshape(m, d)
    mla_rows, k_rows, v_rows, wkv_out, shift_out = [], [], [], [], []
    for l in range(depth):
        lw = LW[l]
        u, ga, gb, gc, kc, cq_t, ckv_t, qc_t, kc_t, vc_t, xn_last = _inproj(
            h, W["norm_g"][l], lw["w_in_n"], segs_n, normalize=True, rows_per_seq=t, tm=tm,
            wt_bf16=lw["w_in_t"], segs_t=segs_t, seg_dtypes=PROMPT_DTYPES_N)
        y_a, s_new = _rwkv_chunked(u.reshape(b, t, A_IN), uprev0, ga.reshape(b, t, D_A), s0, lw["rwkv"])
        q_t, rows_pad, rows_t = _mla_prep_t(cq_t, ckv_t, cos2_t, sin2_t, W["mla_q_norm_g"][l], lw["wuq_t"],
                                            W["mla_kv_norm_g"][l], lw["wuk"], tm=tm)
        lam_init = 0.8 - 0.6 * math.exp(-0.3 * l)
        y_b, y_c = _prompt_attn(q_t, rows_pad, rows_t, causal, gb.reshape(b, t, D_B), lw["wuv_t"],
                                qc_t, kc.reshape(b, t, D_C), vc_t, gc.reshape(b, t, D_C), bias_tiles,
                                lw["lam_vecs"], W["diff_subln_g"][l], lam_init)
        h = _outproj(h, y_a.reshape(m, D_A), y_b.reshape(m, D_B), y_c.reshape(m, D_C), p[l].reshape(m, PLE_DIM),
                     lw["wo"], lw["wple"], lw["wg"], W["final_norm_g"], final=(l == depth - 1), tm=OUTPROJ_TILE)
        mla_rows.append(jnp.transpose(rows_t, (0, 2, 1)))
        k_rows.append(jnp.transpose(kc_t.reshape(b, H_C, 2 * DC, t), (0, 3, 1, 2)))
        v_rows.append(jnp.transpose(vc_t.reshape(b, H_C, DV_C, t), (0, 3, 1, 2)))
        wkv_out.append(s_new)
        shift_out.append(xn_last.reshape(b, d))
    return (h.reshape(b, t, d), jnp.stack(mla_rows), jnp.stack(k_rows), jnp.stack(v_rows), jnp.stack(wkv_out),
            jnp.stack(shift_out))


def _run_sample(x, p, state_shift, state_wkv, cache_mla, cache_k, cache_v, page_table, W, LW, depth):
    b, t, d = x.shape
    assert t == 1
    pages_per_seq = page_table.shape[1]
    past_len = pages_per_seq * PAGE_SIZE
    tm = b
    cos2, sin2 = _rope_tables(jnp.full((b,), past_len, dtype=jnp.int32))
    pt_flat = page_table.reshape(-1).astype(jnp.int32)
    step_keys = DECODE_PAGES_PER_CHUNK * PAGE_SIZE
    key_in_step = jnp.arange(step_keys, dtype=jnp.int32)
    dist = jnp.stack([past_len - key_in_step,
                      past_len - (past_len - step_keys + key_in_step),
                      jnp.zeros((step_keys,), jnp.int32)])
    dist = jnp.broadcast_to(dist[:, None, :], (3, 2, step_keys))
    bias = _bias_tiles(W["rel_bias"], dist)
    bias_rows = bias.reshape(3, 2 * H_C, step_keys)
    to_feature_major = lambda c: jnp.transpose(c, (0, 1, 3, 4, 2)).reshape(c.shape[:2] + (D_C, PAGE_SIZE))
    cache_k2 = to_feature_major(cache_k)
    cache_v2 = to_feature_major(cache_v)
    cache_mla_t = jnp.transpose(cache_mla, (0, 1, 3, 2))
    segs_n = _pack_segments(SAMPLE_SEGS_N)
    segs_t = _pack_segments(SAMPLE_SEGS_T)
    h = x.reshape(b, d)
    mla_rows, k_rows, v_rows, wkv_out, shift_out = [], [], [], [], []
    for l in range(depth):
        lw = LW[l]
        cq, ckv, gb, qc, kc, vc, gc, u_t, ga_t, xn = _inproj(
            h, W["norm_g"][l], lw["w_in_sn"], segs_n, normalize=True, rows_per_seq=1, tm=tm,
            wt_bf16=lw["w_in_st"], segs_t=segs_t)
        (uprev_t,) = _inproj(state_shift[l], W["norm_g"][l], None, (), normalize=False, rows_per_seq=1, tm=tm,
                             wt_bf16=lw["w_in_st"][:A_IN], segs_t=(SEG_U,))
        y_a_t, s_new_t = _rwkv_step(u_t[0], uprev_t[0], ga_t[0], jnp.transpose(state_wkv[l], (1, 2, 3, 0)),
                                    lw["rwkv_t"])
        y_a = y_a_t.T
        s_new = jnp.transpose(s_new_t, (3, 0, 1, 2))
        q, rows = _mla_prep(cq, ckv, cos2, sin2, W["mla_q_norm_g"][l], lw["wuq"], W["mla_kv_norm_g"][l],
                            lw["wuk_t"], tm=tm, pos_tiles=1)
        lam_init = 0.8 - 0.6 * math.exp(-0.3 * l)
        y_b, y_c = _decode_attn(pt_flat, jnp.transpose(q, (1, 0, 2)), rows.reshape(b, 1, MLA_W),
                                gb.reshape(b, 1, D_B), lw["wuv"], qc.reshape(b, 1, D_C), kc.reshape(b, 1, D_C),
                                vc.reshape(b, 1, D_C), gc.reshape(b, 1, D_C), bias_rows, lw["lam_vecs"],
                                W["diff_subln_g"][l], cache_mla_t, cache_k2, cache_v2, l, pages_per_seq, lam_init)
        h = _outproj(h, y_a, y_b.reshape(b, D_B), y_c.reshape(b, D_C), p[l].reshape(b, PLE_DIM),
                     lw["wo"], lw["wple"], lw["wg"], W["final_norm_g"], final=(l == depth - 1), tm=tm)
        mla_rows.append(rows.reshape(b, 1, MLA_W))
        k_rows.append(kc.reshape(b, 1, H_C, 2 * DC))
        v_rows.append(vc.reshape(b, 1, H_C, DV_C))
        wkv_out.append(s_new)
        shift_out.append(xn)
    return (h.reshape(b, 1, d), jnp.stack(mla_rows), jnp.stack(k_rows), jnp.stack(v_rows), jnp.stack(wkv_out),
            jnp.stack(shift_out))


def kernel(x_prompt, x_sample, cache_mla, cache_diff_k, cache_diff_v, state_wkv, state_shift, page_table,
           p_prompt, p_sample, norm_g, w_in, mu_shift, rw_w0, rw_w2, rw_a0, rw_a2, rw_k_k, rw_k_a, rw_r_k,
           rw_gn_g, rw_gn_b, mla_q_norm_g, mla_w_uq, mla_kv_norm_g, mla_w_uk, mla_w_uv, diff_lam_q1,
           diff_lam_k1, diff_lam_q2, diff_lam_k2, diff_subln_g, rel_bias, w_out, w_ple, w_ple_gate,
           final_norm_g):
    W = {"norm_g": norm_g, "w_in": w_in, "mu_shift": mu_shift, "rw_w0": rw_w0, "rw_w2": rw_w2, "rw_a0": rw_a0,
         "rw_a2": rw_a2, "rw_k_k": rw_k_k, "rw_k_a": rw_k_a, "rw_r_k": rw_r_k, "rw_gn_g": rw_gn_g,
         "rw_gn_b": rw_gn_b, "mla_q_norm_g": mla_q_norm_g, "mla_w_uq": mla_w_uq, "mla_kv_norm_g": mla_kv_norm_g,
         "mla_w_uk": mla_w_uk, "mla_w_uv": mla_w_uv, "diff_lam_q1": diff_lam_q1, "diff_lam_k1": diff_lam_k1,
         "diff_lam_q2": diff_lam_q2, "diff_lam_k2": diff_lam_k2, "diff_subln_g": diff_subln_g,
         "rel_bias": rel_bias, "w_out": w_out, "w_ple": w_ple, "w_ple_gate": w_ple_gate,
         "final_norm_g": final_norm_g}
    depth = w_in.shape[0]
    LW = [_layer_weights(l, W) for l in range(depth)]
    y_p, mla_p, dk_p, dv_p, wkv_p, sh_p = _run_prompt(x_prompt, p_prompt, W, LW, depth)
    y_s, mla_s, dk_s, dv_s, wkv_s, sh_s = _run_sample(x_sample, p_sample, state_shift, state_wkv, cache_mla,
                                                      cache_diff_k, cache_diff_v, page_table, W, LW, depth)
    return (y_p, y_s, mla_p, mla_s, dk_p, dk_s, dv_p, dv_s, wkv_p, wkv_s, sh_p, sh_s)
```

```python
import functools
import math

import jax
import jax.numpy as jnp
from jax import lax
from jax.experimental import pallas as pl
from jax.experimental.pallas import tpu as pltpu

F32 = jnp.float32
BF16 = jnp.bfloat16

LANES = 128
SUBLANES = 8
VMEM_LIMIT_BYTES = 56 * 1024 * 1024

D_MODEL = 1024
HEAD_A = 64
D_A = 512
H_A = D_A // HEAD_A
W_LORA = 64
A_LORA = 64
A_IN = 3 * D_A + W_LORA + A_LORA
D_B = 256
DV_B = 64
H_B = D_B // DV_B
NOPE_B = 64
ROPE_B = 32
Q_RANK = 256
KV_RANK = 128
MLA_W = KV_RANK + ROPE_B
ROPE_THETA = 10000.0
D_C = 256
DV_C = 64
H_C = D_C // DV_C
DC = DV_C // 2
NUM_BUCKETS = 32
MAX_DISTANCE = 128
PLE_DIM = 256
PAGE_SIZE = 128
NEG_INF = -1e30
EPS = 1e-6
GN_EPS = 64e-5
SUBLN_EPS = 1e-5
MLA_SCALE = (NOPE_B + ROPE_B) ** -0.5
DIFF_SCALE = DC ** -0.5
LOG2E = math.log2(math.e)

SEG_U = (0, A_IN)
SEG_GA = (A_IN, A_IN + D_A)
SEG_CQ = (SEG_GA[1], SEG_GA[1] + Q_RANK)
SEG_CKV = (SEG_CQ[1], SEG_CQ[1] + 2 * LANES)
SEG_GB = (SEG_CKV[1], SEG_CKV[1] + D_B)
SEG_QC = (SEG_GB[1], SEG_GB[1] + D_C)
SEG_KC = (SEG_QC[1], SEG_QC[1] + D_C)
SEG_VC = (SEG_KC[1], SEG_KC[1] + D_C)
SEG_GC = (SEG_VC[1], SEG_VC[1] + D_C)
IN_COLS_PERM = SEG_GC[1]


def _pack_segments(segs):
    out, pos = [], 0
    for a, b in segs:
        out.append((pos, pos + b - a))
        pos += b - a
    return tuple(out)


PROMPT_SEGS_N = (SEG_U, SEG_GA, SEG_GB, SEG_GC, SEG_KC)
PROMPT_SEGS_T = (SEG_CQ, SEG_CKV, SEG_QC, SEG_KC, SEG_VC)
PROMPT_DTYPES_N = (F32, F32, F32, F32, BF16)
SAMPLE_SEGS_N = (SEG_CQ, SEG_CKV, SEG_GB, SEG_QC, SEG_KC, SEG_VC, SEG_GC)
SAMPLE_SEGS_T = (SEG_U, SEG_GA)

RWKV_CHUNK = 64
RWKV_BLOCK = 256
ATTN_TILE = 512
OUTPROJ_TILE = 512
DECODE_PAGES_PER_CHUNK = 64


def _cparams(semantics):
    return pltpu.CompilerParams(dimension_semantics=semantics, vmem_limit_bytes=VMEM_LIMIT_BYTES)


def _full(shape):
    n = len(shape)
    return pl.BlockSpec(shape, lambda *_: (0,) * n)


def _sigmoid(x):
    return 1.0 / (1.0 + jnp.exp(-x))


def _silu(x):
    return x * _sigmoid(x)


def _rms(x, g, eps):
    return x * lax.rsqrt(jnp.mean(x * x, axis=-1, keepdims=True) + eps) * g


def _dot(a, b, **kw):
    return jnp.dot(a, b, preferred_element_type=F32, **kw)


def _dot_nt(a, b, **kw):
    return lax.dot_general(a, b, (((1,), (1,)), ((), ())), preferred_element_type=F32, **kw)


def _dot_tn(a, b, **kw):
    return lax.dot_general(a, b, (((0,), (0,)), ((), ())), preferred_element_type=F32, **kw)


def _split3(x):
    hi = x.astype(BF16)
    rest = x - hi.astype(F32)
    mid = rest.astype(BF16)
    lo = (rest - mid.astype(F32)).astype(BF16)
    return hi, mid, lo


def _dot_split_rhs(a_exact, b):
    hi, mid, lo = _split3(b)
    return _dot(a_exact, hi) + (_dot(a_exact, mid) + _dot(a_exact, lo))


def _dot_split_lhs(a, b_exact):
    hi, mid, lo = _split3(a)
    return _dot(hi, b_exact) + (_dot(mid, b_exact) + _dot(lo, b_exact))


def _mm(a, b):
    return _dot(a.astype(BF16), b.astype(BF16))


def _mm_nt(a, b):
    return _dot_nt(a.astype(BF16), b.astype(BF16))


def _mm_tn(a, b):
    return _dot_tn(a.astype(BF16), b.astype(BF16))


def _inproj_kernel(h_ref, g_ref, w_ref, wt_ref, *out_refs, normalize, segs, segs_t, emit_xn):
    x = h_ref[...]
    xn = _rms(x, g_ref[...], EPS) if normalize else x
    xb = xn.astype(BF16)
    for o_ref, (a, b) in zip(out_refs, segs):
        o_ref[...] = _dot(xb, w_ref[:, a:b]).astype(o_ref.dtype)
    for o_ref, (a, b) in zip(out_refs[len(segs):], segs_t):
        o_ref[...] = _dot_nt(wt_ref[a:b, :], xb)
    n_proj = len(segs) + len(segs_t)
    if emit_xn == "last_row":
        rows = x.shape[0]
        out_refs[n_proj][...] = xn[rows - 1:rows, :]
    elif emit_xn == "all":
        out_refs[n_proj][...] = xn


def _inproj(h2d, norm_g, w_bf16, segs, *, normalize, rows_per_seq, tm, wt_bf16=None, segs_t=(), seg_dtypes=None):
    m, d = h2d.shape
    assert m % tm == 0 and rows_per_seq % tm == 0 or rows_per_seq == 1
    tiles_per_seq = max(rows_per_seq // tm, 1)
    t_cols = rows_per_seq if rows_per_seq > 1 else m
    t_tiles = t_cols // tm
    if w_bf16 is None:
        w_bf16 = jnp.zeros((d, LANES), BF16)
    if wt_bf16 is None:
        wt_bf16 = jnp.zeros((SUBLANES, d), BF16)
    seg_dtypes = seg_dtypes or (F32,) * len(segs)
    out_shapes = [jax.ShapeDtypeStruct((m, b - a), dt) for (a, b), dt in zip(segs, seg_dtypes)]
    out_specs = [pl.BlockSpec((tm, b - a), lambda i: (i, 0)) for a, b in segs]
    for a, b in segs_t:
        out_shapes.append(jax.ShapeDtypeStruct((m // t_cols, b - a, t_cols), F32))
        out_specs.append(pl.BlockSpec((None, b - a, tm), lambda i: (i // t_tiles, 0, i % t_tiles)))
    emit_xn = None
    if normalize:
        if rows_per_seq == 1:
            emit_xn = "all"
            out_shapes.append(jax.ShapeDtypeStruct((m, d), F32))
            out_specs.append(pl.BlockSpec((tm, d), lambda i: (i, 0)))
        else:
            emit_xn = "last_row"
            out_shapes.append(jax.ShapeDtypeStruct((m // rows_per_seq, 1, d), F32))
            out_specs.append(pl.BlockSpec((None, 1, d), lambda i: (i // tiles_per_seq, 0, 0)))
    kern = functools.partial(_inproj_kernel, normalize=normalize, segs=segs, segs_t=segs_t, emit_xn=emit_xn)
    return pl.pallas_call(
        kern,
        grid=(m // tm,),
        in_specs=[pl.BlockSpec((tm, d), lambda i: (i, 0)), _full((1, d)), _full(w_bf16.shape), _full(wt_bf16.shape)],
        out_specs=out_specs,
        out_shape=out_shapes,
        compiler_params=_cparams(("arbitrary",)),
        name="inproj",
    )(h2d, norm_g.reshape(1, d), w_bf16, wt_bf16)


def _outproj_kernel(h_ref, ya_ref, yb_ref, yc_ref, p_ref, wo_ref, wple_ref, wg_ref, fng_ref, o_ref, *, final):
    mixed = (_dot(ya_ref[...], wo_ref[0:D_A, :])
             + _dot(yb_ref[...], wo_ref[D_A:D_A + D_B, :])
             + _dot(yc_ref[...], wo_ref[D_A + D_B:, :]))
    h2 = h_ref[...] + mixed
    ple = _dot(p_ref[...].astype(BF16), wple_ref[...])
    gate = _sigmoid(_dot(h2.astype(BF16), wg_ref[...]))
    h3 = h2 + ple * gate
    o_ref[...] = _rms(h3, fng_ref[...], EPS) if final else h3


def _outproj(h2d, ya, yb, yc, p2d, wo, wple, wg, final_g, *, final, tm):
    m, d = h2d.shape
    row = lambda w: pl.BlockSpec((tm, w), lambda i: (i, 0))
    return pl.pallas_call(
        functools.partial(_outproj_kernel, final=final),
        grid=(m // tm,),
        in_specs=[row(d), row(D_A), row(D_B), row(D_C), row(PLE_DIM),
                  _full(wo.shape), _full(wple.shape), _full(wg.shape), _full((1, d))],
        out_specs=row(d),
        out_shape=jax.ShapeDtypeStruct((m, d), F32),
        compiler_params=_cparams(("arbitrary",)),
        name="outproj",
    )(h2d, ya, yb, yc, p2d, wo, wple, wg, final_g.reshape(1, d))


def _rwkv_prep(um, w0, w2, a0, a2, k_k, k_a):
    r = um[:, 0:D_A]
    k = um[:, D_A:2 * D_A]
    v = um[:, 2 * D_A:3 * D_A]
    w_lo = um[:, 3 * D_A:3 * D_A + W_LORA]
    a_lo = um[:, 3 * D_A + W_LORA:A_IN]
    wl = w0 + _mm(jnp.tanh(w_lo), w2)
    neg = -wl
    softplus = jnp.maximum(neg, 0.0) + jnp.log(1.0 + jnp.exp(-jnp.abs(neg)))
    w = -softplus - 0.5
    log_decay = -jnp.exp(w)
    a = _sigmoid(a0 + _mm(a_lo, a2))
    kk = k * k_k
    k = k * (1.0 + (a - 1.0) * k_a)
    return r, k, v, kk, a, log_decay


def _rwkv_head_out(y, r_h, k_h, v_h, rk_h, gng_h, gnb_h, gate_h):
    mu = jnp.mean(y, axis=-1, keepdims=True)
    var = jnp.mean(jnp.square(y - mu), axis=-1, keepdims=True)
    yn = (y - mu) * lax.rsqrt(var + GN_EPS) * gng_h + gnb_h
    bonus = jnp.sum(r_h * k_h * rk_h, axis=-1, keepdims=True) * v_h
    return (yn + bonus) * _silu(gate_h)


def _rwkv_chunk_kernel(u_ref, uprev0_ref, ga_ref, s0_ref, ones_ref, mu_ref, w0_ref, w2_ref, a0_ref, a2_ref,
                       kk_ref, ka_ref, rk_ref, gng_ref, gnb_ref, y_ref, s_ref, prev_ref):
    step = pl.program_id(1)
    R = u_ref.shape[0]
    C = RWKV_CHUNK
    subs = range(R // C)

    @pl.when(step == 0)
    def _():
        prev_ref[...] = uprev0_ref[...]
        s_ref[...] = s0_ref[...]

    u = u_ref[...]
    row = lax.broadcasted_iota(jnp.int32, (R, 1), 0)
    u_prev = jnp.where(row == 0, prev_ref[...], pltpu.roll(u, 1, axis=0))
    prev_ref[...] = u[R - 1:R, :]
    um = u + mu_ref[...] * (u_prev - u)
    r, k, v, kk, a, log_decay = _rwkv_prep(um, w0_ref[...], w2_ref[...], a0_ref[...], a2_ref[...],
                                           kk_ref[...], ka_ref[...])

    ri = lax.broadcasted_iota(jnp.int32, (R, R), 0)
    rj = lax.broadcasted_iota(jnp.int32, (R, R), 1)
    same_chunk = (ri // C) == (rj // C)
    cs = _dot_split_rhs((same_chunk & (rj <= ri)).astype(BF16), log_decay)
    cs_last = [cs[(sb + 1) * C - 1:(sb + 1) * C, :] for sb in subs]
    cs_end = jnp.concatenate([jnp.broadcast_to(x, (C, x.shape[1])) for x in cs_last], axis=0)
    p_end = [jnp.exp(x) for x in cs_last]
    e_inv = jnp.exp(-cs)
    e_rem = jnp.exp(cs_end - cs)
    kk_n = kk / jnp.maximum(jnp.sqrt(_dot_split_lhs(kk * kk, ones_ref[...])), 1e-12)
    b_f = kk_n * a
    a_t = (-kk_n * jnp.exp(cs - log_decay)).astype(BF16)
    r_t = (r * jnp.exp(cs)).astype(BF16)
    b_t = (b_f * e_inv).astype(BF16)
    k_t = (k * e_inv).astype(BF16)
    b_end = (b_f * e_rem).astype(BF16)
    k_end = (k * e_rem).astype(BF16)
    v_b = v.astype(BF16)
    ga = ga_ref[...]

    t2 = lax.broadcasted_iota(jnp.int32, (C, 2 * C), 0)
    j2 = lax.broadcasted_iota(jnp.int32, (C, 2 * C), 1)
    j2 = jnp.where(j2 >= C, j2 - C, j2)
    strict2 = j2 < t2
    incl2 = j2 <= t2
    ti = lax.broadcasted_iota(jnp.int32, (C, C), 0)
    tj = lax.broadcasted_iota(jnp.int32, (C, C), 1)
    eye = (ti == tj).astype(F32)
    zeros = jnp.zeros((C, HEAD_A), BF16)

    units = [(sb, h) for sb in subs for h in range(H_A)]
    blk = lambda x, sb, h: x[sb * C:(sb + 1) * C, h * HEAD_A:(h + 1) * HEAD_A]
    a_h = {un: blk(a_t, *un) for un in units}
    r_h = {un: blk(r_t, *un) for un in units}
    v_h = {un: blk(v_b, *un) for un in units}
    gram = {un: _dot_nt(jnp.concatenate([a_h[un], r_h[un]], axis=0),
                        jnp.concatenate([blk(b_t, *un), blk(k_t, *un)], axis=0)) for un in units}
    l_top = {un: jnp.where(strict2, gram[un][0:C], 0.0) for un in units}
    m_bot = {un: jnp.where(incl2, gram[un][C:], 0.0).astype(BF16) for un in units}
    lv = {un: _dot(l_top[un].astype(BF16), jnp.concatenate([zeros, v_h[un]], axis=0)) for un in units}

    l_ab = {un: l_top[un][:, 0:C] for un in units}
    inv = {un: eye + l_ab[un] for un in units}
    pw = {un: l_ab[un].astype(BF16) for un in units}
    for _ in range(int(math.log2(C)) - 1):
        pw = {un: _mm(pw[un], pw[un]).astype(BF16) for un in units}
        inv = {un: inv[un] + _mm(inv[un], pw[un]) for un in units}
    inv = {un: inv[un].astype(BF16) for un in units}
    w_mat = {un: _mm(inv[un], a_h[un]).astype(BF16) for un in units}
    u_v = {un: _mm(inv[un], lv[un]) for un in units}

    state = [s_ref[h] for h in range(H_A)]
    for sb in subs:
        heads = [(sb, h) for h in range(H_A)]
        state_b = [x.astype(BF16) for x in state]
        uv = [jnp.concatenate([(_dot_nt(w_mat[un], state_b[un[1]]) + u_v[un]).astype(BF16), v_h[un]], axis=0)
              for un in heads]
        y = [_dot_nt(r_h[un], state_b[un[1]]) + _dot(m_bot[un], uv[un[1]]) for un in heads]
        state = [state[h] * p_end[sb][:, h * HEAD_A:(h + 1) * HEAD_A]
                 + _dot_tn(uv[h], jnp.concatenate([blk(b_end, sb, h), blk(k_end, sb, h)], axis=0))
                 for h in range(H_A)]
        rows = slice(sb * C, (sb + 1) * C)
        for h in range(H_A):
            hs = slice(h * HEAD_A, (h + 1) * HEAD_A)
            out = _rwkv_head_out(y[h], r[rows, hs], k[rows, hs], v[rows, hs], rk_ref[:, hs], gng_ref[:, hs],
                                 gnb_ref[:, hs], ga[rows, hs])
            y_ref[rows, hs] = out.astype(y_ref.dtype)
    for h in range(H_A):
        s_ref[h] = state[h]


def _rwkv_chunked(u, uprev0, ga, s0, params):
    b, t, _ = u.shape
    c = RWKV_BLOCK
    assert t % c == 0
    head_of_lane = jnp.arange(D_A, dtype=jnp.int32) // HEAD_A
    head_ones = (head_of_lane[:, None] == head_of_lane[None, :]).astype(BF16)
    tok = lambda w: pl.BlockSpec((None, c, w), lambda i, j: (i, j, 0))
    state = pl.BlockSpec((None, H_A, HEAD_A, HEAD_A), lambda i, j: (i, 0, 0, 0))
    return pl.pallas_call(
        _rwkv_chunk_kernel,
        grid=(b, t // c),
        in_specs=[tok(A_IN), pl.BlockSpec((None, 1, A_IN), lambda i, j: (i, 0, 0)), tok(D_A), state,
                  _full(head_ones.shape)] + [_full(p.shape) for p in params],
        out_specs=[tok(D_A), state],
        out_shape=[jax.ShapeDtypeStruct((b, t, D_A), BF16), jax.ShapeDtypeStruct(s0.shape, F32)],
        scratch_shapes=[pltpu.VMEM((1, A_IN), F32)],
        compiler_params=_cparams(("parallel", "arbitrary")),
        name="rwkv_chunked",
    )(u, uprev0, ga, s0, head_ones, *params)


def _rwkv_step_kernel(r_ref, k_ref, v_ref, lo_ref, rp_ref, kp_ref, vp_ref, lop_ref, ga_ref, s0_ref,
                      mur_ref, muk_ref, muv_ref, mulo_ref, w0_ref, w2t_ref, a0_ref, a2t_ref, kk_ref, ka_ref,
                      rk_ref, gng_ref, gnb_ref, y_ref, s_ref, y_scr):
    mix = lambda x_ref, p_ref, mu_ref: x_ref[...] + mu_ref[...] * (p_ref[...] - x_ref[...])
    r = mix(r_ref, rp_ref, mur_ref)
    k = mix(k_ref, kp_ref, muk_ref)
    v = mix(v_ref, vp_ref, muv_ref)
    lo = mix(lo_ref, lop_ref, mulo_ref)
    neg = -(w0_ref[...] + _mm(w2t_ref[...], jnp.tanh(lo[0:W_LORA])))
    w = -(jnp.maximum(neg, 0.0) + jnp.log(1.0 + jnp.exp(-jnp.abs(neg)))) - 0.5
    decay = jnp.exp(-jnp.exp(w))
    a = _sigmoid(a0_ref[...] + _mm(a2t_ref[...], lo[W_LORA:]))
    kk = k * kk_ref[...]
    kk = kk / jnp.maximum(jnp.sqrt(jnp.sum(kk * kk, axis=0, keepdims=True)), 1e-12)
    k = k * (1.0 + (a - 1.0) * ka_ref[...])
    b = kk * a
    nkk = -kk
    for i in range(HEAD_A):
        s = s0_ref[i]
        sa = jnp.sum(s * nkk, axis=0, keepdims=True)
        s_new = s * decay + sa * b + v[i:i + 1, :] * k
        s_ref[i] = s_new
        y_scr[i:i + 1, :] = jnp.sum(s_new * r, axis=0, keepdims=True)
    y = y_scr[...]
    mu = jnp.mean(y, axis=0, keepdims=True)
    var = jnp.mean(jnp.square(y - mu), axis=0, keepdims=True)
    yn = (y - mu) * lax.rsqrt(var + GN_EPS) * gng_ref[...] + gnb_ref[...]
    bonus = jnp.sum(r * k * rk_ref[...], axis=0, keepdims=True) * v
    y_ref[...] = ((yn + bonus) * _silu(ga_ref[...])).astype(y_ref.dtype)


def _rwkv_step(u_t, uprev_t, ga_t, s0_t, params_t):
    nb = u_t.shape[1]
    mu, w0, w2t, a0, a2t, k_k, k_a, r_k, gn_g, gn_b = params_t
    n_head_blocks = D_A // HEAD_A
    lora_block = 3 * D_A // (W_LORA + A_LORA)
    feat = lambda off: pl.BlockSpec((HEAD_A, nb), lambda h: (off * n_head_blocks + h, 0))
    lora = pl.BlockSpec((W_LORA + A_LORA, nb), lambda h: (lora_block, 0))
    col = lambda off: pl.BlockSpec((HEAD_A, 1), lambda h: (off * n_head_blocks + h, 0))
    lora_col = pl.BlockSpec((W_LORA + A_LORA, 1), lambda h: (lora_block, 0))
    head_rows = lambda w: pl.BlockSpec((HEAD_A, w), lambda h: (h, 0))
    state = pl.BlockSpec((None, HEAD_A, HEAD_A, nb), lambda h: (h, 0, 0, 0))
    return pl.pallas_call(
        _rwkv_step_kernel,
        grid=(H_A,),
        in_specs=[feat(0), feat(1), feat(2), lora, feat(0), feat(1), feat(2), lora, head_rows(nb), state,
                  col(0), col(1), col(2), lora_col, col(0), head_rows(W_LORA), col(0), head_rows(A_LORA),
                  col(0), col(0), col(0), col(0), col(0)],
        out_specs=[head_rows(nb), state],
        out_shape=[jax.ShapeDtypeStruct((D_A, nb), BF16), jax.ShapeDtypeStruct(s0_t.shape, F32)],
        scratch_shapes=[pltpu.VMEM((HEAD_A, nb), F32)],
        compiler_params=_cparams(("parallel",)),
        name="rwkv_step",
    )(u_t, u_t, u_t, u_t, uprev_t, uprev_t, uprev_t, uprev_t, ga_t, s0_t,
      mu, mu, mu, mu, w0, w2t, a0, a2t, k_k, k_a, r_k, gn_g, gn_b)


def _mla_prep_kernel(cq_ref, ckv_ref, cos_ref, sin_ref, qg_ref, wuq_ref, kvg_ref, wukt_ref, q_ref, rows_ref):
    cos2 = cos_ref[...]
    sin2 = sin_ref[...]
    qn = _rms(cq_ref[...], qg_ref[...], EPS).astype(BF16)
    q = _dot(qn, wuq_ref[...])
    for h in range(H_B):
        qh = q[:, h * LANES:(h + 1) * LANES]
        q_lat = _dot(qh[:, :NOPE_B].astype(BF16), wukt_ref[h])
        q_rope = qh[:, NOPE_B:NOPE_B + ROPE_B] * cos2 + qh[:, NOPE_B + ROPE_B:] * sin2
        q_ref[h, :, 0:KV_RANK] = (q_lat * MLA_SCALE).astype(q_ref.dtype)
        q_ref[h, :, KV_RANK:MLA_W] = (q_rope * MLA_SCALE).astype(q_ref.dtype)
    ckv = ckv_ref[...]
    rows_ref[:, 0:KV_RANK] = _rms(ckv[:, 0:KV_RANK], kvg_ref[...], EPS)
    rows_ref[:, KV_RANK:MLA_W] = (ckv[:, KV_RANK:KV_RANK + ROPE_B] * cos2
                                  + ckv[:, KV_RANK + ROPE_B:KV_RANK + 2 * ROPE_B] * sin2)


def _mla_prep(cq, ckv, cos2, sin2, q_norm_g, wuq_ext, kv_norm_g, wuk_t, *, tm, pos_tiles):
    m = cq.shape[0]
    row = lambda w: pl.BlockSpec((tm, w), lambda i: (i, 0))
    pos = pl.BlockSpec((tm, ROPE_B), lambda i: (i % pos_tiles, 0))
    return pl.pallas_call(
        _mla_prep_kernel,
        grid=(m // tm,),
        in_specs=[row(Q_RANK), row(2 * LANES), pos, pos, _full((1, Q_RANK)), _full(wuq_ext.shape),
                  _full((1, KV_RANK)), _full(wuk_t.shape)],
        out_specs=[pl.BlockSpec((H_B, tm, MLA_W), lambda i: (0, i, 0)), row(MLA_W)],
        out_shape=[jax.ShapeDtypeStruct((H_B, m, MLA_W), BF16), jax.ShapeDtypeStruct((m, MLA_W), F32)],
        compiler_params=_cparams(("parallel",)),
        name="mla_prep",
    )(cq, ckv, cos2, sin2, q_norm_g.reshape(1, Q_RANK), wuq_ext, kv_norm_g.reshape(1, KV_RANK), wuk_t)


def _mla_out(o_lat, wuv_ref, gb, o_ref, rows_per_head):
    for h in range(H_B):
        o_h = _dot(o_lat[h * rows_per_head:(h + 1) * rows_per_head].astype(BF16), wuv_ref[h])
        hs = slice(h * DV_B, (h + 1) * DV_B)
        o_ref[:, hs] = (o_h * _silu(gb[:, hs])).astype(o_ref.dtype)


def _mla_prep_t_kernel(cq_ref, ckv_ref, cos_ref, sin_ref, qg_ref, wuqt_ref, kvg_ref, wuk_ref,
                       q_ref, rows_ref, rowst_ref):
    tm = cq_ref.shape[1]
    cos2 = cos_ref[...]
    sin2 = sin_ref[...]
    cq = cq_ref[...]
    qn = (cq * lax.rsqrt(jnp.mean(cq * cq, axis=0, keepdims=True) + EPS) * qg_ref[...]).astype(BF16)
    q = _dot(wuqt_ref[...], qn)
    pad = jnp.zeros((2 * LANES - MLA_W, tm), q_ref.dtype)
    for h in range(H_B):
        qh = q[h * LANES:(h + 1) * LANES]
        q_lat = _dot(wuk_ref[h], qh[0:NOPE_B].astype(BF16))
        q_rope = qh[NOPE_B:NOPE_B + ROPE_B] * cos2 + qh[NOPE_B + ROPE_B:] * sin2
        cols = slice(h * tm, (h + 1) * tm)
        q_ref[0:KV_RANK, cols] = (q_lat * (MLA_SCALE * LOG2E)).astype(q_ref.dtype)
        q_ref[KV_RANK:MLA_W, cols] = (q_rope * (MLA_SCALE * LOG2E)).astype(q_ref.dtype)
        q_ref[MLA_W:, cols] = pad
    ckv = ckv_ref[...]
    c = ckv[0:KV_RANK]
    cn = c * lax.rsqrt(jnp.mean(c * c, axis=0, keepdims=True) + EPS) * kvg_ref[...]
    kr = ckv[KV_RANK:KV_RANK + ROPE_B] * cos2 + ckv[KV_RANK + ROPE_B:KV_RANK + 2 * ROPE_B] * sin2
    rowst_ref[0:KV_RANK, :] = cn
    rowst_ref[KV_RANK:MLA_W, :] = kr
    rows_t = jnp.concatenate([cn, kr, jnp.zeros((2 * LANES - MLA_W, tm), F32)], axis=0)
    rows_ref[...] = rows_t.T.astype(rows_ref.dtype)


def _mla_prep_t(cq_t, ckv_t, cos2_t, sin2_t, q_norm_g, wuq_ext_t, kv_norm_g, wuk, *, tm):
    b, _, t = cq_t.shape
    nt = t // tm
    blk = lambda w: pl.BlockSpec((None, w, tm), lambda bi, i: (bi, 0, i))
    pos = pl.BlockSpec((ROPE_B, tm), lambda bi, i: (0, i))
    return pl.pallas_call(
        _mla_prep_t_kernel,
        grid=(b, nt),
        in_specs=[blk(Q_RANK), blk(2 * LANES), pos, pos, _full((Q_RANK, 1)), _full(wuq_ext_t.shape),
                  _full((KV_RANK, 1)), _full(wuk.shape)],
        out_specs=[pl.BlockSpec((None, 2 * LANES, H_B * tm), lambda bi, i: (bi * nt + i, 0, 0)),
                   pl.BlockSpec((None, tm, 2 * LANES), lambda bi, i: (bi, i, 0)),
                   blk(MLA_W)],
        out_shape=[jax.ShapeDtypeStruct((b * nt, 2 * LANES, H_B * tm), BF16),
                   jax.ShapeDtypeStruct((b, t, 2 * LANES), BF16),
                   jax.ShapeDtypeStruct((b, MLA_W, t), F32)],
        compiler_params=_cparams(("parallel", "parallel")),
        name="mla_prep_t",
    )(cq_t, ckv_t, cos2_t, sin2_t, q_norm_g.reshape(Q_RANK, 1), wuq_ext_t, kv_norm_g.reshape(KV_RANK, 1), wuk)


def _softmax_update_t(s, m_ref, l_ref, cols):
    m_old = m_ref[:, cols]
    m_new = jnp.maximum(m_old, jnp.max(s, axis=0, keepdims=True))
    alpha = jnp.exp2(m_old - m_new)
    p = jnp.exp2(s - m_new)
    l_ref[:, cols] = alpha * l_ref[:, cols] + jnp.sum(p, axis=0, keepdims=True)
    m_ref[:, cols] = m_new
    return alpha, p


def _bias_kernel(rb_ref, dist_ref, o_ref):
    dist = dist_ref[...]
    n = jnp.maximum(dist, 0)
    max_exact = NUM_BUCKETS // 2
    n_safe = jnp.maximum(n, max_exact).astype(F32)
    large = max_exact + (jnp.log(n_safe / max_exact) / math.log(MAX_DISTANCE / max_exact)
                         * (NUM_BUCKETS - max_exact)).astype(jnp.int32)
    large = jnp.minimum(large, NUM_BUCKETS - 1)
    bucket = jnp.where(n < max_exact, n, large)
    for h in range(H_C):
        bias = jnp.zeros(dist.shape, F32)
        for kb in range(NUM_BUCKETS):
            bias = jnp.where(bucket == kb, rb_ref[kb * H_C + h], bias)
        o_ref[h] = jnp.where(dist >= 0, bias, NEG_INF)


def _bias_tiles(rel_bias, dist):
    g, r, c = dist.shape
    grid_spec = pltpu.PrefetchScalarGridSpec(
        num_scalar_prefetch=1,
        grid=(g,),
        in_specs=[pl.BlockSpec((None, r, c), lambda i, rb: (i, 0, 0))],
        out_specs=pl.BlockSpec((None, H_C, r, c), lambda i, rb: (i, 0, 0, 0)),
    )
    return pl.pallas_call(
        _bias_kernel,
        grid_spec=grid_spec,
        out_shape=jax.ShapeDtypeStruct((g, H_C, r, c), F32),
        compiler_params=_cparams(("arbitrary",)),
        name="rel_bias_tiles",
    )(rel_bias.reshape(-1), dist)


def _diff_lambda(lam_ref, lam_init):
    lam = lam_ref[...]
    e1 = jnp.exp(jnp.sum(lam[0:1] * lam[1:2], axis=-1, keepdims=True))
    e2 = jnp.exp(jnp.sum(lam[2:3] * lam[3:4], axis=-1, keepdims=True))
    return e1 - e2 + lam_init


def _diff_queries(qc):
    lane = lax.broadcasted_iota(jnp.int32, qc.shape, 1)
    qs = qc * DIFF_SCALE
    groups = []
    for h in range(H_C):
        for c in range(2):
            lo = h * DV_C + c * DC
            groups.append(jnp.where((lane >= lo) & (lane < lo + DC), qs, 0.0))
    return jnp.concatenate(groups, axis=0)


def _diff_out(acc, l, lam, lam_init, sg, gc, o_ref, rows):
    for h in range(H_C):
        hs = slice(h * DV_C, (h + 1) * DV_C)
        r1 = slice((2 * h) * rows, (2 * h + 1) * rows)
        r2 = slice((2 * h + 1) * rows, (2 * h + 2) * rows)
        o = acc[r1, hs] / l[r1] - lam * (acc[r2, hs] / l[r2])
        o = _rms(o, sg, SUBLN_EPS) * (1.0 - lam_init)
        o_ref[:, hs] = (o * _silu(gc[:, hs])).astype(o_ref.dtype)


def _prompt_attn_kernel(pi_ref, pj_ref,
                        q_ref, k_ref, ct_ref, mask_ref, gb_ref, wuvt_ref,
                        qc_ref, kc_ref, vt_ref, gc_ref, bias_ref, lam_ref, sg_ref,
                        ob_ref, oc_ref,
                        m_ref, l_ref, acc_ref, qbd_ref, md_ref, ld_ref, accd_ref, *, lam_init):
    step = pl.program_id(1)
    i = pi_ref[step]
    j = pj_ref[step]
    tq = ob_ref.shape[0]
    nblk = 2 * H_C

    @pl.when(j == 0)
    def _():
        m_ref[...] = jnp.full_like(m_ref, NEG_INF)
        l_ref[...] = jnp.zeros_like(l_ref)
        acc_ref[...] = jnp.zeros_like(acc_ref)
        q = qc_ref[...] * (DIFF_SCALE * LOG2E)
        feat = lax.broadcasted_iota(jnp.int32, q.shape, 0)
        for blk in range(nblk):
            lo = (blk // 2) * DV_C + (blk % 2) * DC
            qbd_ref[:, blk * tq:(blk + 1) * tq] = jnp.where((feat >= lo) & (feat < lo + DC), q, 0.0).astype(BF16)
        md_ref[...] = jnp.full_like(md_ref, NEG_INF)
        ld_ref[...] = jnp.zeros_like(ld_ref)
        accd_ref[...] = jnp.zeros_like(accd_ref)

    k = k_ref[...].astype(BF16)
    ct = ct_ref[...].astype(BF16)
    mask = mask_ref[jnp.minimum(i - j, 1)]
    heads = range(H_B)
    cols = [slice(h * tq, (h + 1) * tq) for h in heads]
    kc = kc_ref[...].astype(BF16)
    vt = vt_ref[...].astype(BF16)
    tile = jnp.minimum(i - j, 2)
    blocks = range(nblk)
    dcols = [slice(blk * tq, (blk + 1) * tq) for blk in blocks]
    s_b = [_dot(k, q_ref[:, cols[h]]) + mask for h in heads]
    s_d = [_dot(kc, qbd_ref[:, dcols[blk]]) + bias_ref[tile, blk // 2] for blk in blocks]
    ap_b = [_softmax_update_t(s_b[h], m_ref, l_ref, cols[h]) for h in heads]
    ap_d = [_softmax_update_t(s_d[blk], md_ref, ld_ref, dcols[blk]) for blk in blocks]
    pv_b = [_dot(ct, ap_b[h][1].astype(BF16)) for h in heads]
    pv_d = [_dot(vt[(blk // 2) * DV_C:(blk // 2 + 1) * DV_C], ap_d[blk][1].astype(BF16)) for blk in blocks]
    for h in heads:
        acc_ref[:, cols[h]] = ap_b[h][0] * acc_ref[:, cols[h]] + pv_b[h]
    for blk in blocks:
        accd_ref[blk] = ap_d[blk][0] * accd_ref[blk] + pv_d[blk]

    @pl.when(j == i)
    def _():
        outs = []
        for h in heads:
            o_lat = (acc_ref[:, cols[h]] / l_ref[:, cols[h]]).astype(BF16)
            outs.append(_dot(wuvt_ref[h], o_lat))
        o = jnp.concatenate(outs, axis=0).T
        ob_ref[...] = (o * _silu(gb_ref[...])).astype(ob_ref.dtype)
        lam = _diff_lambda(lam_ref, lam_init)
        outs = []
        for h in range(H_C):
            o = (accd_ref[2 * h] / ld_ref[:, dcols[2 * h]]
                 - lam * (accd_ref[2 * h + 1] / ld_ref[:, dcols[2 * h + 1]]))
            o = o * lax.rsqrt(jnp.mean(o * o, axis=0, keepdims=True) + SUBLN_EPS) * sg_ref[...]
            outs.append(o * (1.0 - lam_init))
        o = jnp.concatenate(outs, axis=0).T
        oc_ref[...] = (o * _silu(gc_ref[...])).astype(oc_ref.dtype)


def _prompt_attn(q_t, rows_pad, rows_t, mask, gb, wuv_t, qc_t, kc, vc_t, gc, bias_tiles, lam_vecs, subln_g, lam_init):
    b, t, _ = kc.shape
    tq = ATTN_TILE
    nq = t // tq
    pairs = [(i, j) for i in range(nq) for j in range(i + 1)]
    pair_i = jnp.asarray([p[0] for p in pairs], jnp.int32)
    pair_j = jnp.asarray([p[1] for p in pairs], jnp.int32)
    nblk = 2 * H_C
    q_tile = lambda w: pl.BlockSpec((None, tq, w), lambda bi, s, pi, pj: (bi, pi[s], 0))
    k_tile = lambda w: pl.BlockSpec((None, tq, w), lambda bi, s, pi, pj: (bi, pj[s], 0))
    qt_tile = lambda w: pl.BlockSpec((None, w, tq), lambda bi, s, pi, pj: (bi, 0, pi[s]))
    kt_tile = lambda w: pl.BlockSpec((None, w, tq), lambda bi, s, pi, pj: (bi, 0, pj[s]))
    const = lambda shape: pl.BlockSpec(shape, lambda bi, s, pi, pj: (0,) * len(shape))
    grid_spec = pltpu.PrefetchScalarGridSpec(
        num_scalar_prefetch=2,
        grid=(b, len(pairs)),
        in_specs=[pl.BlockSpec((None, 2 * LANES, H_B * tq), lambda bi, s, pi, pj: (bi * nq + pi[s], 0, 0)),
                  k_tile(2 * LANES), kt_tile(KV_RANK), const(mask.shape), q_tile(D_B), const(wuv_t.shape),
                  qt_tile(D_C), k_tile(D_C), kt_tile(D_C), q_tile(D_C), const(bias_tiles.shape),
                  const(lam_vecs.shape), const((DV_C, 1))],
        out_specs=[q_tile(D_B), q_tile(D_C)],
        scratch_shapes=[pltpu.VMEM((1, H_B * tq), F32), pltpu.VMEM((1, H_B * tq), F32),
                        pltpu.VMEM((KV_RANK, H_B * tq), F32),
                        pltpu.VMEM((D_C, nblk * tq), BF16), pltpu.VMEM((1, nblk * tq), F32),
                        pltpu.VMEM((1, nblk * tq), F32), pltpu.VMEM((nblk, DV_C, tq), F32)],
    )
    return pl.pallas_call(
        functools.partial(_prompt_attn_kernel, lam_init=lam_init),
        grid_spec=grid_spec,
        out_shape=[jax.ShapeDtypeStruct((b, t, D_B), BF16), jax.ShapeDtypeStruct((b, t, D_C), BF16)],
        compiler_params=_cparams(("parallel", "arbitrary")),
        name="prompt_attn",
    )(pair_i, pair_j, q_t, rows_pad, rows_t, mask, gb, wuv_t, qc_t, kc, vc_t, gc, bias_tiles, lam_vecs,
      subln_g.reshape(DV_C, 1))


def _softmax_step(s, m, l):
    m_new = jnp.maximum(m, jnp.max(s, axis=-1, keepdims=True))
    alpha = jnp.exp(m - m_new)
    p = jnp.exp(s - m_new)
    return m_new, alpha * l + jnp.sum(p, axis=-1, keepdims=True), alpha, p


def _decode_attn_kernel(pt_ref, q_ref, row_ref, gb_ref, wuv_ref, qc_ref, kn_ref, vn_ref, gc_ref, bias_ref, lam_ref,
                        sg_ref, cm_hbm, ck_hbm, cv_hbm, ob_ref, oc_ref, mbuf, kbuf, vbuf, sems,
                        *, layer, pages_per_seq, lam_init):
    n_pages = DECODE_PAGES_PER_CHUNK
    seq = pl.program_id(0)
    chunks = pages_per_seq // n_pages
    total = pl.num_programs(0) * chunks

    def chunk_copies(g, slot):
        out = []
        for p in range(n_pages):
            page = pt_ref[g * n_pages + p]
            out.append(pltpu.make_async_copy(cm_hbm.at[layer, page], mbuf.at[slot, p], sems.at[slot, 0]))
            out.append(pltpu.make_async_copy(ck_hbm.at[layer, page], kbuf.at[slot, p], sems.at[slot, 1]))
            out.append(pltpu.make_async_copy(cv_hbm.at[layer, page], vbuf.at[slot, p], sems.at[slot, 2]))
        return out

    @pl.when(seq == 0)
    def _():
        for cp in chunk_copies(0, 0):
            cp.start()

    q = q_ref[...]
    qbd = _diff_queries(qc_ref[...])
    qb = qbd.astype(BF16)

    def chunk_body(c, carry):
        mb, lb, accb, md, ld, accd = carry
        g = seq * chunks + c
        slot = lax.rem(g, 2)

        @pl.when(g + 1 < total)
        def _():
            for cp in chunk_copies(g + 1, 1 - slot):
                cp.start()

        for cp in chunk_copies(g, slot):
            cp.wait()

        pages = range(n_pages)
        lanes = [slice(p * PAGE_SIZE, (p + 1) * PAGE_SIZE) for p in pages]
        ks = [mbuf[slot, p].astype(BF16) for p in pages]
        s_b = jnp.concatenate([_dot(q, ks[p]) for p in pages], axis=1)
        s_d = jnp.concatenate([_dot(qb, kbuf[slot, p].astype(BF16)) for p in pages], axis=1)
        s_d = s_d + bias_ref[jnp.where(c == chunks - 1, 1, 0)]
        mb, lb, alpha_b, p_b = _softmax_step(s_b, mb, lb)
        md, ld, alpha_d, p_d = _softmax_step(s_d, md, ld)
        p_b = p_b.astype(BF16)
        p_d = p_d.astype(BF16)
        pv_b = [_dot_nt(p_b[:, lanes[p]], ks[p][0:KV_RANK, :]) for p in pages]
        pv_d = [_dot_nt(p_d[:, lanes[p]], vbuf[slot, p].astype(BF16)) for p in pages]
        accb = alpha_b * accb + functools.reduce(lambda x, y: x + y, pv_b)
        accd = alpha_d * accd + functools.reduce(lambda x, y: x + y, pv_d)
        return mb, lb, accb, md, ld, accd

    init = (jnp.full((H_B, 1), NEG_INF, F32), jnp.zeros((H_B, 1), F32), jnp.zeros((H_B, KV_RANK), F32),
            jnp.full((2 * H_C, 1), NEG_INF, F32), jnp.zeros((2 * H_C, 1), F32), jnp.zeros((2 * H_C, D_C), F32))
    mb, lb, accb, md, ld, accd = lax.fori_loop(0, chunks, chunk_body, init)

    row = row_ref[...]
    s_new = jnp.sum(q.astype(F32) * row, axis=-1, keepdims=True)
    mb, lb, alpha, p_new = _softmax_step(s_new, mb, lb)
    accb = alpha * accb + p_new * row[:, 0:KV_RANK]
    _mla_out(accb / lb, wuv_ref, gb_ref[...], ob_ref, 1)

    s_new = jnp.sum(qbd * kn_ref[...], axis=-1, keepdims=True) + bias_ref[2][:, 0:1]
    md, ld, alpha, p_new = _softmax_step(s_new, md, ld)
    accd = alpha * accd + p_new * vn_ref[...]
    _diff_out(accd, ld, _diff_lambda(lam_ref, lam_init), lam_init, sg_ref[...], gc_ref[...], oc_ref, 1)


def _decode_attn(page_table_flat, q, rows_new, gb, wuv, qc, kc, vc, gc, bias_rows, lam_vecs, subln_g,
                 cache_mla, cache_k, cache_v, layer, pages_per_seq, lam_init):
    b = q.shape[0]
    n_pages = DECODE_PAGES_PER_CHUNK
    assert pages_per_seq % n_pages == 0
    per_b = lambda shape: pl.BlockSpec((None,) + shape, lambda bi, pt: (bi,) + (0,) * len(shape))
    const = lambda shape: pl.BlockSpec(shape, lambda bi, pt: (0,) * len(shape))
    hbm = pl.BlockSpec(memory_space=pl.ANY)
    grid_spec = pltpu.PrefetchScalarGridSpec(
        num_scalar_prefetch=1,
        grid=(b,),
        in_specs=[per_b((H_B, MLA_W)), per_b((1, MLA_W)), per_b((1, D_B)), const(wuv.shape),
                  per_b((1, D_C)), per_b((1, D_C)), per_b((1, D_C)), per_b((1, D_C)), const(bias_rows.shape),
                  const(lam_vecs.shape), const((1, DV_C)), hbm, hbm, hbm],
        out_specs=[per_b((1, D_B)), per_b((1, D_C))],
        scratch_shapes=[pltpu.VMEM((2, n_pages, MLA_W, PAGE_SIZE), F32),
                        pltpu.VMEM((2, n_pages, D_C, PAGE_SIZE), F32),
                        pltpu.VMEM((2, n_pages, D_C, PAGE_SIZE), F32),
                        pltpu.SemaphoreType.DMA((2, 3))],
    )
    return pl.pallas_call(
        functools.partial(_decode_attn_kernel, layer=layer, pages_per_seq=pages_per_seq, lam_init=lam_init),
        grid_spec=grid_spec,
        out_shape=[jax.ShapeDtypeStruct((b, 1, D_B), BF16), jax.ShapeDtypeStruct((b, 1, D_C), BF16)],
        compiler_params=_cparams(("arbitrary",)),
        name="decode_attn",
    )(page_table_flat, q, rows_new, gb, wuv, qc, kc, vc, gc, bias_rows, lam_vecs, subln_g.reshape(1, DV_C),
      cache_mla, cache_k, cache_v)


def _permute_w_in(w):
    o_ckv = A_IN + D_A + Q_RANK
    o_kr = o_ckv + KV_RANK
    o_gb = o_kr + ROPE_B
    half = ROPE_B // 2
    pad = jnp.zeros((w.shape[0], 2 * LANES - KV_RANK - 2 * ROPE_B), w.dtype)
    out = jnp.concatenate([w[:, :o_gb], w[:, o_kr + half:o_gb], w[:, o_kr:o_kr + half], pad, w[:, o_gb:]], axis=1)
    assert out.shape[1] == IN_COLS_PERM
    return out.astype(BF16)


def _extend_w_uq(w):
    w = w.reshape(Q_RANK, H_B, NOPE_B + ROPE_B)
    half = ROPE_B // 2
    rope = w[:, :, NOPE_B:]
    swapped = jnp.concatenate([rope[:, :, half:], rope[:, :, :half]], axis=-1)
    return jnp.concatenate([w, swapped], axis=-1).reshape(Q_RANK, H_B * LANES).astype(BF16)


def _rope_tables(pos):
    inv = ROPE_THETA ** (-jnp.arange(0, ROPE_B, 2, dtype=F32) / ROPE_B)
    ang = pos.astype(F32)[:, None] * inv[None, :]
    cos, sin = jnp.cos(ang), jnp.sin(ang)
    return jnp.concatenate([cos, cos], axis=-1), jnp.concatenate([-sin, sin], axis=-1)


def _layer_weights(l, W):
    row = lambda a: a.reshape(1, -1)
    rwkv = (row(W["mu_shift"][l]), row(W["rw_w0"][l]), W["rw_w2"][l], row(W["rw_a0"][l]), W["rw_a2"][l],
            row(W["rw_k_k"][l]), row(W["rw_k_a"][l]), row(W["rw_r_k"][l]), row(W["rw_gn_g"][l]),
            row(W["rw_gn_b"][l]))
    col = lambda a: a.reshape(-1, 1)
    rwkv_t = (col(W["mu_shift"][l]), col(W["rw_w0"][l]), W["rw_w2"][l].T, col(W["rw_a0"][l]), W["rw_a2"][l].T,
              col(W["rw_k_k"][l]), col(W["rw_k_a"][l]), col(W["rw_r_k"][l]), col(W["rw_gn_g"][l]),
              col(W["rw_gn_b"][l]))
    w_in = _permute_w_in(W["w_in"][l])
    wuq = _extend_w_uq(W["mla_w_uq"][l])
    return dict(
        w_in_n=jnp.concatenate([w_in[:, a:b] for a, b in PROMPT_SEGS_N], axis=1),
        w_in_t=jnp.concatenate([w_in[:, a:b] for a, b in PROMPT_SEGS_T], axis=1).T,
        w_in_sn=jnp.concatenate([w_in[:, a:b] for a, b in SAMPLE_SEGS_N], axis=1),
        w_in_st=jnp.concatenate([w_in[:, a:b] for a, b in SAMPLE_SEGS_T], axis=1).T,
        rwkv=rwkv,
        rwkv_t=rwkv_t,
        wuq=wuq,
        wuq_t=wuq.T,
        wuk=jnp.transpose(W["mla_w_uk"][l], (1, 0, 2)).astype(BF16),
        wuv_t=jnp.transpose(W["mla_w_uv"][l], (1, 2, 0)).astype(BF16),
        wuk_t=jnp.transpose(W["mla_w_uk"][l], (1, 2, 0)).astype(BF16),
        wuv=jnp.transpose(W["mla_w_uv"][l], (1, 0, 2)).astype(BF16),
        lam_vecs=jnp.stack([W["diff_lam_q1"][l], W["diff_lam_k1"][l], W["diff_lam_q2"][l], W["diff_lam_k2"][l]]),
        wo=W["w_out"][l].astype(BF16),
        wple=W["w_ple"][l].astype(BF16),
        wg=W["w_ple_gate"][l].astype(BF16),
    )


def _run_prompt(x, p, W, LW, depth):
    b, t, d = x.shape
    m = b * t
    tm = ATTN_TILE
    cos2, sin2 = _rope_tables(jnp.arange(t, dtype=jnp.int32))
    cos2_t, sin2_t = cos2.T, sin2.T
    tile = jnp.arange(ATTN_TILE, dtype=jnp.int32)
    dist = (jnp.arange(3, dtype=jnp.int32)[:, None, None] * ATTN_TILE + tile[None, None, :] - tile[None, :, None])
    bias_tiles = _bias_tiles(W["rel_bias"], dist) * LOG2E
    causal = jnp.stack([jnp.where(dist[0] >= 0, 0.0, NEG_INF).astype(F32), jnp.zeros(dist.shape[1:], F32)])
    uprev0 = jnp.zeros((b, 1, A_IN), F32)
    s0 = jnp.zeros((b, H_A, HEAD_A, HEAD_A), F32)
    segs_n = _pack_segments(PROMPT_SEGS_N)
    segs_t = _pack_segments(PROMPT_SEGS_T)
    h = x.reshape(m, d)
    mla_rows, k_rows, v_rows, wkv_out, shift_out = [], [], [], [], []
    for l in range(depth):
        lw = LW[l]
        u, ga, gb, gc, kc, cq_t, ckv_t, qc_t, kc_t, vc_t, xn_last = _inproj(
            h, W["norm_g"][l], lw["w_in_n"], segs_n, normalize=True, rows_per_seq=t, tm=tm,
            wt_bf16=lw["w_in_t"], segs_t=segs_t, seg_dtypes=PROMPT_DTYPES_N)
        y_a, s_new = _rwkv_chunked(u.reshape(b, t, A_IN), uprev0, ga.reshape(b, t, D_A), s0, lw["rwkv"])
        q_t, rows_pad, rows_t = _mla_prep_t(cq_t, ckv_t, cos2_t, sin2_t, W["mla_q_norm_g"][l], lw["wuq_t"],
                                            W["mla_kv_norm_g"][l], lw["wuk"], tm=tm)
        lam_init = 0.8 - 0.6 * math.exp(-0.3 * l)
        y_b, y_c = _prompt_attn(q_t, rows_pad, rows_t, causal, gb.reshape(b, t, D_B), lw["wuv_t"],
                                qc_t, kc.reshape(b, t, D_C), vc_t, gc.reshape(b, t, D_C), bias_tiles,
                                lw["lam_vecs"], W["diff_subln_g"][l], lam_init)
        h = _outproj(h, y_a.reshape(m, D_A), y_b.reshape(m, D_B), y_c.reshape(m, D_C), p[l].reshape(m, PLE_DIM),
                     lw["wo"], lw["wple"], lw["wg"], W["final_norm_g"], final=(l == depth - 1), tm=OUTPROJ_TILE)
        mla_rows.append(jnp.transpose(rows_t, (0, 2, 1)))
        k_rows.append(jnp.transpose(kc_t.reshape(b, H_C, 2 * DC, t), (0, 3, 1, 2)))
        v_rows.append(jnp.transpose(vc_t.reshape(b, H_C, DV_C, t), (0, 3, 1, 2)))
        wkv_out.append(s_new)
        shift_out.append(xn_last.reshape(b, d))
    return (h.reshape(b, t, d), jnp.stack(mla_rows), jnp.stack(k_rows), jnp.stack(v_rows), jnp.stack(wkv_out),
            jnp.stack(shift_out))


def _run_sample(x, p, state_shift, state_wkv, cache_mla, cache_k, cache_v, page_table, W, LW, depth):
    b, t, d = x.shape
    assert t == 1
    pages_per_seq = page_table.shape[1]
    past_len = pages_per_seq * PAGE_SIZE
    tm = b
    cos2, sin2 = _rope_tables(jnp.full((b,), past_len, dtype=jnp.int32))
    pt_flat = page_table.reshape(-1).astype(jnp.int32)
    step_keys = DECODE_PAGES_PER_CHUNK * PAGE_SIZE
    key_in_step = jnp.arange(step_keys, dtype=jnp.int32)
    dist = jnp.stack([past_len - key_in_step,
                      past_len - (past_len - step_keys + key_in_step),
                      jnp.zeros((step_keys,), jnp.int32)])
    dist = jnp.broadcast_to(dist[:, None, :], (3, 2, step_keys))
    bias = _bias_tiles(W["rel_bias"], dist)
    bias_rows = bias.reshape(3, 2 * H_C, step_keys)
    to_feature_major = lambda c: jnp.transpose(c, (0, 1, 3, 4, 2)).reshape(c.shape[:2] + (D_C, PAGE_SIZE))
    cache_k2 = to_feature_major(cache_k)
    cache_v2 = to_feature_major(cache_v)
    cache_mla_t = jnp.transpose(cache_mla, (0, 1, 3, 2))
    segs_n = _pack_segments(SAMPLE_SEGS_N)
    segs_t = _pack_segments(SAMPLE_SEGS_T)
    h = x.reshape(b, d)
    mla_rows, k_rows, v_rows, wkv_out, shift_out = [], [], [], [], []
    for l in range(depth):
        lw = LW[l]
        cq, ckv, gb, qc, kc, vc, gc, u_t, ga_t, xn = _inproj(
            h, W["norm_g"][l], lw["w_in_sn"], segs_n, normalize=True, rows_per_seq=1, tm=tm,
            wt_bf16=lw["w_in_st"], segs_t=segs_t)
        (uprev_t,) = _inproj(state_shift[l], W["norm_g"][l], None, (), normalize=False, rows_per_seq=1, tm=tm,
                             wt_bf16=lw["w_in_st"][:A_IN], segs_t=(SEG_U,))
        y_a_t, s_new_t = _rwkv_step(u_t[0], uprev_t[0], ga_t[0], jnp.transpose(state_wkv[l], (1, 2, 3, 0)),
                                    lw["rwkv_t"])
        y_a = y_a_t.T
        s_new = jnp.transpose(s_new_t, (3, 0, 1, 2))
        q, rows = _mla_prep(cq, ckv, cos2, sin2, W["mla_q_norm_g"][l], lw["wuq"], W["mla_kv_norm_g"][l],
                            lw["wuk_t"], tm=tm, pos_tiles=1)
        lam_init = 0.8 - 0.6 * math.exp(-0.3 * l)
        y_b, y_c = _decode_attn(pt_flat, jnp.transpose(q, (1, 0, 2)), rows.reshape(b, 1, MLA_W),
                                gb.reshape(b, 1, D_B), lw["wuv"], qc.reshape(b, 1, D_C), kc.reshape(b, 1, D_C),
                                vc.reshape(b, 1, D_C), gc.reshape(b, 1, D_C), bias_rows, lw["lam_vecs"],
                                W["diff_subln_g"][l], cache_mla_t, cache_k2, cache_v2, l, pages_per_seq, lam_init)
        h = _outproj(h, y_a, y_b.reshape(b, D_B), y_c.reshape(b, D_C), p[l].reshape(b, PLE_DIM),
                     lw["wo"], lw["wple"], lw["wg"], W["final_norm_g"], final=(l == depth - 1), tm=tm)
        mla_rows.append(rows.reshape(b, 1, MLA_W))
        k_rows.append(kc.reshape(b, 1, H_C, 2 * DC))
        v_rows.append(vc.reshape(b, 1, H_C, DV_C))
        wkv_out.append(s_new)
        shift_out.append(xn)
    return (h.reshape(b, 1, d), jnp.stack(mla_rows), jnp.stack(k_rows), jnp.stack(v_rows), jnp.stack(wkv_out),
            jnp.stack(shift_out))


def kernel(x_prompt, x_sample, cache_mla, cache_diff_k, cache_diff_v, state_wkv, state_shift, page_table,
           p_prompt, p_sample, norm_g, w_in, mu_shift, rw_w0, rw_w2, rw_a0, rw_a2, rw_k_k, rw_k_a, rw_r_k,
           rw_gn_g, rw_gn_b, mla_q_norm_g, mla_w_uq, mla_kv_norm_g, mla_w_uk, mla_w_uv, diff_lam_q1,
           diff_lam_k1, diff_lam_q2, diff_lam_k2, diff_subln_g, rel_bias, w_out, w_ple, w_ple_gate,
           final_norm_g):
    W = {"norm_g": norm_g, "w_in": w_in, "mu_shift": mu_shift, "rw_w0": rw_w0, "rw_w2": rw_w2, "rw_a0": rw_a0,
         "rw_a2": rw_a2, "rw_k_k": rw_k_k, "rw_k_a": rw_k_a, "rw_r_k": rw_r_k, "rw_gn_g": rw_gn_g,
         "rw_gn_b": rw_gn_b, "mla_q_norm_g": mla_q_norm_g, "mla_w_uq": mla_w_uq, "mla_kv_norm_g": mla_kv_norm_g,
         "mla_w_uk": mla_w_uk, "mla_w_uv": mla_w_uv, "diff_lam_q1": diff_lam_q1, "diff_lam_k1": diff_lam_k1,
         "diff_lam_q2": diff_lam_q2, "diff_lam_k2": diff_lam_k2, "diff_subln_g": diff_subln_g,
         "rel_bias": rel_bias, "w_out": w_out, "w_ple": w_ple, "w_ple_gate": w_ple_gate,
         "final_norm_g": final_norm_g}
    depth = w_in.shape[0]
    LW = [_layer_weights(l, W) for l in range(depth)]
    y_p, mla_p, dk_p, dv_p, wkv_p, sh_p = _run_prompt(x_prompt, p_prompt, W, LW, depth)
    y_s, mla_s, dk_s, dv_s, wkv_s, sh_s = _run_sample(x_sample, p_sample, state_shift, state_wkv, cache_mla,
                                                      cache_diff_k, cache_diff_v, page_table, W, LW, depth)
    return (y_p, y_s, mla_p, mla_s, dk_p, dk_s, dv_p, dv_s, wkv_p, wkv_s, sh_p, sh_s)
```
